```python
import jax, jax.numpy as jnp
from jax import lax
import numpy as np

D_MODEL = 2048
BATCH = 8
SEQ = 8192
DEPTH = 4

CHUNK = 64
N_MIXERS = 2
N_LAYERS_A = (DEPTH + 1) // 2
N_LAYERS_B = DEPTH // 2

GM_BLOCK = 128
GM_GROUPS = 8
GM_WIDTH = D_MODEL
GM_GROUP_DIM = GM_WIDTH // GM_GROUPS

GLA_HEADS = 4
GLA_KEY_DIM = D_MODEL // 2
GLA_VAL_DIM = D_MODEL
GLA_HEAD_K = GLA_KEY_DIM // GLA_HEADS
GLA_HEAD_V = GLA_VAL_DIM // GLA_HEADS
GLA_GATE_RANK = 16
GLA_GATE_TAU = 16.0

FFN_DIM = 4 * D_MODEL

EPS = 1e-6

kernel_name = "hybrid_gmlp_gla_sqrelu_streaming_encoder"


def rms_norm(x, g):
    xf = x.astype(jnp.float32)
    y = xf * lax.rsqrt(jnp.mean(xf * xf, axis=-1, keepdims=True) + EPS)
    return (y * g.astype(jnp.float32)).astype(x.dtype)


def layer_norm(x, g, b):
    xf = x.astype(jnp.float32)
    mu = jnp.mean(xf, axis=-1, keepdims=True)
    xc = xf - mu
    y = xc * lax.rsqrt(jnp.mean(xc * xc, axis=-1, keepdims=True) + EPS)
    return (y * g.astype(jnp.float32) + b.astype(jnp.float32)).astype(x.dtype)


def gmlp_mixer(h, w_in, ln_g, ln_b, w_s, b_s, w_out):
    bsz, seq, _ = h.shape
    z = jax.nn.gelu(h @ w_in, approximate=False)
    u, v = jnp.split(z, 2, axis=-1)
    v = layer_norm(v, ln_g, ln_b)
    v = v.reshape(bsz, seq // GM_BLOCK, GM_BLOCK, GM_GROUPS, GM_GROUP_DIM)
    chunk_id = jnp.arange(GM_BLOCK) // CHUNK
    mask = chunk_id[None, :] <= chunk_id[:, None]
    w = jnp.where(mask[None], w_s, jnp.zeros((), w_s.dtype))
    mixed = jnp.einsum('gij,bnjgc->bnigc', w, v) + b_s.T[None, None, :, :, None]
    gated = u * mixed.reshape(bsz, seq, GM_WIDTH)
    return gated @ w_out


def gla_mixer(h, w_in, w_a1, w_a2, b_a, norm_g, w_o):
    bsz, seq, _ = h.shape
    n_chunks = seq // CHUNK
    f32 = jnp.float32
    proj = h @ w_in
    q, k, v, r = jnp.split(proj, [GLA_KEY_DIM, 2 * GLA_KEY_DIM, 2 * GLA_KEY_DIM + GLA_VAL_DIM], axis=-1)
    logit = ((h @ w_a1) @ w_a2 + b_a).astype(f32)
    log_alpha = jax.nn.log_sigmoid(logit) / GLA_GATE_TAU

    def to_chunks(t, dh):
        return t.astype(f32).reshape(bsz, n_chunks, CHUNK, GLA_HEADS, dh).transpose(1, 0, 3, 2, 4)

    qc = to_chunks(q, GLA_HEAD_K) * (GLA_HEAD_K ** -0.5)
    kc = to_chunks(k, GLA_HEAD_K)
    vc = to_chunks(v, GLA_HEAD_V)
    gc = lax.cumsum(to_chunks(log_alpha, GLA_HEAD_K), axis=3)

    def step(state, inp):
        q_t, k_t, v_t, g_t = inp
        g_end = g_t[:, :, -1:, :]
        k_dec = k_t * jnp.exp(g_end - g_t)
        state = jnp.exp(g_end[:, :, 0, :])[..., None] * state + jnp.einsum('bhck,bhcv->bhkv', k_dec, v_t)
        out = jnp.einsum('bhck,bhkv->bhcv', q_t, state)
        return state, out

    s0 = jnp.zeros((bsz, GLA_HEADS, GLA_HEAD_K, GLA_HEAD_V), f32)
    _, o = lax.scan(step, s0, (qc, kc, vc, gc))
    o = o.transpose(1, 0, 3, 2, 4).reshape(bsz, seq, GLA_HEADS, GLA_HEAD_V)
    o = o * lax.rsqrt(jnp.mean(o * o, axis=-1, keepdims=True) + EPS)
    o = o.reshape(bsz, seq, GLA_VAL_DIM) * norm_g.astype(f32)
    o = o.astype(h.dtype) * jax.nn.silu(r)
    return o @ w_o


def squared_relu_mlp(h, w_up, w_down):
    a = jax.nn.relu(h @ w_up)
    return (a * a) @ w_down


def _fwd_setup_inputs(seed: int = 0) -> dict:
    key = jax.random.key(seed)
    ks = jax.random.split(key, 20)
    f32 = jnp.float32

    def nrm(k, shape, fan_in):
        return jax.random.normal(k, shape, f32) * (fan_in ** -0.5)

    def gain(k, shape):
        return 1.0 + 0.02 * jax.random.normal(k, shape, f32)

    def small(k, shape):
        return 0.01 * jax.random.normal(k, shape, f32)

    return {
        "x": jax.random.normal(ks[0], (BATCH, SEQ, D_MODEL), f32),
        "norm_mix_g": gain(ks[1], (DEPTH, D_MODEL)),
        "norm_ffn_g": gain(ks[2], (DEPTH, D_MODEL)),
        "final_g": gain(ks[3], (D_MODEL,)),
        "gm_w_in": nrm(ks[4], (N_LAYERS_A, D_MODEL, 2 * GM_WIDTH), D_MODEL),
        "gm_ln_g": gain(ks[5], (N_LAYERS_A, GM_WIDTH)),
        "gm_ln_b": small(ks[6], (N_LAYERS_A, GM_WIDTH)),
        "gm_w_s": nrm(ks[7], (N_LAYERS_A, GM_GROUPS, GM_BLOCK, GM_BLOCK), GM_BLOCK),
        "gm_b_s": 1.0 + small(ks[8], (N_LAYERS_A, GM_GROUPS, GM_BLOCK)),
        "gm_w_out": nrm(ks[9], (N_LAYERS_A, GM_WIDTH, D_MODEL), GM_WIDTH),
        "gla_w_in": nrm(ks[10], (N_LAYERS_B, D_MODEL, 2 * GLA_KEY_DIM + 2 * GLA_VAL_DIM), D_MODEL),
        "gla_w_a1": nrm(ks[11], (N_LAYERS_B, D_MODEL, GLA_GATE_RANK), D_MODEL),
        "gla_w_a2": nrm(ks[12], (N_LAYERS_B, GLA_GATE_RANK, GLA_KEY_DIM), GLA_GATE_RANK),
        "gla_b_a": small(ks[13], (N_LAYERS_B, GLA_KEY_DIM)),
        "gla_norm_g": gain(ks[14], (N_LAYERS_B, GLA_VAL_DIM)),
        "gla_w_o": nrm(ks[15], (N_LAYERS_B, GLA_VAL_DIM, D_MODEL), GLA_VAL_DIM),
        "ffn_w_up": nrm(ks[16], (DEPTH, D_MODEL, FFN_DIM), D_MODEL),
        "ffn_w_down": nrm(ks[17], (DEPTH, FFN_DIM, D_MODEL), FFN_DIM),
    }


def _fwd_reference(x, norm_mix_g, norm_ffn_g, final_g, gm_w_in, gm_ln_g, gm_ln_b, gm_w_s, gm_b_s, gm_w_out,
              gla_w_in, gla_w_a1, gla_w_a2, gla_b_a, gla_norm_g, gla_w_o, ffn_w_up, ffn_w_down):
    for i in range(DEPTH):
        h = rms_norm(x, norm_mix_g[i])
        j = i // N_MIXERS
        if i % N_MIXERS == 0:
            x = x + gmlp_mixer(h, gm_w_in[j], gm_ln_g[j], gm_ln_b[j], gm_w_s[j], gm_b_s[j], gm_w_out[j])
        else:
            x = x + gla_mixer(h, gla_w_in[j], gla_w_a1[j], gla_w_a2[j], gla_b_a[j], gla_norm_g[j], gla_w_o[j])
        h = rms_norm(x, norm_ffn_g[i])
        x = x + squared_relu_mlp(h, ffn_w_up[i], ffn_w_down[i])
    return rms_norm(x, final_g)


import jax as _jax
import jax.numpy as _jnp

TWIN_FORMAT = 'train_step'
FWD_PARAMS = ['x', 'norm_mix_g', 'norm_ffn_g', 'final_g', 'gm_w_in', 'gm_ln_g', 'gm_ln_b', 'gm_w_s', 'gm_b_s', 'gm_w_out', 'gla_w_in', 'gla_w_a1', 'gla_w_a2', 'gla_b_a', 'gla_norm_g', 'gla_w_o', 'ffn_w_up', 'ffn_w_down']
TWIN_WEIGHTS = ['norm_mix_g', 'norm_ffn_g', 'final_g', 'gm_w_in', 'gm_ln_g', 'gm_ln_b', 'gm_w_s', 'gm_b_s', 'gm_w_out', 'gla_w_in', 'gla_w_a1', 'gla_w_a2', 'gla_b_a', 'gla_norm_g', 'gla_w_o', 'ffn_w_up', 'ffn_w_down']
TWIN_DIFF_INPUT = 'x'
TWIN_INPUTS = ['x', 'norm_mix_g', 'norm_ffn_g', 'final_g', 'gm_w_in', 'gm_ln_g', 'gm_ln_b', 'gm_w_s', 'gm_b_s', 'gm_w_out', 'gla_w_in', 'gla_w_a1', 'gla_w_a2', 'gla_b_a', 'gla_norm_g', 'gla_w_o', 'ffn_w_up', 'ffn_w_down', 'loss_target', 'm_norm_mix_g', 'm_norm_ffn_g', 'm_final_g', 'm_gm_w_in', 'm_gm_ln_g', 'm_gm_ln_b', 'm_gm_w_s', 'm_gm_b_s', 'm_gm_w_out', 'm_gla_w_in', 'm_gla_w_a1', 'm_gla_w_a2', 'm_gla_b_a', 'm_gla_norm_g', 'm_gla_w_o', 'm_ffn_w_up', 'm_ffn_w_down', 'v_norm_mix_g', 'v_norm_ffn_g', 'v_final_g', 'v_gm_w_in', 'v_gm_ln_g', 'v_gm_ln_b', 'v_gm_w_s', 'v_gm_b_s', 'v_gm_w_out', 'v_gla_w_in', 'v_gla_w_a1', 'v_gla_w_a2', 'v_gla_b_a', 'v_gla_norm_g', 'v_gla_w_o', 'v_ffn_w_up', 'v_ffn_w_down']
TWIN_OUTPUTS = ['loss', 'grad_x', 'grad_norm_mix_g', 'grad_norm_ffn_g', 'grad_final_g', 'grad_gm_w_in', 'grad_gm_ln_g', 'grad_gm_ln_b', 'grad_gm_w_s', 'grad_gm_b_s', 'grad_gm_w_out', 'grad_gla_w_in', 'grad_gla_w_a1', 'grad_gla_w_a2', 'grad_gla_b_a', 'grad_gla_norm_g', 'grad_gla_w_o', 'grad_ffn_w_up', 'grad_ffn_w_down', 'delta_norm_mix_g', 'delta_norm_ffn_g', 'delta_final_g', 'delta_gm_w_in', 'delta_gm_ln_g', 'delta_gm_ln_b', 'delta_gm_w_s', 'delta_gm_b_s', 'delta_gm_w_out', 'delta_gla_w_in', 'delta_gla_w_a1', 'delta_gla_w_a2', 'delta_gla_b_a', 'delta_gla_norm_g', 'delta_gla_w_o', 'delta_ffn_w_up', 'delta_ffn_w_down', 'new_m_norm_mix_g', 'new_m_norm_ffn_g', 'new_m_final_g', 'new_m_gm_w_in', 'new_m_gm_ln_g', 'new_m_gm_ln_b', 'new_m_gm_w_s', 'new_m_gm_b_s', 'new_m_gm_w_out', 'new_m_gla_w_in', 'new_m_gla_w_a1', 'new_m_gla_w_a2', 'new_m_gla_b_a', 'new_m_gla_norm_g', 'new_m_gla_w_o', 'new_m_ffn_w_up', 'new_m_ffn_w_down', 'new_v_norm_mix_g', 'new_v_norm_ffn_g', 'new_v_final_g', 'new_v_gm_w_in', 'new_v_gm_ln_g', 'new_v_gm_ln_b', 'new_v_gm_w_s', 'new_v_gm_b_s', 'new_v_gm_w_out', 'new_v_gla_w_in', 'new_v_gla_w_a1', 'new_v_gla_w_a2', 'new_v_gla_b_a', 'new_v_gla_norm_g', 'new_v_gla_w_o', 'new_v_ffn_w_up', 'new_v_ffn_w_down']
TWIN_LEAF_KINDS = {'loss': 'loss', 'grad_x': 'grad_x', 'grad_norm_mix_g': 'grad_w', 'grad_norm_ffn_g': 'grad_w', 'grad_final_g': 'grad_w', 'grad_gm_w_in': 'grad_w', 'grad_gm_ln_g': 'grad_w', 'grad_gm_ln_b': 'grad_w', 'grad_gm_w_s': 'grad_w', 'grad_gm_b_s': 'grad_w', 'grad_gm_w_out': 'grad_w', 'grad_gla_w_in': 'grad_w', 'grad_gla_w_a1': 'grad_w', 'grad_gla_w_a2': 'grad_w', 'grad_gla_b_a': 'grad_w', 'grad_gla_norm_g': 'grad_w', 'grad_gla_w_o': 'grad_w', 'grad_ffn_w_up': 'grad_w', 'grad_ffn_w_down': 'grad_w', 'delta_norm_mix_g': 'delta_w', 'delta_norm_ffn_g': 'delta_w', 'delta_final_g': 'delta_w', 'delta_gm_w_in': 'delta_w', 'delta_gm_ln_g': 'delta_w', 'delta_gm_ln_b': 'delta_w', 'delta_gm_w_s': 'delta_w', 'delta_gm_b_s': 'delta_w', 'delta_gm_w_out': 'delta_w', 'delta_gla_w_in': 'delta_w', 'delta_gla_w_a1': 'delta_w', 'delta_gla_w_a2': 'delta_w', 'delta_gla_b_a': 'delta_w', 'delta_gla_norm_g': 'delta_w', 'delta_gla_w_o': 'delta_w', 'delta_ffn_w_up': 'delta_w', 'delta_ffn_w_down': 'delta_w', 'new_m_norm_mix_g': 'new_m', 'new_m_norm_ffn_g': 'new_m', 'new_m_final_g': 'new_m', 'new_m_gm_w_in': 'new_m', 'new_m_gm_ln_g': 'new_m', 'new_m_gm_ln_b': 'new_m', 'new_m_gm_w_s': 'new_m', 'new_m_gm_b_s': 'new_m', 'new_m_gm_w_out': 'new_m', 'new_m_gla_w_in': 'new_m', 'new_m_gla_w_a1': 'new_m', 'new_m_gla_w_a2': 'new_m', 'new_m_gla_b_a': 'new_m', 'new_m_gla_norm_g': 'new_m', 'new_m_gla_w_o': 'new_m', 'new_m_ffn_w_up': 'new_m', 'new_m_ffn_w_down': 'new_m', 'new_v_norm_mix_g': 'new_v', 'new_v_norm_ffn_g': 'new_v', 'new_v_final_g': 'new_v', 'new_v_gm_w_in': 'new_v', 'new_v_gm_ln_g': 'new_v', 'new_v_gm_ln_b': 'new_v', 'new_v_gm_w_s': 'new_v', 'new_v_gm_b_s': 'new_v', 'new_v_gm_w_out': 'new_v', 'new_v_gla_w_in': 'new_v', 'new_v_gla_w_a1': 'new_v', 'new_v_gla_w_a2': 'new_v', 'new_v_gla_b_a': 'new_v', 'new_v_gla_norm_g': 'new_v', 'new_v_gla_w_o': 'new_v', 'new_v_ffn_w_up': 'new_v', 'new_v_ffn_w_down': 'new_v'}


def _forward(args):
    return _fwd_reference(*[args[k] for k in FWD_PARAMS])


def _output_shape():
    def fwd():
        inp = _fwd_setup_inputs(0)
        return _fwd_reference(*[inp[k] for k in FWD_PARAMS])
    out = _jax.eval_shape(fwd)
    return out.shape, out.dtype

N_MICROBATCH = 1
ADAM_LR = 0.001
ADAM_B1 = 0.9
ADAM_B2 = 0.999
ADAM_EPS = 1e-08
ADAM_WD = 0.01
ADAM_STEP = 10
PER_EXAMPLE_BATCH_AXIS = {'x': 0, 'loss_target': 0}
SHARED_INPUTS = []
_WEIGHT_DTYPES = {'norm_mix_g': _jnp.float32, 'norm_ffn_g': _jnp.float32, 'final_g': _jnp.float32, 'gm_w_in': _jnp.float32, 'gm_ln_g': _jnp.float32, 'gm_ln_b': _jnp.float32, 'gm_w_s': _jnp.float32, 'gm_b_s': _jnp.float32, 'gm_w_out': _jnp.float32, 'gla_w_in': _jnp.float32, 'gla_w_a1': _jnp.float32, 'gla_w_a2': _jnp.float32, 'gla_b_a': _jnp.float32, 'gla_norm_g': _jnp.float32, 'gla_w_o': _jnp.float32, 'ffn_w_up': _jnp.float32, 'ffn_w_down': _jnp.float32}
MOMENT_SCALE = {'norm_mix_g': 1.053610e-01, 'norm_ffn_g': 9.860938e-02, 'final_g': 3.280570e+01, 'gm_w_in': 8.428270e-02, 'gm_ln_g': 6.110613e-02, 'gm_ln_b': 6.326580e-02, 'gm_w_s': 8.541267e-02, 'gm_b_s': 9.541618e-02, 'gm_w_out': 1.012730e-01, 'gla_w_in': 4.972226e-02, 'gla_w_a1': 1.013919e-01, 'gla_w_a2': 1.331872e-02, 'gla_b_a': 3.634785e-02, 'gla_norm_g': 4.113488e-02, 'gla_w_o': 3.963995e-02, 'ffn_w_up': 4.941525e-02, 'ffn_w_down': 1.040528e-01}


def _to_microbatches(a, axis):
    t = _jnp.moveaxis(a, axis, 0)
    t = t.reshape((N_MICROBATCH, t.shape[0] // N_MICROBATCH) + t.shape[1:])
    return _jnp.moveaxis(t, 1, axis + 1)


def setup_inputs(seed: int = 0) -> dict:
    inp = _fwd_setup_inputs(seed)
    key = _jax.random.fold_in(_jax.random.key(seed), 7919)
    shape, _ = _output_shape()
    out = dict(inp)
    out["loss_target"] = _jax.random.normal(_jax.random.fold_in(key, 0), shape, _jnp.float32)
    for i, name in enumerate(TWIN_WEIGHTS):
        w = inp[name].astype(_jnp.float32)
        if MOMENT_SCALE is None:
            s = _jnp.sqrt(_jnp.mean(_jnp.square(w)) + 1e-30)
        else:
            s = MOMENT_SCALE[name]
        km, kv = _jax.random.split(_jax.random.fold_in(key, i + 1))
        out[name] = w
        out["m_" + name] = s * _jax.random.normal(km, w.shape, _jnp.float32)
        out["v_" + name] = (s * s) * _jax.random.uniform(kv, w.shape, _jnp.float32, 0.5, 1.5)
    if N_MICROBATCH > 1:
        for name, axis in PER_EXAMPLE_BATCH_AXIS.items():
            out[name] = _to_microbatches(out[name], axis)
    return {'x': out['x'], 'norm_mix_g': out['norm_mix_g'], 'norm_ffn_g': out['norm_ffn_g'], 'final_g': out['final_g'], 'gm_w_in': out['gm_w_in'], 'gm_ln_g': out['gm_ln_g'], 'gm_ln_b': out['gm_ln_b'], 'gm_w_s': out['gm_w_s'], 'gm_b_s': out['gm_b_s'], 'gm_w_out': out['gm_w_out'], 'gla_w_in': out['gla_w_in'], 'gla_w_a1': out['gla_w_a1'], 'gla_w_a2': out['gla_w_a2'], 'gla_b_a': out['gla_b_a'], 'gla_norm_g': out['gla_norm_g'], 'gla_w_o': out['gla_w_o'], 'ffn_w_up': out['ffn_w_up'], 'ffn_w_down': out['ffn_w_down'], 'loss_target': out['loss_target'], 'm_norm_mix_g': out['m_norm_mix_g'], 'm_norm_ffn_g': out['m_norm_ffn_g'], 'm_final_g': out['m_final_g'], 'm_gm_w_in': out['m_gm_w_in'], 'm_gm_ln_g': out['m_gm_ln_g'], 'm_gm_ln_b': out['m_gm_ln_b'], 'm_gm_w_s': out['m_gm_w_s'], 'm_gm_b_s': out['m_gm_b_s'], 'm_gm_w_out': out['m_gm_w_out'], 'm_gla_w_in': out['m_gla_w_in'], 'm_gla_w_a1': out['m_gla_w_a1'], 'm_gla_w_a2': out['m_gla_w_a2'], 'm_gla_b_a': out['m_gla_b_a'], 'm_gla_norm_g': out['m_gla_norm_g'], 'm_gla_w_o': out['m_gla_w_o'], 'm_ffn_w_up': out['m_ffn_w_up'], 'm_ffn_w_down': out['m_ffn_w_down'], 'v_norm_mix_g': out['v_norm_mix_g'], 'v_norm_ffn_g': out['v_norm_ffn_g'], 'v_final_g': out['v_final_g'], 'v_gm_w_in': out['v_gm_w_in'], 'v_gm_ln_g': out['v_gm_ln_g'], 'v_gm_ln_b': out['v_gm_ln_b'], 'v_gm_w_s': out['v_gm_w_s'], 'v_gm_b_s': out['v_gm_b_s'], 'v_gm_w_out': out['v_gm_w_out'], 'v_gla_w_in': out['v_gla_w_in'], 'v_gla_w_a1': out['v_gla_w_a1'], 'v_gla_w_a2': out['v_gla_w_a2'], 'v_gla_b_a': out['v_gla_b_a'], 'v_gla_norm_g': out['v_gla_norm_g'], 'v_gla_w_o': out['v_gla_w_o'], 'v_ffn_w_up': out['v_ffn_w_up'], 'v_ffn_w_down': out['v_ffn_w_down']}


def _loss(weights, diff, rest, loss_target):
    with _jax.named_scope("forward"):
        args = {**rest, TWIN_DIFF_INPUT: diff, **{k: w.astype(_WEIGHT_DTYPES[k]) for k, w in weights.items()}}
        y = _forward(args)
    with _jax.named_scope("loss_head"):
        err = _jnp.square(y.astype(_jnp.float32) - loss_target)
        return 0.5 * _jnp.sum(_jnp.mean(err, axis=-1)) if err.ndim else 0.5 * err


def _adamw(w, g, m, v):
    m = ADAM_B1 * m + (1.0 - ADAM_B1) * g
    v = ADAM_B2 * v + (1.0 - ADAM_B2) * _jnp.square(g)
    m_hat = m / (1.0 - ADAM_B1 ** ADAM_STEP)
    v_hat = v / (1.0 - ADAM_B2 ** ADAM_STEP)
    delta = -ADAM_LR * (m_hat / (_jnp.sqrt(v_hat) + ADAM_EPS) + ADAM_WD * w)
    return delta, m, v


def reference(x, norm_mix_g, norm_ffn_g, final_g, gm_w_in, gm_ln_g, gm_ln_b, gm_w_s, gm_b_s, gm_w_out, gla_w_in, gla_w_a1, gla_w_a2, gla_b_a, gla_norm_g, gla_w_o, ffn_w_up, ffn_w_down, loss_target, m_norm_mix_g, m_norm_ffn_g, m_final_g, m_gm_w_in, m_gm_ln_g, m_gm_ln_b, m_gm_w_s, m_gm_b_s, m_gm_w_out, m_gla_w_in, m_gla_w_a1, m_gla_w_a2, m_gla_b_a, m_gla_norm_g, m_gla_w_o, m_ffn_w_up, m_ffn_w_down, v_norm_mix_g, v_norm_ffn_g, v_final_g, v_gm_w_in, v_gm_ln_g, v_gm_ln_b, v_gm_w_s, v_gm_b_s, v_gm_w_out, v_gla_w_in, v_gla_w_a1, v_gla_w_a2, v_gla_b_a, v_gla_norm_g, v_gla_w_o, v_ffn_w_up, v_ffn_w_down):
    given = dict(x=x, norm_mix_g=norm_mix_g, norm_ffn_g=norm_ffn_g, final_g=final_g, gm_w_in=gm_w_in, gm_ln_g=gm_ln_g, gm_ln_b=gm_ln_b, gm_w_s=gm_w_s, gm_b_s=gm_b_s, gm_w_out=gm_w_out, gla_w_in=gla_w_in, gla_w_a1=gla_w_a1, gla_w_a2=gla_w_a2, gla_b_a=gla_b_a, gla_norm_g=gla_norm_g, gla_w_o=gla_w_o, ffn_w_up=ffn_w_up, ffn_w_down=ffn_w_down, loss_target=loss_target, m_norm_mix_g=m_norm_mix_g, m_norm_ffn_g=m_norm_ffn_g, m_final_g=m_final_g, m_gm_w_in=m_gm_w_in, m_gm_ln_g=m_gm_ln_g, m_gm_ln_b=m_gm_ln_b, m_gm_w_s=m_gm_w_s, m_gm_b_s=m_gm_b_s, m_gm_w_out=m_gm_w_out, m_gla_w_in=m_gla_w_in, m_gla_w_a1=m_gla_w_a1, m_gla_w_a2=m_gla_w_a2, m_gla_b_a=m_gla_b_a, m_gla_norm_g=m_gla_norm_g, m_gla_w_o=m_gla_w_o, m_ffn_w_up=m_ffn_w_up, m_ffn_w_down=m_ffn_w_down, v_norm_mix_g=v_norm_mix_g, v_norm_ffn_g=v_norm_ffn_g, v_final_g=v_final_g, v_gm_w_in=v_gm_w_in, v_gm_ln_g=v_gm_ln_g, v_gm_ln_b=v_gm_ln_b, v_gm_w_s=v_gm_w_s, v_gm_b_s=v_gm_b_s, v_gm_w_out=v_gm_w_out, v_gla_w_in=v_gla_w_in, v_gla_w_a1=v_gla_w_a1, v_gla_w_a2=v_gla_w_a2, v_gla_b_a=v_gla_b_a, v_gla_norm_g=v_gla_norm_g, v_gla_w_o=v_gla_w_o, v_ffn_w_up=v_ffn_w_up, v_ffn_w_down=v_ffn_w_down)
    weights = {n: given[n] for n in TWIN_WEIGHTS}
    shared = {n: given[n] for n in SHARED_INPUTS}
    per_example = {n: given[n] for n in ['x']}
    grad_fn = _jax.value_and_grad(_loss, argnums=(0, 1))

    def one_microbatch(ex, loss_target):
        ex = dict(ex)
        diff = ex.pop(TWIN_DIFF_INPUT)
        return grad_fn(weights, diff, {**shared, **ex}, loss_target)

    if N_MICROBATCH == 1:
        loss, (grad_w, grad_x) = one_microbatch(per_example, given["loss_target"])
    else:
        def body(carry, xs):
            loss_sum, grad_sum = carry
            l_k, (gw_k, gx_k) = one_microbatch(xs[0], xs[1])
            with _jax.named_scope("update"):
                return (loss_sum + l_k, _jax.tree.map(_jnp.add, grad_sum, gw_k)), gx_k

        init = (_jnp.zeros((), _jnp.float32), _jax.tree.map(_jnp.zeros_like, weights))
        (loss, grad_w), grad_x = _jax.lax.scan(body, init, (per_example, given["loss_target"]))
    with _jax.named_scope("update"):
        delta_w, new_m, new_v = {}, {}, {}
        for n in TWIN_WEIGHTS:
            delta_w[n], new_m[n], new_v[n] = _adamw(weights[n], grad_w[n], given["m_" + n], given["v_" + n])
    return (loss, grad_x, *[grad_w[n] for n in TWIN_WEIGHTS], *[delta_w[n] for n in TWIN_WEIGHTS],
            *[new_m[n] for n in TWIN_WEIGHTS], *[new_v[n] for n in TWIN_WEIGHTS])
```

```python
import functools

import jax
import jax.numpy as jnp
from jax import lax
from jax.experimental import pallas as pl
from jax.experimental.pallas import tpu as pltpu

F32 = jnp.float32
BF16 = jnp.bfloat16

EPS = 1e-6
CHUNK = 64
GM_BLOCK = 128
GM_GROUPS = 8
GLA_HEADS = 4
GATE_RANK = 16
GATE_TAU = 16.0
LOW = 128
N_CHIPS = 4
N_DEV = 8

ADAM_LR = 0.001
ADAM_B1 = 0.9
ADAM_B2 = 0.999
ADAM_EPS = 1e-08
ADAM_WD = 0.01
ADAM_STEP = 10

V7X_VMEM_LIMIT = 48 * 1024 * 1024
LANES = 128
SUBLANES = 8
MESH = pl.DeviceIdType.MESH
HIGHEST = lax.Precision.HIGHEST


def _pick(n, cap):
    if n <= cap:
        return n
    best = LANES
    for t in range(LANES, cap + 1, LANES):
        if n % t == 0:
            best = t
    return best


def _cparams(sem=None):
    return pltpu.CompilerParams(dimension_semantics=sem, vmem_limit_bytes=V7X_VMEM_LIMIT)


ANY = pl.BlockSpec(memory_space=pl.ANY)


NN = ((1,), (0,))
NT = ((1,), (1,))
TN = ((0,), (0,))


def _mm(name, a, b, *, dims, grid, a_spec, b_spec, out_spec, out_shape, acc_shape, extra=(), extra_specs=(),
        prologue=None, epilogue=None, slab=None):
    nk = grid[2]
    n_extra = len(extra)

    def body(*refs):
        a_ref, b_ref = refs[0], refs[1]
        ex = refs[2:2 + n_extra]
        pos = 2 + n_extra + (1 if slab is not None else 0)
        o_ref = refs[pos]
        av = a_ref[...]
        if prologue is not None:
            av = prologue(av)
        p = lax.dot_general(av, b_ref[...], (dims, ((), ())), preferred_element_type=F32)

        def finish(acc):
            r = acc if epilogue is None else epilogue(acc, *ex)
            o_ref[...] = r.astype(o_ref.dtype)

        if nk == 1:
            finish(p)
        else:
            acc_ref = refs[pos + 1]
            k = pl.program_id(2)

            @pl.when(k == 0)
            def _():
                acc_ref[...] = p

            @pl.when(k > 0)
            def _():
                acc_ref[...] += p

            @pl.when(k == nk - 1)
            def _():
                finish(acc_ref[...])

    inputs = [a, b, *extra]
    in_specs = [a_spec, b_spec, *extra_specs]
    aliases = {}
    if slab is not None:
        inputs.append(slab)
        in_specs.append(ANY)
        aliases = {len(inputs) - 1: 0}
    return pl.pallas_call(
        body, name=name, grid=grid, in_specs=in_specs, out_specs=out_spec, out_shape=out_shape,
        scratch_shapes=[pltpu.VMEM(acc_shape, F32)] if nk > 1 else [],
        input_output_aliases=aliases,
        compiler_params=_cparams(("parallel", "parallel", "arbitrary")),
    )(*inputs)


def _sds(shape, dtype):
    return jax.ShapeDtypeStruct(shape, dtype)


def mm_fwd_col(name, h, slab, off, n_out, epilogue=None, out_dtype=BF16):
    S, D = h.shape
    W = slab.shape[2]
    tm, tn = min(1024, S), _pick(W, 1024)
    wps = W // tn
    return _mm(name, h, slab, dims=NN, grid=(S // tm, n_out // tn, 1),
               a_spec=pl.BlockSpec((tm, D), lambda i, j, k: (i, 0)),
               b_spec=pl.BlockSpec((None, D, tn), lambda i, j, k: (j // wps, off // D, j % wps)),
               out_spec=pl.BlockSpec((tm, tn), lambda i, j, k: (i, j)),
               out_shape=_sds((S, n_out), out_dtype), acc_shape=(tm, tn), epilogue=epilogue)


def mm_fwd_row(name, a, slab, off, ksh, res, prologue=None):
    S, K = a.shape
    D = slab.shape[2]
    tm, tn, tk = min(1024, S), min(1024, D), min(512, ksh)
    kps = ksh // tk
    return _mm(name, a, slab, dims=NN, grid=(S // tm, D // tn, K // tk),
               a_spec=pl.BlockSpec((tm, tk), lambda i, j, k: (i, k)),
               b_spec=pl.BlockSpec((None, tk, tn), lambda i, j, k: (k // kps, off // tk + k % kps, j)),
               out_spec=pl.BlockSpec((tm, tn), lambda i, j, k: (i, j)),
               out_shape=_sds((S, D), F32), acc_shape=(tm, tn),
               extra=(res,), extra_specs=(pl.BlockSpec((tm, tn), lambda i, j, k: (i, j)),),
               prologue=prologue, epilogue=lambda acc, r: acc + r[...])


def mm_bwd_col_x(name, dz, slab, off, D, add=None):
    S, N = dz.shape
    W = slab.shape[2]
    tm, to, tr = min(1024, S), min(1024, D), _pick(W, 1024)
    wps = W // tr
    extra, extra_specs, epi = (), (), None
    if add is not None:
        extra, extra_specs = (add,), (pl.BlockSpec((tm, to), lambda i, j, k: (i, j)),)
        epi = lambda acc, r: acc + r[...]
    return _mm(name, dz, slab, dims=NT, grid=(S // tm, D // to, N // tr),
               a_spec=pl.BlockSpec((tm, tr), lambda i, j, k: (i, k)),
               b_spec=pl.BlockSpec((None, to, tr), lambda i, j, k: (k // wps, off // to + j, k % wps)),
               out_spec=pl.BlockSpec((tm, to), lambda i, j, k: (i, j)),
               out_shape=_sds((S, D), F32), acc_shape=(tm, to), extra=extra, extra_specs=extra_specs, epilogue=epi)


def mm_bwd_row_x(name, dxb, slab, off, ksh, mul=None):
    S, D = dxb.shape
    tm, tn = min(1024, S), min(1024, ksh)
    kps = ksh // tn
    extra, extra_specs, epi = (), (), None
    if mul is not None:
        extra, extra_specs = (mul,), (pl.BlockSpec((tm, tn), lambda i, j, k: (i, j)),)
        epi = lambda acc, r: acc * (2.0 * r[...].astype(F32))
    return _mm(name, dxb, slab, dims=NT, grid=(S // tm, 4 * kps, 1),
               a_spec=pl.BlockSpec((tm, D), lambda i, j, k: (i, 0)),
               b_spec=pl.BlockSpec((None, tn, D), lambda i, j, k: (j // kps, off // tn + j % kps, 0)),
               out_spec=pl.BlockSpec((tm, tn), lambda i, j, k: (i, j)),
               out_shape=_sds((S, 4 * ksh), BF16), acc_shape=(tm, tn), extra=extra, extra_specs=extra_specs, epilogue=epi)


def mm_bwd_col_w(name, h, dz, gslab, off):
    S, D = h.shape
    N = dz.shape[1]
    W = gslab.shape[2]
    tk, tn, ts = min(1024, D), _pick(W, 1024), min(1024, S)
    wps = W // tn
    return _mm(name, h, dz, dims=TN, grid=(D // tk, N // tn, S // ts),
               a_spec=pl.BlockSpec((ts, tk), lambda i, j, k: (k, i)),
               b_spec=pl.BlockSpec((ts, tn), lambda i, j, k: (k, j)),
               out_spec=pl.BlockSpec((None, tk, tn), lambda i, j, k: (j // wps, off // tk + i, j % wps)),
               out_shape=_sds(gslab.shape, gslab.dtype), acc_shape=(tk, tn), slab=gslab)


def mm_bwd_row_w(name, a, dxb, gslab, off, ksh, prologue=None):
    S, K = a.shape
    D = dxb.shape[1]
    tk, tn, ts = min(1024, ksh), min(1024, D), min(1024, S)
    kps = ksh // tk
    return _mm(name, a, dxb, dims=TN, grid=(K // tk, D // tn, S // ts),
               a_spec=pl.BlockSpec((ts, tk), lambda i, j, k: (k, i)),
               b_spec=pl.BlockSpec((ts, tn), lambda i, j, k: (k, j)),
               out_spec=pl.BlockSpec((None, tk, tn), lambda i, j, k: (i // kps, off // tk + i % kps, j)),
               out_shape=_sds(gslab.shape, gslab.dtype), acc_shape=(tk, tn), slab=gslab, prologue=prologue)


def mm_plain(name, a, b, dims, out_dtype, add=None):
    if dims == NN:
        M, N = a.shape[0], b.shape[1]
    elif dims == NT:
        M, N = a.shape[0], b.shape[0]
    else:
        M, N = a.shape[1], b.shape[1]
    red = a.shape[0] if dims == TN else a.shape[1]
    tm, tn = min(1024, M), min(1024, N)
    tr = min(1024, red) if dims == TN else red
    nk = red // tr
    if dims == TN:
        a_spec = pl.BlockSpec((tr, tm), lambda i, j, k: (k, i))
        b_spec = pl.BlockSpec((tr, tn), lambda i, j, k: (k, j))
    elif dims == NN:
        a_spec = pl.BlockSpec((tm, tr), lambda i, j, k: (i, k))
        b_spec = pl.BlockSpec((tr, tn), lambda i, j, k: (k, j))
    else:
        a_spec = pl.BlockSpec((tm, tr), lambda i, j, k: (i, k))
        b_spec = pl.BlockSpec((tn, tr), lambda i, j, k: (j, k))
    extra, extra_specs, epi = (), (), None
    if add is not None:
        extra, extra_specs = (add,), (pl.BlockSpec((tm, tn), lambda i, j, k: (i, j)),)
        epi = lambda acc, r: acc + r[...]
    return _mm(name, a, b, dims=dims, grid=(M // tm, N // tn, nk), a_spec=a_spec, b_spec=b_spec,
               out_spec=pl.BlockSpec((tm, tn), lambda i, j, k: (i, j)),
               out_shape=_sds((M, N), out_dtype), acc_shape=(tm, tn), extra=extra, extra_specs=extra_specs, epilogue=epi)


ROWS = 256


def rmsnorm_fwd(name, x, g):
    S, D = x.shape
    tr = min(ROWS, S)

    def body(x_ref, g_ref, h_ref):
        xv = x_ref[...]
        rstd = lax.rsqrt(jnp.mean(xv * xv, axis=-1, keepdims=True) + EPS)
        h_ref[...] = (xv * rstd * g_ref[...]).astype(BF16)

    return pl.pallas_call(
        body, name=name, grid=(S // tr,),
        in_specs=[pl.BlockSpec((tr, D), lambda i: (i, 0)), pl.BlockSpec((1, D), lambda i: (0, 0))],
        out_specs=pl.BlockSpec((tr, D), lambda i: (i, 0)), out_shape=_sds((S, D), BF16),
        compiler_params=_cparams(("parallel",)),
    )(x, g)


def rmsnorm_bwd(name, x, g, dh, dres):
    S, D = x.shape
    tr = min(ROWS, S)

    def body(x_ref, g_ref, dh_ref, dres_ref, dx_ref, dxb_ref, dg_ref):
        xv = x_ref[...]
        rstd = lax.rsqrt(jnp.mean(xv * xv, axis=-1, keepdims=True) + EPS)
        xh = xv * rstd
        dy = dh_ref[...]
        dxh = dy * g_ref[...]
        dx = dres_ref[...] + rstd * (dxh - xh * jnp.mean(dxh * xh, axis=-1, keepdims=True))
        dx_ref[...] = dx
        dxb_ref[...] = dx.astype(BF16)

        @pl.when(pl.program_id(0) == 0)
        def _():
            dg_ref[...] = jnp.zeros_like(dg_ref)

        dg_ref[...] += jnp.sum(dy * xh, axis=0, keepdims=True)

    row = pl.BlockSpec((tr, D), lambda i: (i, 0))
    vec = pl.BlockSpec((1, D), lambda i: (0, 0))
    return pl.pallas_call(
        body, name=name, grid=(S // tr,), in_specs=[row, vec, row, row], out_specs=[row, row, vec],
        out_shape=[_sds((S, D), F32), _sds((S, D), BF16), _sds((1, D), F32)],
        compiler_params=_cparams(("arbitrary",)),
    )(x, g, dh, dres)


def final_loss(name, x, g, target):
    S, D = x.shape
    tr = min(ROWS, S)

    def body(x_ref, g_ref, t_ref, loss_ref, dx_ref, dxb_ref, dg_ref):
        xv = x_ref[...]
        gv = g_ref[...]
        rstd = lax.rsqrt(jnp.mean(xv * xv, axis=-1, keepdims=True) + EPS)
        xh = xv * rstd
        err = xh * gv - t_ref[...]
        dy = err * (1.0 / D)
        dxh = dy * gv
        dx = rstd * (dxh - xh * jnp.mean(dxh * xh, axis=-1, keepdims=True))
        dx_ref[...] = dx
        dxb_ref[...] = dx.astype(BF16)

        @pl.when(pl.program_id(0) == 0)
        def _():
            dg_ref[...] = jnp.zeros_like(dg_ref)
            loss_ref[...] = jnp.zeros_like(loss_ref)

        dg_ref[...] += jnp.sum(dy * xh, axis=0, keepdims=True)
        loss_ref[...] += 0.5 * jnp.sum(jnp.mean(err * err, axis=-1, keepdims=True))

    row = pl.BlockSpec((tr, D), lambda i: (i, 0))
    vec = pl.BlockSpec((1, D), lambda i: (0, 0))
    return pl.pallas_call(
        body, name=name, grid=(S // tr,), in_specs=[row, vec, row],
        out_specs=[pl.BlockSpec((SUBLANES, LANES), lambda i: (0, 0)), row, row, vec],
        out_shape=[_sds((SUBLANES, LANES), F32), _sds((S, D), F32), _sds((S, D), BF16), _sds((1, D), F32)],
        compiler_params=_cparams(("arbitrary",)),
    )(x, g, target)


def _gelu(x):
    return 0.5 * x * (1.0 + lax.erf(x * 0.7071067811865476))


def _gelu_grad(x):
    return 0.5 * (1.0 + lax.erf(x * 0.7071067811865476)) + x * jnp.exp(-0.5 * x * x) * 0.3989422804014327


def _gm_common(zp, lng, lnb, D):
    z = _gelu(zp)
    u, v = z[:, :D], z[:, D:]
    xc = v - jnp.mean(v, axis=-1, keepdims=True)
    rstd = lax.rsqrt(jnp.mean(xc * xc, axis=-1, keepdims=True) + EPS)
    xh = xc * rstd
    return u, xh, rstd, xh * lng + lnb


def gm_mid_fwd(name, zp, lng, lnb, wm, bT):
    S, D2 = zp.shape
    D = D2 // 2
    dg = D // GM_GROUPS
    P = GM_BLOCK

    def body(z_ref, lng_ref, lnb_ref, wm_ref, bT_ref, o_ref):
        u, _, _, vn = _gm_common(z_ref[...].astype(F32), lng_ref[...], lnb_ref[...], D)
        vnb = vn.astype(BF16)
        for gi in range(GM_GROUPS):
            cols = slice(gi * dg, (gi + 1) * dg)
            mixed = jnp.dot(wm_ref[gi], vnb[:, cols], preferred_element_type=F32) + bT_ref[:, gi:gi + 1]
            o_ref[:, cols] = (u[:, cols] * mixed).astype(BF16)

    vec = pl.BlockSpec((1, D), lambda i: (0, 0))
    return pl.pallas_call(
        body, name=name, grid=(S // P,),
        in_specs=[pl.BlockSpec((P, D2), lambda i: (i, 0)), vec, vec,
                  pl.BlockSpec((GM_GROUPS, P, P), lambda i: (0, 0, 0)), pl.BlockSpec((P, GM_GROUPS), lambda i: (0, 0))],
        out_specs=pl.BlockSpec((P, D), lambda i: (i, 0)), out_shape=_sds((S, D), BF16),
        compiler_params=_cparams(("parallel",)),
    )(zp, lng, lnb, wm, bT)


def gm_mid_bwd(name, zp, dgated, lng, lnb, wm, wmT, bT):
    S, D2 = zp.shape
    D = D2 // 2
    dg = D // GM_GROUPS
    P = GM_BLOCK

    def body(z_ref, dgt_ref, lng_ref, lnb_ref, wm_ref, wmT_ref, bT_ref, dz_ref, dlng_ref, dlnb_ref, dw_ref, dbT_ref, dvn_ref):
        @pl.when(pl.program_id(0) == 0)
        def _():
            dlng_ref[...] = jnp.zeros_like(dlng_ref)
            dlnb_ref[...] = jnp.zeros_like(dlnb_ref)
            dw_ref[...] = jnp.zeros_like(dw_ref)
            dbT_ref[...] = jnp.zeros_like(dbT_ref)

        zp_v = z_ref[...].astype(F32)
        lng_v = lng_ref[...]
        u, xh, rstd, vn = _gm_common(zp_v, lng_v, lnb_ref[...], D)
        vnb = vn.astype(BF16)
        dgt = dgt_ref[...].astype(F32)
        for gi in range(GM_GROUPS):
            cols = slice(gi * dg, (gi + 1) * dg)
            mixed = jnp.dot(wm_ref[gi], vnb[:, cols], preferred_element_type=F32) + bT_ref[:, gi:gi + 1]
            dm = dgt[:, cols] * u[:, cols]
            dmb = dm.astype(BF16)
            dz_ref[:, cols] = (dgt[:, cols] * mixed * _gelu_grad(zp_v[:, cols])).astype(BF16)
            dbT_ref[:, gi:gi + 1] += jnp.sum(dm, axis=1, keepdims=True)
            dw_ref[gi] += lax.dot_general(dmb, vnb[:, cols], (NT, ((), ())), preferred_element_type=F32)
            dvn_ref[:, cols] = jnp.dot(wmT_ref[gi], dmb, preferred_element_type=F32)
        dvn = dvn_ref[...]
        dlng_ref[...] += jnp.sum(dvn * xh, axis=0, keepdims=True)
        dlnb_ref[...] += jnp.sum(dvn, axis=0, keepdims=True)
        dyg = dvn * lng_v
        dv = rstd * (dyg - jnp.mean(dyg, axis=-1, keepdims=True) - xh * jnp.mean(dyg * xh, axis=-1, keepdims=True))
        dz_ref[:, D:] = (dv * _gelu_grad(zp_v[:, D:])).astype(BF16)

    vec = pl.BlockSpec((1, D), lambda i: (0, 0))
    wsp = pl.BlockSpec((GM_GROUPS, P, P), lambda i: (0, 0, 0))
    bsp = pl.BlockSpec((P, GM_GROUPS), lambda i: (0, 0))
    return pl.pallas_call(
        body, name=name, grid=(S // P,),
        in_specs=[pl.BlockSpec((P, D2), lambda i: (i, 0)), pl.BlockSpec((P, D), lambda i: (i, 0)), vec, vec, wsp, wsp, bsp],
        out_specs=[pl.BlockSpec((P, D2), lambda i: (i, 0)), vec, vec, wsp, bsp],
        out_shape=[_sds((S, D2), BF16), _sds((1, D), F32), _sds((1, D), F32), _sds((GM_GROUPS, P, P), F32), _sds((P, GM_GROUPS), F32)],
        scratch_shapes=[pltpu.VMEM((P, D), F32)],
        compiler_params=_cparams(("arbitrary",)),
    )(zp, dgated, lng, lnb, wm, wmT, bT)


def _gla_gate(lr, w2, ba, tri):
    logit = jnp.dot(lr, w2, preferred_element_type=F32) + ba
    la = (jnp.minimum(logit, 0.0) - jnp.log1p(jnp.exp(-jnp.abs(logit)))) * (1.0 / GATE_TAU)
    g = jnp.dot(tri, la, preferred_element_type=F32, precision=HIGHEST)
    return logit, g


def gla_scan_fwd(name, proj, lr, w2p, ba, tri):
    S, D3 = proj.shape
    D = D3 // 3
    H, C = GLA_HEADS, CHUNK
    dk, dv = D // 2 // H, D // H
    NC = S // C
    scale = dk ** -0.5

    def body(q_ref, k_ref, v_ref, lr_ref, w2_ref, ba_ref, tri_ref, o_ref, st_ref, state):
        @pl.when(pl.program_id(1) == 0)
        def _():
            state[...] = jnp.zeros_like(state)

        _, g = _gla_gate(lr_ref[...], w2_ref[...], ba_ref[...], tri_ref[...])
        gend = g[C - 1:C, :]
        kdec = (k_ref[...].astype(F32) * jnp.exp(gend - g)).astype(BF16)
        kv = lax.dot_general(v_ref[...], kdec, (TN, ((), ())), preferred_element_type=F32)
        new = jnp.exp(gend) * state[...] + kv
        state[...] = new
        nb = new.astype(BF16)
        st_ref[...] = nb
        qs = (q_ref[...].astype(F32) * scale).astype(BF16)
        o_ref[...] = lax.dot_general(qs, nb, (NT, ((), ())), preferred_element_type=F32).astype(BF16)

    return pl.pallas_call(
        body, name=name, grid=(H, NC),
        in_specs=[pl.BlockSpec((C, dk), lambda h, t: (t, h)), pl.BlockSpec((C, dk), lambda h, t: (t, H + h)),
                  pl.BlockSpec((C, dv), lambda h, t: (t, H + h)), pl.BlockSpec((C, LOW), lambda h, t: (t, 0)),
                  pl.BlockSpec((LOW, dk), lambda h, t: (0, h)), pl.BlockSpec((1, dk), lambda h, t: (0, h)),
                  pl.BlockSpec((C, C), lambda h, t: (0, 0))],
        out_specs=[pl.BlockSpec((C, dv), lambda h, t: (t, h)), pl.BlockSpec((None, None, dv, dk), lambda h, t: (t, h, 0, 0))],
        out_shape=[_sds((S, D), BF16), _sds((NC, H, dv, dk), BF16)],
        scratch_shapes=[pltpu.VMEM((dv, dk), F32)],
        compiler_params=_cparams(("parallel", "arbitrary")),
    )(proj, proj, proj, lr, w2p, ba, tri)


def gla_scan_bwd(name, proj, lr, w2p, ba, tri, triT, states, do):
    S, D3 = proj.shape
    D = D3 // 3
    H, C = GLA_HEADS, CHUNK
    dk, dv = D // 2 // H, D // H
    NC = S // C
    scale = dk ** -0.5

    def body(q_ref, k_ref, v_ref, lr_ref, w2_ref, ba_ref, tri_ref, triT_ref, st_ref, sp_ref, do_ref,
             dq_ref, dk_ref, dv_ref, dl_ref, dba_ref, dstate):
        t = pl.program_id(1)

        @pl.when(t == 0)
        def _():
            dstate[...] = jnp.zeros_like(dstate)
            dba_ref[...] = jnp.zeros_like(dba_ref)

        logit, g = _gla_gate(lr_ref[...], w2_ref[...], ba_ref[...], tri_ref[...])
        gend = g[C - 1:C, :]
        e = jnp.exp(gend - g)
        kf = k_ref[...].astype(F32)
        kdec = (kf * e).astype(BF16)
        dec = jnp.exp(gend)
        qs = (q_ref[...].astype(F32) * scale).astype(BF16)
        dob = do_ref[...]
        dq_ref[...] = (jnp.dot(dob, st_ref[...], preferred_element_type=F32) * scale).astype(BF16)
        ds = dstate[...] + lax.dot_general(dob, qs, (TN, ((), ())), preferred_element_type=F32)
        dsb = ds.astype(BF16)
        dkdec = jnp.dot(v_ref[...], dsb, preferred_element_type=F32)
        dv_ref[...] = lax.dot_general(kdec, dsb, (NT, ((), ())), preferred_element_type=F32).astype(BF16)
        has_prev = (t < NC - 1).astype(F32)
        ddec = jnp.sum(ds * sp_ref[...].astype(F32), axis=0, keepdims=True) * has_prev
        dstate[...] = dec * ds
        dk_ref[...] = (dkdec * e).astype(BF16)
        dd = dkdec * kf * e
        dgend = jnp.sum(dd, axis=0, keepdims=True) + ddec * dec
        last = lax.broadcasted_iota(jnp.int32, (C, 1), 0) == C - 1
        dg = jnp.where(last, dgend, 0.0) - dd
        dla = jnp.dot(triT_ref[...], dg, preferred_element_type=F32, precision=HIGHEST)
        dlogit = dla * (1.0 / GATE_TAU) * (1.0 - jax.nn.sigmoid(logit))
        dl_ref[...] = dlogit.astype(BF16)
        dba_ref[...] += jnp.sum(dlogit, axis=0, keepdims=True)

    rev = lambda t: NC - 1 - t
    return pl.pallas_call(
        body, name=name, grid=(H, NC),
        in_specs=[pl.BlockSpec((C, dk), lambda h, t: (rev(t), h)), pl.BlockSpec((C, dk), lambda h, t: (rev(t), H + h)),
                  pl.BlockSpec((C, dv), lambda h, t: (rev(t), H + h)), pl.BlockSpec((C, LOW), lambda h, t: (rev(t), 0)),
                  pl.BlockSpec((LOW, dk), lambda h, t: (0, h)), pl.BlockSpec((1, dk), lambda h, t: (0, h)),
                  pl.BlockSpec((C, C), lambda h, t: (0, 0)), pl.BlockSpec((C, C), lambda h, t: (0, 0)),
                  pl.BlockSpec((None, None, dv, dk), lambda h, t: (rev(t), h, 0, 0)),
                  pl.BlockSpec((None, None, dv, dk), lambda h, t: (jnp.maximum(rev(t) - 1, 0), h, 0, 0)),
                  pl.BlockSpec((C, dv), lambda h, t: (rev(t), h))],
        out_specs=[pl.BlockSpec((C, dk), lambda h, t: (rev(t), h)), pl.BlockSpec((C, dk), lambda h, t: (rev(t), h)),
                   pl.BlockSpec((C, dv), lambda h, t: (rev(t), h)), pl.BlockSpec((C, dk), lambda h, t: (rev(t), h)),
                   pl.BlockSpec((1, dk), lambda h, t: (0, h))],
        out_shape=[_sds((S, D // 2), BF16), _sds((S, D // 2), BF16), _sds((S, D), BF16), _sds((S, D // 2), BF16),
                   _sds((1, D // 2), F32)],
        scratch_shapes=[pltpu.VMEM((dv, dk), F32)],
        compiler_params=_cparams(("parallel", "arbitrary")),
    )(proj, proj, proj, lr, w2p, ba, tri, triT, states, states, do)


def _gla_post_common(o, r, ng):
    rs = lax.rsqrt(jnp.mean(o * o, axis=-1, keepdims=True) + EPS)
    oh = o * rs
    sig = jax.nn.sigmoid(r)
    return rs, oh, oh * ng, sig, r * sig


def gla_post_fwd(name, o_raw, proj, ng):
    S, D = o_raw.shape
    dv = D // GLA_HEADS
    tr = min(ROWS, S)

    def body(o_ref, r_ref, ng_ref, og_ref):
        for hh in range(GLA_HEADS):
            cols = slice(hh * dv, (hh + 1) * dv)
            _, _, on, _, sil = _gla_post_common(o_ref[:, cols].astype(F32), r_ref[:, cols].astype(F32), ng_ref[:, cols])
            og_ref[:, cols] = (on * sil).astype(BF16)

    row = pl.BlockSpec((tr, D), lambda i: (i, 0))
    return pl.pallas_call(
        body, name=name, grid=(S // tr,),
        in_specs=[row, pl.BlockSpec((tr, D), lambda i: (i, 2)), pl.BlockSpec((1, D), lambda i: (0, 0))],
        out_specs=row, out_shape=_sds((S, D), BF16), compiler_params=_cparams(("parallel",)),
    )(o_raw, proj, ng)


def gla_post_bwd(name, dog, o_raw, proj, ng):
    S, D = o_raw.shape
    dv = D // GLA_HEADS
    tr = min(ROWS, S)

    def body(dog_ref, o_ref, r_ref, ng_ref, do_ref, dr_ref, dng_ref):
        @pl.when(pl.program_id(0) == 0)
        def _():
            dng_ref[...] = jnp.zeros_like(dng_ref)

        for hh in range(GLA_HEADS):
            cols = slice(hh * dv, (hh + 1) * dv)
            r = r_ref[:, cols].astype(F32)
            ngv = ng_ref[:, cols]
            rs, oh, on, sig, sil = _gla_post_common(o_ref[:, cols].astype(F32), r, ngv)
            dogv = dog_ref[:, cols].astype(F32)
            don = dogv * sil
            dr_ref[:, cols] = (dogv * on * (sig * (1.0 + r * (1.0 - sig)))).astype(BF16)
            dng_ref[:, cols] += jnp.sum(don * oh, axis=0, keepdims=True)
            doh = don * ngv
            do_ref[:, cols] = (rs * (doh - oh * jnp.mean(doh * oh, axis=-1, keepdims=True))).astype(BF16)

    row = pl.BlockSpec((tr, D), lambda i: (i, 0))
    vec = pl.BlockSpec((1, D), lambda i: (0, 0))
    return pl.pallas_call(
        body, name=name, grid=(S // tr,),
        in_specs=[row, row, pl.BlockSpec((tr, D), lambda i: (i, 2)), vec],
        out_specs=[row, row, vec], out_shape=[_sds((S, D), BF16), _sds((S, D), BF16), _sds((1, D), F32)],
        compiler_params=_cparams(("arbitrary",)),
    )(dog, o_raw, proj, ng)


def _place():
    return lax.axis_index("x"), lax.axis_index("y"), lax.axis_index("c")


def allgather_chips(slabs):
    n = len(slabs)

    def body(*refs):
        ins, outs = refs[:n], refs[n:2 * n]
        send_sems, recv_sems, local_sems = refs[2 * n:]
        x, y, c = _place()
        chips = [(1 - x, y), (x, 1 - y), (1 - x, 1 - y)]
        sibling = (x, y, 1 - c)
        me = 2 * x + y

        def half(a, chip, hc):
            hr = slabs[a].shape[0] // 2
            return outs[a].at[chip, pl.ds(hc * hr, hr), :]

        def copy(a, k, src, dst, to):
            return pltpu.make_async_remote_copy(src_ref=src, dst_ref=dst, send_sem=send_sems.at[a, k],
                                                recv_sem=recv_sems.at[a, k], device_id=to, device_id_type=MESH)

        started = []
        for a in range(n):
            hr = slabs[a].shape[0] // 2
            mine = pltpu.make_async_copy(ins[a], outs[a].at[me], local_sems.at[a])
            mine.start()
            started.append(mine)
        sends = []
        for a in range(n):
            hr = slabs[a].shape[0] // 2
            for j, (px, py) in enumerate(chips):
                cp = copy(a, j, ins[a].at[pl.ds(c * hr, hr), :], half(a, me, c), (px, py, c))
                cp.start()
                sends.append(cp)
        for a in range(n):
            for j, (px, py) in enumerate(chips):
                src_chip = 2 * px + py
                copy(a, j, half(a, src_chip, c), half(a, src_chip, c), (px, py, c)).wait_recv()
                fw = copy(a, 3 + j, half(a, src_chip, c), half(a, src_chip, c), sibling)
                fw.start()
                sends.append(fw)
        for a in range(n):
            for j, (px, py) in enumerate(chips):
                src_chip = 2 * px + py
                copy(a, 3 + j, half(a, src_chip, 1 - c), half(a, src_chip, 1 - c), sibling).wait_recv()
        for cp in sends:
            cp.wait_send()
        for mine in started:
            mine.wait()

    return pl.pallas_call(
        body, name="allgather_chips",
        in_specs=[ANY] * n, out_specs=[ANY] * n,
        out_shape=[_sds((N_CHIPS,) + s.shape, s.dtype) for s in slabs],
        scratch_shapes=[pltpu.SemaphoreType.DMA((n, 6)), pltpu.SemaphoreType.DMA((n, 6)), pltpu.SemaphoreType.DMA((n,))],
    )(*slabs)


def allgather_devices(name, v):
    def body(v_ref, out_ref, send_sems, recv_sems, local_sem):
        x, y, c = _place()
        me = 4 * x + 2 * y + c
        mine = pltpu.make_async_copy(v_ref, out_ref.at[me], local_sem)
        mine.start()

        def peer(k):
            return (1 - x if k & 4 else x, 1 - y if k & 2 else y, 1 - c if k & 1 else c)

        def copy(k, src, dst):
            return pltpu.make_async_remote_copy(src_ref=src, dst_ref=dst, send_sem=send_sems.at[k - 1],
                                                recv_sem=recv_sems.at[k - 1], device_id=peer(k), device_id_type=MESH)

        sends = [copy(k, v_ref, out_ref.at[me]) for k in range(1, N_DEV)]
        for cp in sends:
            cp.start()
        for k in range(1, N_DEV):
            px, py, pc = peer(k)
            them = 4 * px + 2 * py + pc
            copy(k, out_ref.at[them], out_ref.at[them]).wait_recv()
        for cp in sends:
            cp.wait_send()
        mine.wait()

    return pl.pallas_call(
        body, name=name, in_specs=[ANY], out_specs=ANY, out_shape=_sds((N_DEV,) + v.shape, v.dtype),
        scratch_shapes=[pltpu.SemaphoreType.DMA((N_DEV - 1,)), pltpu.SemaphoreType.DMA((N_DEV - 1,)), pltpu.SemaphoreType.DMA],
    )(v)


def exchange_sibling_halves(gslabs):
    n = len(gslabs)

    def body(*refs):
        ins, outs = refs[:n], refs[n:2 * n]
        send_sems, recv_sems = refs[2 * n:]
        x, y, c = _place()
        cps = []
        for a in range(n):
            hr = gslabs[a].shape[1] // 2
            cp = pltpu.make_async_remote_copy(src_ref=ins[a].at[:, pl.ds((1 - c) * hr, hr), :], dst_ref=outs[a],
                                              send_sem=send_sems.at[a], recv_sem=recv_sems.at[a],
                                              device_id=(x, y, 1 - c), device_id_type=MESH)
            cp.start()
            cps.append(cp)
        for cp in cps:
            cp.wait()

    return pl.pallas_call(
        body, name="exchange_sibling_halves", in_specs=[ANY] * n, out_specs=[ANY] * n,
        out_shape=[_sds((N_CHIPS, g.shape[1] // 2, g.shape[2]), g.dtype) for g in gslabs],
        scratch_shapes=[pltpu.SemaphoreType.DMA((n,)), pltpu.SemaphoreType.DMA((n,))],
    )(*gslabs)


def scatter_chips(pslabs):
    n = len(pslabs)

    def body(*refs):
        ins, outs = refs[:n], refs[n:2 * n]
        send_sems, recv_sems = refs[2 * n:]
        x, y, c = _place()
        chips = [(1 - x, y), (x, 1 - y), (1 - x, 1 - y)]
        cps = []
        for a in range(n):
            for j, (px, py) in enumerate(chips):
                cp = pltpu.make_async_remote_copy(src_ref=ins[a].at[2 * px + py], dst_ref=outs[a].at[j],
                                                  send_sem=send_sems.at[a, j], recv_sem=recv_sems.at[a, j],
                                                  device_id=(px, py, c), device_id_type=MESH)
                cp.start()
                cps.append(cp)
        for cp in cps:
            cp.wait()

    return pl.pallas_call(
        body, name="scatter_chips", in_specs=[ANY] * n, out_specs=[ANY] * n,
        out_shape=[_sds((3,) + p.shape[1:], p.dtype) for p in pslabs],
        scratch_shapes=[pltpu.SemaphoreType.DMA((n, 3)), pltpu.SemaphoreType.DMA((n, 3))],
    )(*pslabs)


def join_sibling_halves(halves):
    n = len(halves)

    def body(*refs):
        ins, outs = refs[:n], refs[n:2 * n]
        send_sems, recv_sems, local_sems = refs[2 * n:]
        x, y, c = _place()
        cps, loc = [], []
        for a in range(n):
            hr = halves[a].shape[0]
            mine = pltpu.make_async_copy(ins[a], outs[a].at[pl.ds(c * hr, hr), :], local_sems.at[a])
            mine.start()
            loc.append(mine)
            cp = pltpu.make_async_remote_copy(src_ref=ins[a], dst_ref=outs[a].at[pl.ds(c * hr, hr), :],
                                              send_sem=send_sems.at[a], recv_sem=recv_sems.at[a],
                                              device_id=(x, y, 1 - c), device_id_type=MESH)
            cp.start()
            cps.append(cp)
        for a in range(n):
            hr = halves[a].shape[0]
            other = outs[a].at[pl.ds((1 - c) * hr, hr), :]
            pltpu.make_async_remote_copy(src_ref=other, dst_ref=other, send_sem=send_sems.at[a], recv_sem=recv_sems.at[a],
                                         device_id=(x, y, 1 - c), device_id_type=MESH).wait_recv()
        for cp in cps:
            cp.wait_send()
        for mine in loc:
            mine.wait()

    return pl.pallas_call(
        body, name="join_sibling_halves", in_specs=[ANY] * n, out_specs=[ANY] * n,
        out_shape=[_sds((2 * h.shape[0], h.shape[1]), h.dtype) for h in halves],
        scratch_shapes=[pltpu.SemaphoreType.DMA((n,)), pltpu.SemaphoreType.DMA((n,)), pltpu.SemaphoreType.DMA((n,))],
    )(*halves)


def pair_add(name, gslab, rsib, c_idx):
    _, R, W = gslab.shape
    hr = R // 2
    tr = _pick(hr, 512) if hr % 512 == 0 else hr
    nb = hr // tr

    def body(c_ref, a_ref, b_ref, o_ref):
        o_ref[...] = (a_ref[...].astype(F32) + b_ref[...].astype(F32)).astype(BF16)

    return pl.pallas_call(
        body, name=name,
        grid_spec=pltpu.PrefetchScalarGridSpec(
            num_scalar_prefetch=1, grid=(N_CHIPS, nb),
            in_specs=[pl.BlockSpec((None, tr, W), lambda s, i, c: (s, c[0] * nb + i, 0)),
                      pl.BlockSpec((None, tr, W), lambda s, i, c: (s, i, 0))],
            out_specs=pl.BlockSpec((None, tr, W), lambda s, i, c: (s, i, 0))),
        out_shape=_sds((N_CHIPS, hr, W), BF16), compiler_params=_cparams(("parallel", "parallel")),
    )(c_idx, gslab, rsib)


def sum_chips(name, pslab, q, chip_idx):
    _, hr, W = pslab.shape
    tr = _pick(hr, 512) if hr % 512 == 0 else hr
    nb = hr // tr

    def body(s_ref, p_ref, q0_ref, q1_ref, q2_ref, o_ref):
        o_ref[...] = ((p_ref[...].astype(F32) + q0_ref[...].astype(F32)) + q1_ref[...].astype(F32)) + q2_ref[...].astype(F32)

    def qspec(j):
        return pl.BlockSpec((None, tr, W), lambda i, s: (j, i, 0))

    return pl.pallas_call(
        body, name=name,
        grid_spec=pltpu.PrefetchScalarGridSpec(
            num_scalar_prefetch=1, grid=(nb,),
            in_specs=[pl.BlockSpec((None, tr, W), lambda i, s: (s[0], i, 0)), qspec(0), qspec(1), qspec(2)],
            out_specs=pl.BlockSpec((tr, W), lambda i, s: (i, 0))),
        out_shape=_sds((hr, W), F32), compiler_params=_cparams(("parallel",)),
    )(chip_idx, pslab, q, q, q)


def sum_devices(name, parts):
    _, R, W = parts.shape

    def body(p_ref, o_ref):
        acc = p_ref[0]
        for d in range(1, N_DEV):
            acc = acc + p_ref[d]
        o_ref[...] = acc

    tr = _pick(R, 512) if R % 512 == 0 else R
    return pl.pallas_call(
        body, name=name, grid=(R // tr,), in_specs=[pl.BlockSpec((N_DEV, tr, W), lambda i: (0, i, 0))],
        out_specs=pl.BlockSpec((tr, W), lambda i: (i, 0)), out_shape=_sds((R, W), F32),
        compiler_params=_cparams(("parallel",)),
    )(parts)


def adamw(name, g, off, w, m, v):
    R, W = w.shape
    tr = _pick(R, 256) if R % 256 == 0 else R

    def body(g_ref, w_ref, m_ref, v_ref, go_ref, d_ref, mo_ref, vo_ref):
        gv = g_ref[...]
        mn = ADAM_B1 * m_ref[...] + (1.0 - ADAM_B1) * gv
        vn = ADAM_B2 * v_ref[...] + (1.0 - ADAM_B2) * (gv * gv)
        m_hat = mn / (1.0 - ADAM_B1 ** ADAM_STEP)
        v_hat = vn / (1.0 - ADAM_B2 ** ADAM_STEP)
        go_ref[...] = gv
        d_ref[...] = -ADAM_LR * (m_hat / (jnp.sqrt(v_hat) + ADAM_EPS) + ADAM_WD * w_ref[...])
        mo_ref[...] = mn
        vo_ref[...] = vn

    blk = pl.BlockSpec((tr, W), lambda i: (i, 0))
    return pl.pallas_call(
        body, name=name, grid=(R // tr,),
        in_specs=[pl.BlockSpec((tr, W), lambda i: (off // tr + i, 0)), blk, blk, blk], out_specs=[blk] * 4,
        out_shape=[_sds((R, W), F32)] * 4, compiler_params=_cparams(("parallel",)),
    )(g, w, m, v)


def _pack(arrs):
    flat = jnp.concatenate([a.reshape(-1).astype(F32) for a in arrs])
    tile = SUBLANES * LANES * 2
    pad = (-flat.shape[0]) % tile
    return jnp.pad(flat, (0, pad)).reshape(-1, LANES)


def _unpack(packed, shapes):
    flat = packed.reshape(-1)
    out, pos = [], 0
    for s in shapes:
        n = 1
        for d in s:
            n *= d
        out.append(flat[pos:pos + n].reshape(s))
        pos += n
    return out


def kernel(x, norm_mix_g, norm_ffn_g, final_g, gm_w_in, gm_ln_g, gm_ln_b, gm_w_s, gm_b_s, gm_w_out, gla_w_in, gla_w_a1, gla_w_a2, gla_b_a, gla_norm_g, gla_w_o, ffn_w_up, ffn_w_down, loss_target, m_norm_mix_g, m_norm_ffn_g, m_final_g, m_gm_w_in, m_gm_ln_g, m_gm_ln_b, m_gm_w_s, m_gm_b_s, m_gm_w_out, m_gla_w_in, m_gla_w_a1, m_gla_w_a2, m_gla_b_a, m_gla_norm_g, m_gla_w_o, m_ffn_w_up, m_ffn_w_down, v_norm_mix_g, v_norm_ffn_g, v_final_g, v_gm_w_in, v_gm_ln_g, v_gm_ln_b, v_gm_w_s, v_gm_b_s, v_gm_w_out, v_gla_w_in, v_gla_w_a1, v_gla_w_a2, v_gla_b_a, v_gla_norm_g, v_gla_w_o, v_ffn_w_up, v_ffn_w_down):
    S, D = x.shape[1], x.shape[2]
    depth = norm_mix_g.shape[0]
    n_gm, n_gla = gm_w_in.shape[0], gla_w_in.shape[0]
    F = 4 * D
    P = GM_BLOCK
    xi, yi, ci = lax.axis_index("x"), lax.axis_index("y"), lax.axis_index("c")
    chip = 2 * xi + yi
    chip_idx = jnp.reshape(chip, (1,)).astype(jnp.int32)
    c_idx = jnp.reshape(ci, (1,)).astype(jnp.int32)

    def up_off(i): return i * D
    def down_off(i): return depth * D + i * D
    def gmo_off(j): return 2 * depth * D + j * (D // 4)
    def glo_off(j): return 2 * depth * D + n_gm * (D // 4) + j * (D // 4)
    rows_a = 2 * depth * D + (n_gm + n_gla) * (D // 4)

    slab_a = jnp.concatenate([ffn_w_up.reshape(depth * D, D), ffn_w_down.reshape(depth * D, D),
                              gm_w_out.reshape(n_gm * (D // 4), D), gla_w_o.reshape(n_gla * (D // 4), D)]).astype(BF16)
    slab_b = gm_w_in.reshape(n_gm * D, D // 2).astype(BF16)
    slab_c = gla_w_in.reshape(n_gla * D, 3 * D // 4).astype(BF16)
    small_w = [gla_w_a1, gla_w_a2, gla_b_a, gla_norm_g]
    slab_s = _pack(small_w)
    ga, gb, gc = allgather_chips([slab_a, slab_b, slab_c])
    gs = allgather_devices("allgather_small_weights", slab_s)
    per_chip = [_unpack(gs[2 * s], [a.shape for a in small_w]) for s in range(N_CHIPS)]
    w_a1 = jnp.concatenate([p[0] for p in per_chip], axis=1)
    w_a2 = jnp.concatenate([p[1] for p in per_chip], axis=2)
    b_a = jnp.concatenate([p[2] for p in per_chip], axis=1)
    gnorm = jnp.concatenate([p[3] for p in per_chip], axis=1)
    w_a1p = jnp.pad(w_a1, ((0, 0), (0, 0), (0, LOW - GATE_RANK))).astype(BF16)
    w_a2p = jnp.pad(w_a2, ((0, 0), (0, LOW - GATE_RANK), (0, 0))).astype(BF16)

    chunk_id = jnp.arange(P) // CHUNK
    mask = chunk_id[None, :] <= chunk_id[:, None]
    wm_all = jnp.where(mask[None, None], gm_w_s, 0.0)
    tri = jnp.tril(jnp.ones((CHUNK, CHUNK), F32))
    triT = tri.T

    xs = x[0]
    saved = []
    for i in range(depth):
        j = i // 2
        h1 = rmsnorm_fwd(f"norm_mix_{i}", xs, norm_mix_g[i][None])
        if i % 2 == 0:
            zp = mm_fwd_col(f"gm_in_{i}", h1, gb, j * D, 2 * D)
            wm = wm_all[j].astype(BF16)
            bT = gm_b_s[j].T
            gated = gm_mid_fwd(f"gm_mid_{i}", zp, gm_ln_g[j][None], gm_ln_b[j][None], wm, bT)
            x_mid = mm_fwd_row(f"gm_out_{i}", gated, ga, gmo_off(j), D // 4, xs)
            mix = (h1, zp, gated)
        else:
            proj = mm_fwd_col(f"gla_in_{i}", h1, gc, j * D, 3 * D)
            lr = mm_plain(f"gla_low_{i}", h1, w_a1p[j], NN, BF16)
            o_raw, states = gla_scan_fwd(f"gla_scan_{i}", proj, lr, w_a2p[j], b_a[j][None], tri)
            og = gla_post_fwd(f"gla_post_{i}", o_raw, proj, gnorm[j][None])
            x_mid = mm_fwd_row(f"gla_out_{i}", og, ga, glo_off(j), D // 4, xs)
            mix = (h1, proj, lr, o_raw, states, og)
        h2 = rmsnorm_fwd(f"norm_ffn_{i}", x_mid, norm_ffn_g[i][None])
        act = mm_fwd_col(f"ffn_up_{i}", h2, ga, up_off(i), F, epilogue=lambda acc: jnp.maximum(acc, 0.0))
        x_out = mm_fwd_row(f"ffn_down_{i}", act, ga, down_off(i), D, x_mid, prologue=lambda a: a * a)
        saved.append((xs, x_mid, h2, act, mix))
        xs = x_out

    loss_part, dx, dxb, d_final_g = final_loss("final_loss", xs, final_g[None], loss_target[0])
    loss = lax.psum(loss_part[0, 0], ("x", "y", "c"))

    da = lax.empty(ga.shape, BF16)
    db = lax.empty(gb.shape, BF16)
    dc = lax.empty(gc.shape, BF16)
    d_mix_g, d_ffn_g = [None] * depth, [None] * depth
    d_ln_g, d_ln_b, d_w_s, d_b_s = [None] * n_gm, [None] * n_gm, [None] * n_gm, [None] * n_gm
    d_a1, d_a2, d_ba, d_gn = [None] * n_gla, [None] * n_gla, [None] * n_gla, [None] * n_gla
    for i in reversed(range(depth)):
        j = i // 2
        x_in, x_mid, h2, act, mix = saved[i]
        d_apre = mm_bwd_row_x(f"ffn_down_dx_{i}", dxb, ga, down_off(i), D, mul=act)
        da = mm_bwd_row_w(f"ffn_down_dw_{i}", act, dxb, da, down_off(i), D, prologue=lambda a: a * a)
        da = mm_bwd_col_w(f"ffn_up_dw_{i}", h2, d_apre, da, up_off(i))
        dh2 = mm_bwd_col_x(f"ffn_up_dx_{i}", d_apre, ga, up_off(i), D)
        dx, dxb, d_ffn_g[i] = rmsnorm_bwd(f"norm_ffn_bwd_{i}", x_mid, norm_ffn_g[i][None], dh2, dx)
        if i % 2 == 0:
            h1, zp, gated = mix
            d_gated = mm_bwd_row_x(f"gm_out_dx_{i}", dxb, ga, gmo_off(j), D // 4)
            da = mm_bwd_row_w(f"gm_out_dw_{i}", gated, dxb, da, gmo_off(j), D // 4)
            wm = wm_all[j].astype(BF16)
            wmT = jnp.swapaxes(wm_all[j], 1, 2).astype(BF16)
            dzp, d_ln_g[j], d_ln_b[j], dw, dbT = gm_mid_bwd(f"gm_mid_bwd_{i}", zp, d_gated, gm_ln_g[j][None], gm_ln_b[j][None],
                                                             wm, wmT, gm_b_s[j].T)
            d_w_s[j] = jnp.where(mask[None], dw, 0.0)
            d_b_s[j] = dbT.T
            db = mm_bwd_col_w(f"gm_in_dw_{i}", h1, dzp, db, j * D)
            dh1 = mm_bwd_col_x(f"gm_in_dx_{i}", dzp, gb, j * D, D)
        else:
            h1, proj, lr, o_raw, states, og = mix
            d_og = mm_bwd_row_x(f"gla_out_dx_{i}", dxb, ga, glo_off(j), D // 4)
            da = mm_bwd_row_w(f"gla_out_dw_{i}", og, dxb, da, glo_off(j), D // 4)
            d_oraw, d_r, d_gn[j] = gla_post_bwd(f"gla_post_bwd_{i}", d_og, o_raw, proj, gnorm[j][None])
            dq, dk, dv, dlogit, d_ba[j] = gla_scan_bwd(f"gla_scan_bwd_{i}", proj, lr, w_a2p[j], b_a[j][None], tri, triT, states, d_oraw)
            dproj = jnp.concatenate([dq, dk, dv, d_r], axis=1)
            dc = mm_bwd_col_w(f"gla_in_dw_{i}", h1, dproj, dc, j * D)
            dh1 = mm_bwd_col_x(f"gla_in_dx_{i}", dproj, gc, j * D, D)
            dlr = mm_plain(f"gla_gate_dlow_{i}", dlogit, w_a2p[j], NT, BF16)
            d_a2[j] = mm_plain(f"gla_gate_dw2_{i}", lr, dlogit, TN, F32)[:GATE_RANK]
            d_a1[j] = mm_plain(f"gla_gate_dw1_{i}", h1, dlr, TN, F32)[:, :GATE_RANK]
            dh1 = mm_plain(f"gla_gate_dx_{i}", dlr, w_a1p[j], NT, F32, add=dh1)
        dx, dxb, d_mix_g[i] = rmsnorm_bwd(f"norm_mix_bwd_{i}", x_in, norm_mix_g[i][None], dh1, dx)
    grad_x = dx[None]

    ra, rb, rc = exchange_sibling_halves([da, db, dc])
    pa = pair_add("pair_add_a", da, ra, c_idx)
    pb = pair_add("pair_add_b", db, rb, c_idx)
    pc = pair_add("pair_add_c", dc, rc, c_idx)
    qa, qb, qc = scatter_chips([pa, pb, pc])
    ha = sum_chips("sum_chips_a", pa, qa, chip_idx)
    hb = sum_chips("sum_chips_b", pb, qb, chip_idx)
    hc = sum_chips("sum_chips_c", pc, qc, chip_idx)
    fa, fb, fc = join_sibling_halves([ha, hb, hc])

    small_g = [jnp.concatenate(d_mix_g), jnp.concatenate(d_ffn_g), d_final_g[0], jnp.concatenate(d_ln_g), jnp.concatenate(d_ln_b),
               jnp.stack(d_w_s), jnp.stack(d_b_s), jnp.stack(d_a1), jnp.stack(d_a2), jnp.concatenate(d_ba), jnp.concatenate(d_gn)]
    small_shapes = [(depth, D), (depth, D), (D,), (n_gm, D), (n_gm, D), (n_gm, GM_GROUPS, P, P), (n_gm, GM_GROUPS, P),
                    (n_gla, D, GATE_RANK), (n_gla, GATE_RANK, D // 2), (n_gla, D // 2), (n_gla, D)]
    parts = allgather_devices("allgather_small_grads", _pack(small_g))
    red = _unpack(sum_devices("sum_small_grads", parts), small_shapes)
    g_rep = red[:7]
    g_a1 = lax.dynamic_slice_in_dim(red[7], chip * (D // 4), D // 4, axis=1)
    g_a2 = lax.dynamic_slice_in_dim(red[8], chip * (D // 8), D // 8, axis=2)
    g_ba = lax.dynamic_slice_in_dim(red[9], chip * (D // 8), D // 8, axis=1)
    g_gn = lax.dynamic_slice_in_dim(red[10], chip * (D // 4), D // 4, axis=1)
    g_small = g_rep + [g_a1, g_a2, g_ba, g_gn]
    w_small = [norm_mix_g, norm_ffn_g, final_g, gm_ln_g, gm_ln_b, gm_w_s, gm_b_s, gla_w_a1, gla_w_a2, gla_b_a, gla_norm_g]
    m_small = [m_norm_mix_g, m_norm_ffn_g, m_final_g, m_gm_ln_g, m_gm_ln_b, m_gm_w_s, m_gm_b_s, m_gla_w_a1, m_gla_w_a2, m_gla_b_a, m_gla_norm_g]
    v_small = [v_norm_mix_g, v_norm_ffn_g, v_final_g, v_gm_ln_g, v_gm_ln_b, v_gm_w_s, v_gm_b_s, v_gla_w_a1, v_gla_w_a2, v_gla_b_a, v_gla_norm_g]
    shapes_small = [w.shape for w in w_small]
    sm = adamw("adamw_small", _pack(g_small), 0, _pack(w_small), _pack(m_small), _pack(v_small))
    sm = [_unpack(o, shapes_small) for o in sm]

    def big(name, g, off, w, m, v):
        shape = w.shape
        two_d = (shape[0] * shape[1], shape[2])
        outs = adamw(name, g, off, w.reshape(two_d), m.reshape(two_d), v.reshape(two_d))
        return [o.reshape(shape) for o in outs]

    o_gm_in = big("adamw_gm_w_in", fb, 0, gm_w_in, m_gm_w_in, v_gm_w_in)
    o_gm_out = big("adamw_gm_w_out", fa, gmo_off(0), gm_w_out, m_gm_w_out, v_gm_w_out)
    o_gla_in = big("adamw_gla_w_in", fc, 0, gla_w_in, m_gla_w_in, v_gla_w_in)
    o_gla_o = big("adamw_gla_w_o", fa, glo_off(0), gla_w_o, m_gla_w_o, v_gla_w_o)
    o_up = big("adamw_ffn_w_up", fa, up_off(0), ffn_w_up, m_ffn_w_up, v_ffn_w_up)
    o_down = big("adamw_ffn_w_down", fa, down_off(0), ffn_w_down, m_ffn_w_down, v_ffn_w_down)

    def ordered(kind):
        s = sm[kind]
        return [s[0], s[1], s[2], o_gm_in[kind], s[3], s[4], s[5], s[6], o_gm_out[kind], o_gla_in[kind],
                s[7], s[8], s[9], s[10], o_gla_o[kind], o_up[kind], o_down[kind]]

    return (loss, grad_x, *ordered(0), *ordered(1), *ordered(2), *ordered(3))
```

```python
import functools

import jax
import jax.numpy as jnp
from jax import lax
from jax.experimental import pallas as pl
from jax.experimental.pallas import tpu as pltpu

F32 = jnp.float32
BF16 = jnp.bfloat16

EPS = 1e-6
CHUNK = 64
GM_BLOCK = 128
GM_GROUPS = 8
GLA_HEADS = 4
GATE_RANK = 16
GATE_TAU = 16.0
LOW = 128
N_CHIPS = 4
N_DEV = 8

ADAM_LR = 0.001
ADAM_B1 = 0.9
ADAM_B2 = 0.999
ADAM_EPS = 1e-08
ADAM_WD = 0.01
ADAM_STEP = 10

V7X_VMEM_LIMIT = 48 * 1024 * 1024
LANES = 128
SUBLANES = 8
MESH = pl.DeviceIdType.MESH
HIGHEST = lax.Precision.HIGHEST


def _pick(n, cap):
    if n <= cap:
        return n
    best = LANES
    for t in range(LANES, cap + 1, LANES):
        if n % t == 0:
            best = t
    return best


def _cparams(sem=None):
    return pltpu.CompilerParams(dimension_semantics=sem, vmem_limit_bytes=V7X_VMEM_LIMIT)


ANY = pl.BlockSpec(memory_space=pl.ANY)


NN = ((1,), (0,))
NT = ((1,), (1,))
TN = ((0,), (0,))


def _mm(name, a, b, *, dims, grid, a_spec, b_spec, out_spec, out_shape, acc_shape, extra=(), extra_specs=(),
        prologue=None, epilogue=None, slab=None, b_reshape=None):
    nk = grid[2]
    n_extra = len(extra)

    def body(*refs):
        a_ref, b_ref = refs[0], refs[1]
        ex = refs[2:2 + n_extra]
        pos = 2 + n_extra + (1 if slab is not None else 0)
        o_ref = refs[pos]
        av = a_ref[...]
        if prologue is not None:
            av = prologue(av)
        bv = b_ref[...]
        if b_reshape is not None:
            bv = bv.reshape(b_reshape)
        p = lax.dot_general(av, bv, (dims, ((), ())), preferred_element_type=F32)

        def finish(acc):
            r = acc if epilogue is None else epilogue(acc, *ex)
            o_ref[...] = r.astype(o_ref.dtype)

        if nk == 1:
            finish(p)
        else:
            acc_ref = refs[pos + 1]
            k = pl.program_id(2)

            @pl.when(k == 0)
            def _():
                acc_ref[...] = p

            @pl.when(jnp.logical_and(k > 0, k < nk - 1))
            def _():
                acc_ref[...] += p

            @pl.when(k == nk - 1)
            def _():
                finish(acc_ref[...] + p)

    inputs = [a, b, *extra]
    in_specs = [a_spec, b_spec, *extra_specs]
    aliases = {}
    if slab is not None:
        inputs.append(slab)
        in_specs.append(ANY)
        aliases = {len(inputs) - 1: 0}
    return pl.pallas_call(
        body, name=name, grid=grid, in_specs=in_specs, out_specs=out_spec, out_shape=out_shape,
        scratch_shapes=[pltpu.VMEM(acc_shape, F32)] if nk > 1 else [],
        input_output_aliases=aliases,
        compiler_params=_cparams(("parallel", "parallel", "arbitrary")),
    )(*inputs)


def _sds(shape, dtype):
    return jax.ShapeDtypeStruct(shape, dtype)


def mm_fwd_col(name, h, slab, off, n_out, epilogue=None, out_dtype=BF16):
    S, D = h.shape
    W = slab.shape[2]
    tm, tn = min(1024, S), _pick(W, 1024)
    wps = W // tn
    return _mm(name, h, slab, dims=NN, grid=(S // tm, n_out // tn, 1),
               a_spec=pl.BlockSpec((tm, D), lambda i, j, k: (i, 0)),
               b_spec=pl.BlockSpec((None, D, tn), lambda i, j, k: (j // wps, off // D, j % wps)),
               out_spec=pl.BlockSpec((tm, tn), lambda i, j, k: (i, j)),
               out_shape=_sds((S, n_out), out_dtype), acc_shape=(tm, tn), epilogue=epilogue)


def mm_fwd_row(name, a, slab, off, ksh, res, prologue=None):
    S, K = a.shape
    D = slab.shape[2]
    tm, tn, tk = min(1024, S), min(1024, D), min(2048, ksh)
    res_spec = pl.BlockSpec((tm, tn), lambda i, j, k: (i, j))
    if K <= 2048 and ksh < K:
        return _mm(name, a, slab, dims=NN, grid=(S // tm, D // tn, 1),
                   a_spec=pl.BlockSpec((tm, K), lambda i, j, k: (i, 0)),
                   b_spec=pl.BlockSpec((N_CHIPS, ksh, tn), lambda i, j, k: (0, off // ksh, j)),
                   out_spec=res_spec, out_shape=_sds((S, D), F32), acc_shape=(tm, tn),
                   extra=(res,), extra_specs=(res_spec,), prologue=prologue, epilogue=lambda acc, r: acc + r[...],
                   b_reshape=(K, tn))
    kps = ksh // tk
    return _mm(name, a, slab, dims=NN, grid=(S // tm, D // tn, K // tk),
               a_spec=pl.BlockSpec((tm, tk), lambda i, j, k: (i, k)),
               b_spec=pl.BlockSpec((None, tk, tn), lambda i, j, k: (k // kps, off // tk + k % kps, j)),
               out_spec=pl.BlockSpec((tm, tn), lambda i, j, k: (i, j)),
               out_shape=_sds((S, D), F32), acc_shape=(tm, tn),
               extra=(res,), extra_specs=(pl.BlockSpec((tm, tn), lambda i, j, k: (i, j)),),
               prologue=prologue, epilogue=lambda acc, r: acc + r[...])


def mm_bwd_col_x(name, dz, slab, off, D, add=None):
    S, N = dz.shape
    W = slab.shape[2]
    tm, to, tr = min(1024, S), min(1024, D), _pick(W, 2048)
    wps = W // tr
    extra, extra_specs, epi = (), (), None
    if add is not None:
        extra, extra_specs = (add,), (pl.BlockSpec((tm, to), lambda i, j, k: (i, j)),)
        epi = lambda acc, r: acc + r[...]
    return _mm(name, dz, slab, dims=NT, grid=(S // tm, D // to, N // tr),
               a_spec=pl.BlockSpec((tm, tr), lambda i, j, k: (i, k)),
               b_spec=pl.BlockSpec((None, to, tr), lambda i, j, k: (k // wps, off // to + j, k % wps)),
               out_spec=pl.BlockSpec((tm, to), lambda i, j, k: (i, j)),
               out_shape=_sds((S, D), F32), acc_shape=(tm, to), extra=extra, extra_specs=extra_specs, epilogue=epi)


def mm_bwd_row_x(name, dxb, slab, off, ksh, mul=None):
    S, D = dxb.shape
    tm, tn = min(1024, S), min(1024, ksh)
    kps = ksh // tn
    extra, extra_specs, epi = (), (), None
    if mul is not None:
        extra, extra_specs = (mul,), (pl.BlockSpec((tm, tn), lambda i, j, k: (i, j)),)
        epi = lambda acc, r: acc * (2.0 * r[...].astype(F32))
    return _mm(name, dxb, slab, dims=NT, grid=(S // tm, 4 * kps, 1),
               a_spec=pl.BlockSpec((tm, D), lambda i, j, k: (i, 0)),
               b_spec=pl.BlockSpec((None, tn, D), lambda i, j, k: (j // kps, off // tn + j % kps, 0)),
               out_spec=pl.BlockSpec((tm, tn), lambda i, j, k: (i, j)),
               out_shape=_sds((S, 4 * ksh), BF16), acc_shape=(tm, tn), extra=extra, extra_specs=extra_specs, epilogue=epi)


def mm_bwd_col_w(name, h, dz, gslab, off):
    S, D = h.shape
    N = dz.shape[1]
    W = gslab.shape[2]
    tk, tn, ts = min(1024, D), _pick(W, 1024), min(2048, S)
    wps = W // tn
    return _mm(name, h, dz, dims=TN, grid=(D // tk, N // tn, S // ts),
               a_spec=pl.BlockSpec((ts, tk), lambda i, j, k: (k, i)),
               b_spec=pl.BlockSpec((ts, tn), lambda i, j, k: (k, j)),
               out_spec=pl.BlockSpec((None, tk, tn), lambda i, j, k: (j // wps, off // tk + i, j % wps)),
               out_shape=_sds(gslab.shape, gslab.dtype), acc_shape=(tk, tn), slab=gslab)


def mm_bwd_row_w(name, a, dxb, gslab, off, ksh, prologue=None):
    S, K = a.shape
    D = dxb.shape[1]
    tk, tn, ts = min(1024, ksh), min(1024, D), min(2048, S)
    kps = ksh // tk
    return _mm(name, a, dxb, dims=TN, grid=(K // tk, D // tn, S // ts),
               a_spec=pl.BlockSpec((ts, tk), lambda i, j, k: (k, i)),
               b_spec=pl.BlockSpec((ts, tn), lambda i, j, k: (k, j)),
               out_spec=pl.BlockSpec((None, tk, tn), lambda i, j, k: (i // kps, off // tk + i % kps, j)),
               out_shape=_sds(gslab.shape, gslab.dtype), acc_shape=(tk, tn), slab=gslab, prologue=prologue)


def mm_plain(name, a, b, dims, out_dtype, add=None):
    if dims == NN:
        M, N = a.shape[0], b.shape[1]
    elif dims == NT:
        M, N = a.shape[0], b.shape[0]
    else:
        M, N = a.shape[1], b.shape[1]
    red = a.shape[0] if dims == TN else a.shape[1]
    tm, tn = min(1024, M), min(1024, N)
    tr = min(1024, red) if dims == TN else red
    nk = red // tr
    if dims == TN:
        a_spec = pl.BlockSpec((tr, tm), lambda i, j, k: (k, i))
        b_spec = pl.BlockSpec((tr, tn), lambda i, j, k: (k, j))
    elif dims == NN:
        a_spec = pl.BlockSpec((tm, tr), lambda i, j, k: (i, k))
        b_spec = pl.BlockSpec((tr, tn), lambda i, j, k: (k, j))
    else:
        a_spec = pl.BlockSpec((tm, tr), lambda i, j, k: (i, k))
        b_spec = pl.BlockSpec((tn, tr), lambda i, j, k: (j, k))
    extra, extra_specs, epi = (), (), None
    if add is not None:
        extra, extra_specs = (add,), (pl.BlockSpec((tm, tn), lambda i, j, k: (i, j)),)
        epi = lambda acc, r: acc + r[...]
    return _mm(name, a, b, dims=dims, grid=(M // tm, N // tn, nk), a_spec=a_spec, b_spec=b_spec,
               out_spec=pl.BlockSpec((tm, tn), lambda i, j, k: (i, j)),
               out_shape=_sds((M, N), out_dtype), acc_shape=(tm, tn), extra=extra, extra_specs=extra_specs, epilogue=epi)


ROWS = 256


def rmsnorm_fwd(name, x, g):
    S, D = x.shape
    tr = min(ROWS, S)

    def body(x_ref, g_ref, h_ref):
        xv = x_ref[...]
        rstd = lax.rsqrt(jnp.mean(xv * xv, axis=-1, keepdims=True) + EPS)
        h_ref[...] = (xv * rstd * g_ref[...]).astype(BF16)

    return pl.pallas_call(
        body, name=name, grid=(S // tr,),
        in_specs=[pl.BlockSpec((tr, D), lambda i: (i, 0)), pl.BlockSpec((1, D), lambda i: (0, 0))],
        out_specs=pl.BlockSpec((tr, D), lambda i: (i, 0)), out_shape=_sds((S, D), BF16),
        compiler_params=_cparams(("parallel",)),
    )(x, g)


def rmsnorm_bwd(name, x, g, dh, dres):
    S, D = x.shape
    tr = min(ROWS, S)

    def body(x_ref, g_ref, dh_ref, dres_ref, dx_ref, dxb_ref, dg_ref):
        xv = x_ref[...]
        rstd = lax.rsqrt(jnp.mean(xv * xv, axis=-1, keepdims=True) + EPS)
        xh = xv * rstd
        dy = dh_ref[...]
        dxh = dy * g_ref[...]
        dx = dres_ref[...] + rstd * (dxh - xh * jnp.mean(dxh * xh, axis=-1, keepdims=True))
        dx_ref[...] = dx
        dxb_ref[...] = dx.astype(BF16)

        @pl.when(pl.program_id(0) == 0)
        def _():
            dg_ref[...] = jnp.zeros_like(dg_ref)

        dg_ref[...] += jnp.sum(dy * xh, axis=0, keepdims=True)

    row = pl.BlockSpec((tr, D), lambda i: (i, 0))
    vec = pl.BlockSpec((1, D), lambda i: (0, 0))
    return pl.pallas_call(
        body, name=name, grid=(S // tr,), in_specs=[row, vec, row, row], out_specs=[row, row, vec],
        out_shape=[_sds((S, D), F32), _sds((S, D), BF16), _sds((1, D), F32)],
        compiler_params=_cparams(("arbitrary",)),
    )(x, g, dh, dres)


def final_loss(name, x, g, target):
    S, D = x.shape
    tr = min(ROWS, S)

    def body(x_ref, g_ref, t_ref, loss_ref, dx_ref, dxb_ref, dg_ref):
        xv = x_ref[...]
        gv = g_ref[...]
        rstd = lax.rsqrt(jnp.mean(xv * xv, axis=-1, keepdims=True) + EPS)
        xh = xv * rstd
        err = xh * gv - t_ref[...]
        dy = err * (1.0 / D)
        dxh = dy * gv
        dx = rstd * (dxh - xh * jnp.mean(dxh * xh, axis=-1, keepdims=True))
        dx_ref[...] = dx
        dxb_ref[...] = dx.astype(BF16)

        @pl.when(pl.program_id(0) == 0)
        def _():
            dg_ref[...] = jnp.zeros_like(dg_ref)
            loss_ref[...] = jnp.zeros_like(loss_ref)

        dg_ref[...] += jnp.sum(dy * xh, axis=0, keepdims=True)
        loss_ref[...] += 0.5 * jnp.sum(jnp.mean(err * err, axis=-1, keepdims=True))

    row = pl.BlockSpec((tr, D), lambda i: (i, 0))
    vec = pl.BlockSpec((1, D), lambda i: (0, 0))
    return pl.pallas_call(
        body, name=name, grid=(S // tr,), in_specs=[row, vec, row],
        out_specs=[pl.BlockSpec((SUBLANES, LANES), lambda i: (0, 0)), row, row, vec],
        out_shape=[_sds((SUBLANES, LANES), F32), _sds((S, D), F32), _sds((S, D), BF16), _sds((1, D), F32)],
        compiler_params=_cparams(("arbitrary",)),
    )(x, g, target)


def _gelu(x):
    return 0.5 * x * (1.0 + lax.erf(x * 0.7071067811865476))


def _gelu_grad(x):
    return 0.5 * (1.0 + lax.erf(x * 0.7071067811865476)) + x * jnp.exp(-0.5 * x * x) * 0.3989422804014327


def _gm_common(zp, lng, lnb, D):
    z = _gelu(zp)
    u, v = z[:, :D], z[:, D:]
    xc = v - jnp.mean(v, axis=-1, keepdims=True)
    rstd = lax.rsqrt(jnp.mean(xc * xc, axis=-1, keepdims=True) + EPS)
    xh = xc * rstd
    return u, xh, rstd, xh * lng + lnb


def gm_mid_fwd(name, zp, lng, lnb, wm, bT):
    S, D2 = zp.shape
    D = D2 // 2
    dg = D // GM_GROUPS
    P = GM_BLOCK

    def body(z_ref, lng_ref, lnb_ref, wm_ref, bT_ref, o_ref):
        u, _, _, vn = _gm_common(z_ref[...].astype(F32), lng_ref[...], lnb_ref[...], D)
        vnb = vn.astype(BF16)
        for gi in range(GM_GROUPS):
            cols = slice(gi * dg, (gi + 1) * dg)
            mixed = jnp.dot(wm_ref[gi], vnb[:, cols], preferred_element_type=F32) + bT_ref[:, gi:gi + 1]
            o_ref[:, cols] = (u[:, cols] * mixed).astype(BF16)

    vec = pl.BlockSpec((1, D), lambda i: (0, 0))
    return pl.pallas_call(
        body, name=name, grid=(S // P,),
        in_specs=[pl.BlockSpec((P, D2), lambda i: (i, 0)), vec, vec,
                  pl.BlockSpec((GM_GROUPS, P, P), lambda i: (0, 0, 0)), pl.BlockSpec((P, GM_GROUPS), lambda i: (0, 0))],
        out_specs=pl.BlockSpec((P, D), lambda i: (i, 0)), out_shape=_sds((S, D), BF16),
        compiler_params=_cparams(("parallel",)),
    )(zp, lng, lnb, wm, bT)


def gm_mid_bwd(name, zp, dgated, lng, lnb, wm, wmT, bT):
    S, D2 = zp.shape
    D = D2 // 2
    dg = D // GM_GROUPS
    P = GM_BLOCK

    def body(z_ref, dgt_ref, lng_ref, lnb_ref, wm_ref, wmT_ref, bT_ref, dz_ref, dlng_ref, dlnb_ref, dw_ref, dbT_ref, dvn_ref):
        @pl.when(pl.program_id(0) == 0)
        def _():
            dlng_ref[...] = jnp.zeros_like(dlng_ref)
            dlnb_ref[...] = jnp.zeros_like(dlnb_ref)
            dw_ref[...] = jnp.zeros_like(dw_ref)
            dbT_ref[...] = jnp.zeros_like(dbT_ref)

        zp_v = z_ref[...].astype(F32)
        lng_v = lng_ref[...]
        u, xh, rstd, vn = _gm_common(zp_v, lng_v, lnb_ref[...], D)
        vnb = vn.astype(BF16)
        dgt = dgt_ref[...].astype(F32)
        for gi in range(GM_GROUPS):
            cols = slice(gi * dg, (gi + 1) * dg)
            mixed = jnp.dot(wm_ref[gi], vnb[:, cols], preferred_element_type=F32) + bT_ref[:, gi:gi + 1]
            dm = dgt[:, cols] * u[:, cols]
            dmb = dm.astype(BF16)
            dz_ref[:, cols] = (dgt[:, cols] * mixed * _gelu_grad(zp_v[:, cols])).astype(BF16)
            dbT_ref[:, gi:gi + 1] += jnp.sum(dm, axis=1, keepdims=True)
            dw_ref[gi] += lax.dot_general(dmb, vnb[:, cols], (NT, ((), ())), preferred_element_type=F32)
            dvn_ref[:, cols] = jnp.dot(wmT_ref[gi], dmb, preferred_element_type=F32)
        dvn = dvn_ref[...]
        dlng_ref[...] += jnp.sum(dvn * xh, axis=0, keepdims=True)
        dlnb_ref[...] += jnp.sum(dvn, axis=0, keepdims=True)
        dyg = dvn * lng_v
        dv = rstd * (dyg - jnp.mean(dyg, axis=-1, keepdims=True) - xh * jnp.mean(dyg * xh, axis=-1, keepdims=True))
        dz_ref[:, D:] = (dv * _gelu_grad(zp_v[:, D:])).astype(BF16)

    vec = pl.BlockSpec((1, D), lambda i: (0, 0))
    wsp = pl.BlockSpec((GM_GROUPS, P, P), lambda i: (0, 0, 0))
    bsp = pl.BlockSpec((P, GM_GROUPS), lambda i: (0, 0))
    return pl.pallas_call(
        body, name=name, grid=(S // P,),
        in_specs=[pl.BlockSpec((P, D2), lambda i: (i, 0)), pl.BlockSpec((P, D), lambda i: (i, 0)), vec, vec, wsp, wsp, bsp],
        out_specs=[pl.BlockSpec((P, D2), lambda i: (i, 0)), vec, vec, wsp, bsp],
        out_shape=[_sds((S, D2), BF16), _sds((1, D), F32), _sds((1, D), F32), _sds((GM_GROUPS, P, P), F32), _sds((P, GM_GROUPS), F32)],
        scratch_shapes=[pltpu.VMEM((P, D), F32)],
        compiler_params=_cparams(("arbitrary",)),
    )(zp, dgated, lng, lnb, wm, wmT, bT)


def _gla_gate(lr, w2, ba, tri):
    logit = jnp.dot(lr, w2, preferred_element_type=F32) + ba
    la = (jnp.minimum(logit, 0.0) - jnp.log1p(jnp.exp(-jnp.abs(logit)))) * (1.0 / GATE_TAU)
    g = jnp.dot(tri, la, preferred_element_type=F32, precision=HIGHEST)
    return logit, g


def gla_scan_fwd(name, proj, lr, w2p, ba, tri):
    S, D3 = proj.shape
    D = D3 // 3
    H, C = GLA_HEADS, CHUNK
    dk, dv = D // 2 // H, D // H
    NC = S // C
    scale = dk ** -0.5

    def body(q_ref, k_ref, v_ref, lr_ref, w2_ref, ba_ref, tri_ref, o_ref, st_ref, state):
        @pl.when(pl.program_id(1) == 0)
        def _():
            state[...] = jnp.zeros_like(state)

        _, g = _gla_gate(lr_ref[...], w2_ref[...], ba_ref[...], tri_ref[...])
        gend = g[C - 1:C, :]
        kdec = (k_ref[...].astype(F32) * jnp.exp(gend - g)).astype(BF16)
        kv = lax.dot_general(v_ref[...], kdec, (TN, ((), ())), preferred_element_type=F32)
        new = jnp.exp(gend) * state[...] + kv
        state[...] = new
        nb = new.astype(BF16)
        st_ref[...] = nb
        qs = (q_ref[...].astype(F32) * scale).astype(BF16)
        o_ref[...] = lax.dot_general(qs, nb, (NT, ((), ())), preferred_element_type=F32).astype(BF16)

    return pl.pallas_call(
        body, name=name, grid=(H, NC),
        in_specs=[pl.BlockSpec((C, dk), lambda h, t: (t, h)), pl.BlockSpec((C, dk), lambda h, t: (t, H + h)),
                  pl.BlockSpec((C, dv), lambda h, t: (t, H + h)), pl.BlockSpec((C, LOW), lambda h, t: (t, 0)),
                  pl.BlockSpec((LOW, dk), lambda h, t: (0, h)), pl.BlockSpec((1, dk), lambda h, t: (0, h)),
                  pl.BlockSpec((C, C), lambda h, t: (0, 0))],
        out_specs=[pl.BlockSpec((C, dv), lambda h, t: (t, h)), pl.BlockSpec((None, None, dv, dk), lambda h, t: (t, h, 0, 0))],
        out_shape=[_sds((S, D), BF16), _sds((NC, H, dv, dk), BF16)],
        scratch_shapes=[pltpu.VMEM((dv, dk), F32)],
        compiler_params=_cparams(("parallel", "arbitrary")),
    )(proj, proj, proj, lr, w2p, ba, tri)


def gla_scan_bwd(name, proj, lr, w2p, ba, tri, triT, states, do):
    S, D3 = proj.shape
    D = D3 // 3
    H, C = GLA_HEADS, CHUNK
    dk, dv = D // 2 // H, D // H
    NC = S // C
    scale = dk ** -0.5

    def body(q_ref, k_ref, v_ref, lr_ref, w2_ref, ba_ref, tri_ref, triT_ref, st_ref, sp_ref, do_ref,
             dq_ref, dk_ref, dv_ref, dl_ref, dba_ref, dstate):
        t = pl.program_id(1)

        @pl.when(t == 0)
        def _():
            dstate[...] = jnp.zeros_like(dstate)
            dba_ref[...] = jnp.zeros_like(dba_ref)

        logit, g = _gla_gate(lr_ref[...], w2_ref[...], ba_ref[...], tri_ref[...])
        gend = g[C - 1:C, :]
        e = jnp.exp(gend - g)
        kf = k_ref[...].astype(F32)
        kdec = (kf * e).astype(BF16)
        dec = jnp.exp(gend)
        qs = (q_ref[...].astype(F32) * scale).astype(BF16)
        dob = do_ref[...]
        dq_ref[...] = (jnp.dot(dob, st_ref[...], preferred_element_type=F32) * scale).astype(BF16)
        ds = dstate[...] + lax.dot_general(dob, qs, (TN, ((), ())), preferred_element_type=F32)
        dsb = ds.astype(BF16)
        dkdec = jnp.dot(v_ref[...], dsb, preferred_element_type=F32)
        dv_ref[...] = lax.dot_general(kdec, dsb, (NT, ((), ())), preferred_element_type=F32).astype(BF16)
        has_prev = (t < NC - 1).astype(F32)
        ddec = jnp.sum(ds * sp_ref[...].astype(F32), axis=0, keepdims=True) * has_prev
        dstate[...] = dec * ds
        dk_ref[...] = (dkdec * e).astype(BF16)
        dd = dkdec * kf * e
        dgend = jnp.sum(dd, axis=0, keepdims=True) + ddec * dec
        last = lax.broadcasted_iota(jnp.int32, (C, 1), 0) == C - 1
        dg = jnp.where(last, dgend, 0.0) - dd
        dla = jnp.dot(triT_ref[...], dg, preferred_element_type=F32, precision=HIGHEST)
        dlogit = dla * (1.0 / GATE_TAU) * (1.0 - jax.nn.sigmoid(logit))
        dl_ref[...] = dlogit.astype(BF16)
        dba_ref[...] += jnp.sum(dlogit, axis=0, keepdims=True)

    rev = lambda t: NC - 1 - t
    return pl.pallas_call(
        body, name=name, grid=(H, NC),
        in_specs=[pl.BlockSpec((C, dk), lambda h, t: (rev(t), h)), pl.BlockSpec((C, dk), lambda h, t: (rev(t), H + h)),
                  pl.BlockSpec((C, dv), lambda h, t: (rev(t), H + h)), pl.BlockSpec((C, LOW), lambda h, t: (rev(t), 0)),
                  pl.BlockSpec((LOW, dk), lambda h, t: (0, h)), pl.BlockSpec((1, dk), lambda h, t: (0, h)),
                  pl.BlockSpec((C, C), lambda h, t: (0, 0)), pl.BlockSpec((C, C), lambda h, t: (0, 0)),
                  pl.BlockSpec((None, None, dv, dk), lambda h, t: (rev(t), h, 0, 0)),
                  pl.BlockSpec((None, None, dv, dk), lambda h, t: (jnp.maximum(rev(t) - 1, 0), h, 0, 0)),
                  pl.BlockSpec((C, dv), lambda h, t: (rev(t), h))],
        out_specs=[pl.BlockSpec((C, dk), lambda h, t: (rev(t), h)), pl.BlockSpec((C, dk), lambda h, t: (rev(t), h)),
                   pl.BlockSpec((C, dv), lambda h, t: (rev(t), h)), pl.BlockSpec((C, dk), lambda h, t: (rev(t), h)),
                   pl.BlockSpec((1, dk), lambda h, t: (0, h))],
        out_shape=[_sds((S, D // 2), BF16), _sds((S, D // 2), BF16), _sds((S, D), BF16), _sds((S, D // 2), BF16),
                   _sds((1, D // 2), F32)],
        scratch_shapes=[pltpu.VMEM((dv, dk), F32)],
        compiler_params=_cparams(("parallel", "arbitrary")),
    )(proj, proj, proj, lr, w2p, ba, tri, triT, states, states, do)


def _gla_post_common(o, r, ng):
    rs = lax.rsqrt(jnp.mean(o * o, axis=-1, keepdims=True) + EPS)
    oh = o * rs
    sig = jax.nn.sigmoid(r)
    return rs, oh, oh * ng, sig, r * sig


def gla_post_fwd(name, o_raw, proj, ng):
    S, D = o_raw.shape
    dv = D // GLA_HEADS
    tr = min(ROWS, S)

    def body(o_ref, r_ref, ng_ref, og_ref):
        for hh in range(GLA_HEADS):
            cols = slice(hh * dv, (hh + 1) * dv)
            _, _, on, _, sil = _gla_post_common(o_ref[:, cols].astype(F32), r_ref[:, cols].astype(F32), ng_ref[:, cols])
            og_ref[:, cols] = (on * sil).astype(BF16)

    row = pl.BlockSpec((tr, D), lambda i: (i, 0))
    return pl.pallas_call(
        body, name=name, grid=(S // tr,),
        in_specs=[row, pl.BlockSpec((tr, D), lambda i: (i, 2)), pl.BlockSpec((1, D), lambda i: (0, 0))],
        out_specs=row, out_shape=_sds((S, D), BF16), compiler_params=_cparams(("parallel",)),
    )(o_raw, proj, ng)


def gla_post_bwd(name, dog, o_raw, proj, ng):
    S, D = o_raw.shape
    dv = D // GLA_HEADS
    tr = min(ROWS, S)

    def body(dog_ref, o_ref, r_ref, ng_ref, do_ref, dr_ref, dng_ref):
        @pl.when(pl.program_id(0) == 0)
        def _():
            dng_ref[...] = jnp.zeros_like(dng_ref)

        for hh in range(GLA_HEADS):
            cols = slice(hh * dv, (hh + 1) * dv)
            r = r_ref[:, cols].astype(F32)
            ngv = ng_ref[:, cols]
            rs, oh, on, sig, sil = _gla_post_common(o_ref[:, cols].astype(F32), r, ngv)
            dogv = dog_ref[:, cols].astype(F32)
            don = dogv * sil
            dr_ref[:, cols] = (dogv * on * (sig * (1.0 + r * (1.0 - sig)))).astype(BF16)
            dng_ref[:, cols] += jnp.sum(don * oh, axis=0, keepdims=True)
            doh = don * ngv
            do_ref[:, cols] = (rs * (doh - oh * jnp.mean(doh * oh, axis=-1, keepdims=True))).astype(BF16)

    row = pl.BlockSpec((tr, D), lambda i: (i, 0))
    vec = pl.BlockSpec((1, D), lambda i: (0, 0))
    return pl.pallas_call(
        body, name=name, grid=(S // tr,),
        in_specs=[row, row, pl.BlockSpec((tr, D), lambda i: (i, 2)), vec],
        out_specs=[row, row, vec], out_shape=[_sds((S, D), BF16), _sds((S, D), BF16), _sds((1, D), F32)],
        compiler_params=_cparams(("arbitrary",)),
    )(dog, o_raw, proj, ng)


def _place():
    return lax.axis_index("x"), lax.axis_index("y"), lax.axis_index("c")


def cast_into(name, w, buf, off, place):
    r, W = w.shape
    tr = _pick(r, 512) if r % 512 == 0 else r

    def body(p_ref, w_ref, buf_ref, o_ref):
        o_ref[...] = w_ref[...].astype(BF16)

    return pl.pallas_call(
        body, name=name,
        grid_spec=pltpu.PrefetchScalarGridSpec(
            num_scalar_prefetch=1, grid=(r // tr,),
            in_specs=[pl.BlockSpec((tr, W), lambda i, p: (i, 0)), ANY],
            out_specs=pl.BlockSpec((None, tr, W), lambda i, p: (p[0], off // tr + i, 0))),
        out_shape=_sds(buf.shape, buf.dtype), input_output_aliases={2: 0},
        compiler_params=_cparams(("parallel",)),
    )(place, w, buf)


def allgather_chips(bufs):
    n = len(bufs)

    def body(*refs):
        outs = refs[n:2 * n]
        send_sems, recv_sems = refs[2 * n:]
        x, y, c = _place()
        chips = [(1 - x, y), (x, 1 - y), (1 - x, 1 - y)]
        sibling = (x, y, 1 - c)
        me = 2 * x + y

        def half(a, chip, hc):
            hr = bufs[a].shape[1] // 2
            return outs[a].at[chip, pl.ds(hc * hr, hr), :]

        def copy(a, k, src, dst, to):
            return pltpu.make_async_remote_copy(src_ref=src, dst_ref=dst, send_sem=send_sems.at[a, k],
                                                recv_sem=recv_sems.at[a, k], device_id=to, device_id_type=MESH)

        sends = []
        for a in range(n):
            for j, (px, py) in enumerate(chips):
                cp = copy(a, j, half(a, me, c), half(a, me, c), (px, py, c))
                cp.start()
                sends.append(cp)
        for a in range(n):
            for j, (px, py) in enumerate(chips):
                src_chip = 2 * px + py
                copy(a, j, half(a, src_chip, c), half(a, src_chip, c), (px, py, c)).wait_recv()
                fw = copy(a, 3 + j, half(a, src_chip, c), half(a, src_chip, c), sibling)
                fw.start()
                sends.append(fw)
        for a in range(n):
            for j, (px, py) in enumerate(chips):
                src_chip = 2 * px + py
                copy(a, 3 + j, half(a, src_chip, 1 - c), half(a, src_chip, 1 - c), sibling).wait_recv()
        for cp in sends:
            cp.wait_send()

    return pl.pallas_call(
        body, name="allgather_chips",
        in_specs=[ANY] * n, out_specs=[ANY] * n,
        out_shape=[_sds(b.shape, b.dtype) for b in bufs],
        input_output_aliases={a: a for a in range(n)},
        scratch_shapes=[pltpu.SemaphoreType.DMA((n, 6)), pltpu.SemaphoreType.DMA((n, 6))],
    )(*bufs)


def allgather_devices(name, v):
    def body(v_ref, out_ref, send_sems, recv_sems, local_sem):
        x, y, c = _place()
        me = 4 * x + 2 * y + c
        mine = pltpu.make_async_copy(v_ref, out_ref.at[me], local_sem)
        mine.start()

        def peer(k):
            return (1 - x if k & 4 else x, 1 - y if k & 2 else y, 1 - c if k & 1 else c)

        def copy(k, src, dst):
            return pltpu.make_async_remote_copy(src_ref=src, dst_ref=dst, send_sem=send_sems.at[k - 1],
                                                recv_sem=recv_sems.at[k - 1], device_id=peer(k), device_id_type=MESH)

        sends = [copy(k, v_ref, out_ref.at[me]) for k in range(1, N_DEV)]
        for cp in sends:
            cp.start()
        for k in range(1, N_DEV):
            px, py, pc = peer(k)
            them = 4 * px + 2 * py + pc
            copy(k, out_ref.at[them], out_ref.at[them]).wait_recv()
        for cp in sends:
            cp.wait_send()
        mine.wait()

    return pl.pallas_call(
        body, name=name, in_specs=[ANY], out_specs=ANY, out_shape=_sds((N_DEV,) + v.shape, v.dtype),
        scratch_shapes=[pltpu.SemaphoreType.DMA((N_DEV - 1,)), pltpu.SemaphoreType.DMA((N_DEV - 1,)), pltpu.SemaphoreType.DMA],
    )(v)


def exchange_sibling_halves(gslabs):
    n = len(gslabs)

    def body(*refs):
        ins, outs = refs[:n], refs[n:2 * n]
        send_sems, recv_sems = refs[2 * n:]
        x, y, c = _place()
        cps = []
        for a in range(n):
            hr = gslabs[a].shape[1] // 2
            cp = pltpu.make_async_remote_copy(src_ref=ins[a].at[:, pl.ds((1 - c) * hr, hr), :], dst_ref=outs[a],
                                              send_sem=send_sems.at[a], recv_sem=recv_sems.at[a],
                                              device_id=(x, y, 1 - c), device_id_type=MESH)
            cp.start()
            cps.append(cp)
        for cp in cps:
            cp.wait()

    return pl.pallas_call(
        body, name="exchange_sibling_halves", in_specs=[ANY] * n, out_specs=[ANY] * n,
        out_shape=[_sds((N_CHIPS, g.shape[1] // 2, g.shape[2]), g.dtype) for g in gslabs],
        scratch_shapes=[pltpu.SemaphoreType.DMA((n,)), pltpu.SemaphoreType.DMA((n,))],
    )(*gslabs)


def scatter_chips(pslabs):
    n = len(pslabs)

    def body(*refs):
        ins, outs = refs[:n], refs[n:2 * n]
        send_sems, recv_sems = refs[2 * n:]
        x, y, c = _place()
        chips = [(1 - x, y), (x, 1 - y), (1 - x, 1 - y)]
        cps = []
        for a in range(n):
            for j, (px, py) in enumerate(chips):
                cp = pltpu.make_async_remote_copy(src_ref=ins[a].at[2 * px + py], dst_ref=outs[a].at[j],
                                                  send_sem=send_sems.at[a, j], recv_sem=recv_sems.at[a, j],
                                                  device_id=(px, py, c), device_id_type=MESH)
                cp.start()
                cps.append(cp)
        for cp in cps:
            cp.wait()

    return pl.pallas_call(
        body, name="scatter_chips", in_specs=[ANY] * n, out_specs=[ANY] * n,
        out_shape=[_sds((3,) + p.shape[1:], p.dtype) for p in pslabs],
        scratch_shapes=[pltpu.SemaphoreType.DMA((n, 3)), pltpu.SemaphoreType.DMA((n, 3))],
    )(*pslabs)


def join_sibling_halves(fulls):
    n = len(fulls)

    def body(*refs):
        outs = refs[n:2 * n]
        send_sems, recv_sems = refs[2 * n:]
        x, y, c = _place()
        cps = []
        for a in range(n):
            hr = fulls[a].shape[0] // 2
            mine = outs[a].at[pl.ds(c * hr, hr), :]
            cp = pltpu.make_async_remote_copy(src_ref=mine, dst_ref=mine, send_sem=send_sems.at[a], recv_sem=recv_sems.at[a],
                                              device_id=(x, y, 1 - c), device_id_type=MESH)
            cp.start()
            cps.append(cp)
        for a in range(n):
            hr = fulls[a].shape[0] // 2
            other = outs[a].at[pl.ds((1 - c) * hr, hr), :]
            pltpu.make_async_remote_copy(src_ref=other, dst_ref=other, send_sem=send_sems.at[a], recv_sem=recv_sems.at[a],
                                         device_id=(x, y, 1 - c), device_id_type=MESH).wait_recv()
        for cp in cps:
            cp.wait_send()

    return pl.pallas_call(
        body, name="join_sibling_halves", in_specs=[ANY] * n, out_specs=[ANY] * n,
        out_shape=[_sds(f.shape, f.dtype) for f in fulls], input_output_aliases={a: a for a in range(n)},
        scratch_shapes=[pltpu.SemaphoreType.DMA((n,)), pltpu.SemaphoreType.DMA((n,))],
    )(*fulls)


def pair_add(name, gslab, rsib, c_idx):
    _, R, W = gslab.shape
    hr = R // 2
    tr = _pick(hr, 512) if hr % 512 == 0 else hr
    nb = hr // tr

    def body(c_ref, a_ref, b_ref, o_ref):
        o_ref[...] = (a_ref[...].astype(F32) + b_ref[...].astype(F32)).astype(BF16)

    return pl.pallas_call(
        body, name=name,
        grid_spec=pltpu.PrefetchScalarGridSpec(
            num_scalar_prefetch=1, grid=(N_CHIPS, nb),
            in_specs=[pl.BlockSpec((None, tr, W), lambda s, i, c: (s, c[0] * nb + i, 0)),
                      pl.BlockSpec((None, tr, W), lambda s, i, c: (s, i, 0))],
            out_specs=pl.BlockSpec((None, tr, W), lambda s, i, c: (s, i, 0))),
        out_shape=_sds((N_CHIPS, hr, W), BF16), compiler_params=_cparams(("parallel", "parallel")),
    )(c_idx, gslab, rsib)


def sum_chips(name, pslab, q, place, c_idx):
    _, hr, W = pslab.shape
    tr = _pick(hr, 512) if hr % 512 == 0 else hr
    nb = hr // tr

    def body(s_ref, c_ref, p_ref, q0_ref, q1_ref, q2_ref, o_ref):
        o_ref[...] = ((p_ref[...].astype(F32) + q0_ref[...].astype(F32)) + q1_ref[...].astype(F32)) + q2_ref[...].astype(F32)

    def qspec(j):
        return pl.BlockSpec((None, tr, W), lambda i, s, c: (j, i, 0))

    return pl.pallas_call(
        body, name=name,
        grid_spec=pltpu.PrefetchScalarGridSpec(
            num_scalar_prefetch=2, grid=(nb,),
            in_specs=[pl.BlockSpec((None, tr, W), lambda i, s, c: (s[0], i, 0)), qspec(0), qspec(1), qspec(2)],
            out_specs=pl.BlockSpec((tr, W), lambda i, s, c: (c[0] * nb + i, 0))),
        out_shape=_sds((2 * hr, W), F32), compiler_params=_cparams(("parallel",)),
    )(place, c_idx, pslab, q, q, q)


def sum_devices(name, parts):
    _, R, W = parts.shape

    def body(p_ref, o_ref):
        acc = p_ref[0]
        for d in range(1, N_DEV):
            acc = acc + p_ref[d]
        o_ref[...] = acc

    tr = _pick(R, 512) if R % 512 == 0 else R
    return pl.pallas_call(
        body, name=name, grid=(R // tr,), in_specs=[pl.BlockSpec((N_DEV, tr, W), lambda i: (0, i, 0))],
        out_specs=pl.BlockSpec((tr, W), lambda i: (i, 0)), out_shape=_sds((R, W), F32),
        compiler_params=_cparams(("parallel",)),
    )(parts)


def adamw(name, g, off, w, m, v):
    R, W = w.shape
    tr = _pick(R, 256) if R % 256 == 0 else R

    def body(g_ref, w_ref, m_ref, v_ref, go_ref, d_ref, mo_ref, vo_ref):
        gv = g_ref[...]
        mn = ADAM_B1 * m_ref[...] + (1.0 - ADAM_B1) * gv
        vn = ADAM_B2 * v_ref[...] + (1.0 - ADAM_B2) * (gv * gv)
        m_hat = mn / (1.0 - ADAM_B1 ** ADAM_STEP)
        v_hat = vn / (1.0 - ADAM_B2 ** ADAM_STEP)
        go_ref[...] = gv
        d_ref[...] = -ADAM_LR * (m_hat / (jnp.sqrt(v_hat) + ADAM_EPS) + ADAM_WD * w_ref[...])
        mo_ref[...] = mn
        vo_ref[...] = vn

    blk = pl.BlockSpec((tr, W), lambda i: (i, 0))
    return pl.pallas_call(
        body, name=name, grid=(R // tr,),
        in_specs=[pl.BlockSpec((tr, W), lambda i: (off // tr + i, 0)), blk, blk, blk], out_specs=[blk] * 4,
        out_shape=[_sds((R, W), F32)] * 4, compiler_params=_cparams(("parallel",)),
    )(g, w, m, v)


def _pack(arrs):
    flat = jnp.concatenate([a.reshape(-1).astype(F32) for a in arrs])
    tile = SUBLANES * LANES * 2
    pad = (-flat.shape[0]) % tile
    return jnp.pad(flat, (0, pad)).reshape(-1, LANES)


def _unpack(packed, shapes):
    flat = packed.reshape(-1)
    out, pos = [], 0
    for s in shapes:
        n = 1
        for d in s:
            n *= d
        out.append(flat[pos:pos + n].reshape(s))
        pos += n
    return out


def kernel(x, norm_mix_g, norm_ffn_g, final_g, gm_w_in, gm_ln_g, gm_ln_b, gm_w_s, gm_b_s, gm_w_out, gla_w_in, gla_w_a1, gla_w_a2, gla_b_a, gla_norm_g, gla_w_o, ffn_w_up, ffn_w_down, loss_target, m_norm_mix_g, m_norm_ffn_g, m_final_g, m_gm_w_in, m_gm_ln_g, m_gm_ln_b, m_gm_w_s, m_gm_b_s, m_gm_w_out, m_gla_w_in, m_gla_w_a1, m_gla_w_a2, m_gla_b_a, m_gla_norm_g, m_gla_w_o, m_ffn_w_up, m_ffn_w_down, v_norm_mix_g, v_norm_ffn_g, v_final_g, v_gm_w_in, v_gm_ln_g, v_gm_ln_b, v_gm_w_s, v_gm_b_s, v_gm_w_out, v_gla_w_in, v_gla_w_a1, v_gla_w_a2, v_gla_b_a, v_gla_norm_g, v_gla_w_o, v_ffn_w_up, v_ffn_w_down):
    S, D = x.shape[1], x.shape[2]
    depth = norm_mix_g.shape[0]
    n_gm, n_gla = gm_w_in.shape[0], gla_w_in.shape[0]
    F = 4 * D
    P = GM_BLOCK
    xi, yi, ci = lax.axis_index("x"), lax.axis_index("y"), lax.axis_index("c")
    chip = 2 * xi + yi
    place = jnp.reshape(chip, (1,)).astype(jnp.int32)
    c_idx = jnp.reshape(ci, (1,)).astype(jnp.int32)

    def up_off(i): return i * D
    def down_off(i): return depth * D + i * D
    def gmo_off(j): return 2 * depth * D + j * (D // 4)
    def glo_off(j): return 2 * depth * D + n_gm * (D // 4) + j * (D // 4)
    rows_a = 2 * depth * D + (n_gm + n_gla) * (D // 4)

    ga = cast_into("cast_ffn_w_up", ffn_w_up.reshape(depth * D, D), lax.empty((N_CHIPS, rows_a, D), BF16), up_off(0), place)
    ga = cast_into("cast_ffn_w_down", ffn_w_down.reshape(depth * D, D), ga, down_off(0), place)
    ga = cast_into("cast_gm_w_out", gm_w_out.reshape(n_gm * (D // 4), D), ga, gmo_off(0), place)
    ga = cast_into("cast_gla_w_o", gla_w_o.reshape(n_gla * (D // 4), D), ga, glo_off(0), place)
    gb = cast_into("cast_gm_w_in", gm_w_in.reshape(n_gm * D, D // 2), lax.empty((N_CHIPS, n_gm * D, D // 2), BF16), 0, place)
    gc = cast_into("cast_gla_w_in", gla_w_in.reshape(n_gla * D, 3 * D // 4), lax.empty((N_CHIPS, n_gla * D, 3 * D // 4), BF16), 0, place)
    small_w = [gla_w_a1, gla_w_a2, gla_b_a, gla_norm_g]
    slab_s = _pack(small_w)
    ga, gb, gc = allgather_chips([ga, gb, gc])
    gs = allgather_devices("allgather_small_weights", slab_s)
    per_chip = [_unpack(gs[2 * s], [a.shape for a in small_w]) for s in range(N_CHIPS)]
    w_a1 = jnp.concatenate([p[0] for p in per_chip], axis=1)
    w_a2 = jnp.concatenate([p[1] for p in per_chip], axis=2)
    b_a = jnp.concatenate([p[2] for p in per_chip], axis=1)
    gnorm = jnp.concatenate([p[3] for p in per_chip], axis=1)
    w_a1p = jnp.pad(w_a1, ((0, 0), (0, 0), (0, LOW - GATE_RANK))).astype(BF16)
    w_a2p = jnp.pad(w_a2, ((0, 0), (0, LOW - GATE_RANK), (0, 0))).astype(BF16)

    chunk_id = jnp.arange(P) // CHUNK
    mask = chunk_id[None, :] <= chunk_id[:, None]
    wm_all = jnp.where(mask[None, None], gm_w_s, 0.0)
    tri = jnp.tril(jnp.ones((CHUNK, CHUNK), F32))
    triT = tri.T

    xs = x[0]
    saved = []
    for i in range(depth):
        j = i // 2
        h1 = rmsnorm_fwd(f"norm_mix_{i}", xs, norm_mix_g[i][None])
        if i % 2 == 0:
            zp = mm_fwd_col(f"gm_in_{i}", h1, gb, j * D, 2 * D)
            wm = wm_all[j].astype(BF16)
            bT = gm_b_s[j].T
            gated = gm_mid_fwd(f"gm_mid_{i}", zp, gm_ln_g[j][None], gm_ln_b[j][None], wm, bT)
            x_mid = mm_fwd_row(f"gm_out_{i}", gated, ga, gmo_off(j), D // 4, xs)
            mix = (h1, zp, gated)
        else:
            proj = mm_fwd_col(f"gla_in_{i}", h1, gc, j * D, 3 * D)
            lr = mm_plain(f"gla_low_{i}", h1, w_a1p[j], NN, BF16)
            o_raw, states = gla_scan_fwd(f"gla_scan_{i}", proj, lr, w_a2p[j], b_a[j][None], tri)
            og = gla_post_fwd(f"gla_post_{i}", o_raw, proj, gnorm[j][None])
            x_mid = mm_fwd_row(f"gla_out_{i}", og, ga, glo_off(j), D // 4, xs)
            mix = (h1, proj, lr, o_raw, states, og)
        h2 = rmsnorm_fwd(f"norm_ffn_{i}", x_mid, norm_ffn_g[i][None])
        act = mm_fwd_col(f"ffn_up_{i}", h2, ga, up_off(i), F, epilogue=lambda acc: jnp.maximum(acc, 0.0))
        x_out = mm_fwd_row(f"ffn_down_{i}", act, ga, down_off(i), D, x_mid, prologue=lambda a: a * a)
        saved.append((xs, x_mid, h2, act, mix))
        xs = x_out

    loss_part, dx, dxb, d_final_g = final_loss("final_loss", xs, final_g[None], loss_target[0])
    loss = lax.psum(loss_part[0, 0], ("x", "y", "c"))

    da = lax.empty(ga.shape, BF16)
    db = lax.empty(gb.shape, BF16)
    dc = lax.empty(gc.shape, BF16)
    d_mix_g, d_ffn_g = [None] * depth, [None] * depth
    d_ln_g, d_ln_b, d_w_s, d_b_s = [None] * n_gm, [None] * n_gm, [None] * n_gm, [None] * n_gm
    d_a1, d_a2, d_ba, d_gn = [None] * n_gla, [None] * n_gla, [None] * n_gla, [None] * n_gla
    for i in reversed(range(depth)):
        j = i // 2
        x_in, x_mid, h2, act, mix = saved[i]
        d_apre = mm_bwd_row_x(f"ffn_down_dx_{i}", dxb, ga, down_off(i), D, mul=act)
        da = mm_bwd_row_w(f"ffn_down_dw_{i}", act, dxb, da, down_off(i), D, prologue=lambda a: a * a)
        da = mm_bwd_col_w(f"ffn_up_dw_{i}", h2, d_apre, da, up_off(i))
        dh2 = mm_bwd_col_x(f"ffn_up_dx_{i}", d_apre, ga, up_off(i), D)
        dx, dxb, d_ffn_g[i] = rmsnorm_bwd(f"norm_ffn_bwd_{i}", x_mid, norm_ffn_g[i][None], dh2, dx)
        if i % 2 == 0:
            h1, zp, gated = mix
            d_gated = mm_bwd_row_x(f"gm_out_dx_{i}", dxb, ga, gmo_off(j), D // 4)
            da = mm_bwd_row_w(f"gm_out_dw_{i}", gated, dxb, da, gmo_off(j), D // 4)
            wm = wm_all[j].astype(BF16)
            wmT = jnp.swapaxes(wm_all[j], 1, 2).astype(BF16)
            dzp, d_ln_g[j], d_ln_b[j], dw, dbT = gm_mid_bwd(f"gm_mid_bwd_{i}", zp, d_gated, gm_ln_g[j][None], gm_ln_b[j][None],
                                                             wm, wmT, gm_b_s[j].T)
            d_w_s[j] = jnp.where(mask[None], dw, 0.0)
            d_b_s[j] = dbT.T
            db = mm_bwd_col_w(f"gm_in_dw_{i}", h1, dzp, db, j * D)
            dh1 = mm_bwd_col_x(f"gm_in_dx_{i}", dzp, gb, j * D, D)
        else:
            h1, proj, lr, o_raw, states, og = mix
            d_og = mm_bwd_row_x(f"gla_out_dx_{i}", dxb, ga, glo_off(j), D // 4)
            da = mm_bwd_row_w(f"gla_out_dw_{i}", og, dxb, da, glo_off(j), D // 4)
            d_oraw, d_r, d_gn[j] = gla_post_bwd(f"gla_post_bwd_{i}", d_og, o_raw, proj, gnorm[j][None])
            dq, dk, dv, dlogit, d_ba[j] = gla_scan_bwd(f"gla_scan_bwd_{i}", proj, lr, w_a2p[j], b_a[j][None], tri, triT, states, d_oraw)
            dproj = jnp.concatenate([dq, dk, dv, d_r], axis=1)
            dc = mm_bwd_col_w(f"gla_in_dw_{i}", h1, dproj, dc, j * D)
            dh1 = mm_bwd_col_x(f"gla_in_dx_{i}", dproj, gc, j * D, D)
            dlr = mm_plain(f"gla_gate_dlow_{i}", dlogit, w_a2p[j], NT, BF16)
            d_a2[j] = mm_plain(f"gla_gate_dw2_{i}", lr, dlogit, TN, F32)[:GATE_RANK]
            d_a1[j] = mm_plain(f"gla_gate_dw1_{i}", h1, dlr, TN, F32)[:, :GATE_RANK]
            dh1 = mm_plain(f"gla_gate_dx_{i}", dlr, w_a1p[j], NT, F32, add=dh1)
        dx, dxb, d_mix_g[i] = rmsnorm_bwd(f"norm_mix_bwd_{i}", x_in, norm_mix_g[i][None], dh1, dx)
    grad_x = dx[None]

    ra, rb, rc = exchange_sibling_halves([da, db, dc])
    pa = pair_add("pair_add_a", da, ra, c_idx)
    pb = pair_add("pair_add_b", db, rb, c_idx)
    pc = pair_add("pair_add_c", dc, rc, c_idx)
    qa, qb, qc = scatter_chips([pa, pb, pc])
    ha = sum_chips("sum_chips_a", pa, qa, place, c_idx)
    hb = sum_chips("sum_chips_b", pb, qb, place, c_idx)
    hc = sum_chips("sum_chips_c", pc, qc, place, c_idx)
    fa, fb, fc = join_sibling_halves([ha, hb, hc])

    small_g = [jnp.concatenate(d_mix_g), jnp.concatenate(d_ffn_g), d_final_g[0], jnp.concatenate(d_ln_g), jnp.concatenate(d_ln_b),
               jnp.stack(d_w_s), jnp.stack(d_b_s), jnp.stack(d_a1), jnp.stack(d_a2), jnp.concatenate(d_ba), jnp.concatenate(d_gn)]
    small_shapes = [(depth, D), (depth, D), (D,), (n_gm, D), (n_gm, D), (n_gm, GM_GROUPS, P, P), (n_gm, GM_GROUPS, P),
                    (n_gla, D, GATE_RANK), (n_gla, GATE_RANK, D // 2), (n_gla, D // 2), (n_gla, D)]
    parts = allgather_devices("allgather_small_grads", _pack(small_g))
    red = _unpack(sum_devices("sum_small_grads", parts), small_shapes)
    g_rep = red[:7]
    g_a1 = lax.dynamic_slice_in_dim(red[7], chip * (D // 4), D // 4, axis=1)
    g_a2 = lax.dynamic_slice_in_dim(red[8], chip * (D // 8), D // 8, axis=2)
    g_ba = lax.dynamic_slice_in_dim(red[9], chip * (D // 8), D // 8, axis=1)
    g_gn = lax.dynamic_slice_in_dim(red[10], chip * (D // 4), D // 4, axis=1)
    g_small = g_rep + [g_a1, g_a2, g_ba, g_gn]
    w_small = [norm_mix_g, norm_ffn_g, final_g, gm_ln_g, gm_ln_b, gm_w_s, gm_b_s, gla_w_a1, gla_w_a2, gla_b_a, gla_norm_g]
    m_small = [m_norm_mix_g, m_norm_ffn_g, m_final_g, m_gm_ln_g, m_gm_ln_b, m_gm_w_s, m_gm_b_s, m_gla_w_a1, m_gla_w_a2, m_gla_b_a, m_gla_norm_g]
    v_small = [v_norm_mix_g, v_norm_ffn_g, v_final_g, v_gm_ln_g, v_gm_ln_b, v_gm_w_s, v_gm_b_s, v_gla_w_a1, v_gla_w_a2, v_gla_b_a, v_gla_norm_g]
    shapes_small = [w.shape for w in w_small]
    sm = adamw("adamw_small", _pack(g_small), 0, _pack(w_small), _pack(m_small), _pack(v_small))
    sm = [_unpack(o, shapes_small) for o in sm]

    def big(name, g, off, w, m, v):
        shape = w.shape
        two_d = (shape[0] * shape[1], shape[2])
        outs = adamw(name, g, off, w.reshape(two_d), m.reshape(two_d), v.reshape(two_d))
        return [o.reshape(shape) for o in outs]

    o_gm_in = big("adamw_gm_w_in", fb, 0, gm_w_in, m_gm_w_in, v_gm_w_in)
    o_gm_out = big("adamw_gm_w_out", fa, gmo_off(0), gm_w_out, m_gm_w_out, v_gm_w_out)
    o_gla_in = big("adamw_gla_w_in", fc, 0, gla_w_in, m_gla_w_in, v_gla_w_in)
    o_gla_o = big("adamw_gla_w_o", fa, glo_off(0), gla_w_o, m_gla_w_o, v_gla_w_o)
    o_up = big("adamw_ffn_w_up", fa, up_off(0), ffn_w_up, m_ffn_w_up, v_ffn_w_up)
    o_down = big("adamw_ffn_w_down", fa, down_off(0), ffn_w_down, m_ffn_w_down, v_ffn_w_down)

    def ordered(kind):
        s = sm[kind]
        return [s[0], s[1], s[2], o_gm_in[kind], s[3], s[4], s[5], s[6], o_gm_out[kind], o_gla_in[kind],
                s[7], s[8], s[9], s[10], o_gla_o[kind], o_up[kind], o_down[kind]]

    return (loss, grad_x, *ordered(0), *ordered(1), *ordered(2), *ordered(3))
```

```python
import jax
import jax.numpy as jnp
from jax import lax
from jax.experimental import pallas as pl
from jax.experimental.pallas import tpu as pltpu

F32 = jnp.float32
BF16 = jnp.bfloat16

EPS = 1e-6
CHUNK = 64
GM_BLOCK = 128
GM_GROUPS = 8
GLA_HEADS = 4
GATE_RANK = 16
GATE_TAU = 16.0
LOW = 128
N_CHIPS = 4
N_DEV = 8

ADAM_LR = 0.001
ADAM_B1 = 0.9
ADAM_B2 = 0.999
ADAM_EPS = 1e-08
ADAM_WD = 0.01
ADAM_STEP = 10

V7X_VMEM_LIMIT = 48 * 1024 * 1024
LANES = 128
SUBLANES = 8
BF16_ROWS = 16
MESH = pl.DeviceIdType.MESH
HIGHEST = lax.Precision.HIGHEST


def _pick(n, cap):
    if n <= cap:
        return n
    best = LANES
    for t in range(LANES, cap + 1, LANES):
        if n % t == 0:
            best = t
    return best


def _rows(n, cap):
    if n <= cap:
        return n
    best = 0
    for t in range(BF16_ROWS, cap + 1, BF16_ROWS):
        if n % t == 0:
            best = t
    return best if best >= LANES else n


def _cparams(sem=None):
    return pltpu.CompilerParams(dimension_semantics=sem, vmem_limit_bytes=V7X_VMEM_LIMIT)


ANY = pl.BlockSpec(memory_space=pl.ANY)


def _sds(shape, dtype):
    return jax.ShapeDtypeStruct(shape, dtype)


def _place():
    return lax.axis_index("x"), lax.axis_index("y"), lax.axis_index("c")


def _other_chips(x, y):
    return [(1 - x, y), (x, 1 - y), (1 - x, 1 - y)]


class Comm:
    def __init__(self, ins, out_shapes, alias, n_sems, start, wait):
        self.ins, self.out_shapes, self.alias, self.n_sems, self.start, self.wait = ins, out_shapes, alias, n_sems, start, wait


def _remote(src, dst, ss, rs, k, to):
    return pltpu.make_async_remote_copy(src_ref=src, dst_ref=dst, send_sem=ss.at[k], recv_sem=rs.at[k],
                                        device_id=to, device_id_type=MESH)


def comm_gather_ici(bufs, parts):
    def region(alias, a, chip, c, r0, nr):
        hr = bufs[a].shape[1] // 2
        return alias[a].at[chip, pl.ds(c * hr + r0, nr), :]

    def start(ins, outs, alias, ss, rs):
        x, y, c = _place()
        for p, (a, r0, nr) in enumerate(parts):
            for j, (px, py) in enumerate(_other_chips(x, y)):
                mine = region(alias, a, 2 * x + y, c, r0, nr)
                _remote(mine, mine, ss, rs, 3 * p + j, (px, py, c)).start()

    def wait(ins, outs, alias, ss, rs):
        x, y, c = _place()
        for p, (a, r0, nr) in enumerate(parts):
            for j, (px, py) in enumerate(_other_chips(x, y)):
                theirs = region(alias, a, 2 * px + py, c, r0, nr)
                _remote(theirs, theirs, ss, rs, 3 * p + j, (px, py, c)).wait_recv()
                mine = region(alias, a, 2 * x + y, c, r0, nr)
                _remote(mine, mine, ss, rs, 3 * p + j, (px, py, c)).wait_send()

    return Comm([], [], list(bufs), 3 * len(parts), start, wait)


def comm_gather_forward(bufs):
    def region(alias, a, chip, hc):
        hr = bufs[a].shape[1] // 2
        return alias[a].at[chip, pl.ds(hc * hr, hr), :]

    def start(ins, outs, alias, ss, rs):
        x, y, c = _place()
        for a in range(len(bufs)):
            for j, (px, py) in enumerate(_other_chips(x, y)):
                got = region(alias, a, 2 * px + py, c)
                _remote(got, got, ss, rs, 3 * a + j, (x, y, 1 - c)).start()

    def wait(ins, outs, alias, ss, rs):
        x, y, c = _place()
        for a in range(len(bufs)):
            for j, (px, py) in enumerate(_other_chips(x, y)):
                other = region(alias, a, 2 * px + py, 1 - c)
                _remote(other, other, ss, rs, 3 * a + j, (x, y, 1 - c)).wait_recv()
                got = region(alias, a, 2 * px + py, c)
                _remote(got, got, ss, rs, 3 * a + j, (x, y, 1 - c)).wait_send()

    return Comm([], [], list(bufs), 3 * len(bufs), start, wait)


def comm_gather_all(bufs):
    ici = comm_gather_ici(bufs, [(a, 0, b.shape[1] // 2) for a, b in enumerate(bufs)])
    fwd = comm_gather_forward(bufs)
    n1 = ici.n_sems

    class Shift:
        def __init__(self, sems):
            self.sems = sems

        @property
        def at(self):
            return self

        def __getitem__(self, k):
            return self.sems.at[n1 + k]

    def start(ins, outs, alias, ss, rs):
        ici.start(ins, outs, alias, ss, rs)

    def wait(ins, outs, alias, ss, rs):
        ici.wait(ins, outs, alias, ss, rs)
        fwd.start(ins, outs, alias, Shift(ss), Shift(rs))
        fwd.wait(ins, outs, alias, Shift(ss), Shift(rs))

    return Comm([], [], list(bufs), n1 + fwd.n_sems, start, wait)


def comm_exchange(gslabs):
    def descr(ins, outs, ss, rs, a):
        x, y, c = _place()
        hr = gslabs[a].shape[1] // 2
        return _remote(ins[a].at[:, pl.ds((1 - c) * hr, hr), :], outs[a], ss, rs, a, (x, y, 1 - c))

    def start(ins, outs, alias, ss, rs):
        for a in range(len(gslabs)):
            descr(ins, outs, ss, rs, a).start()

    def wait(ins, outs, alias, ss, rs):
        for a in range(len(gslabs)):
            descr(ins, outs, ss, rs, a).wait()

    return Comm(list(gslabs), [_sds((N_CHIPS, g.shape[1] // 2, g.shape[2]), g.dtype) for g in gslabs], [], len(gslabs), start, wait)


def comm_scatter(pslabs, parts, q_prev=None):
    def descr(ins, q, ss, rs, p, j, px, py, c):
        a, r0, nr = parts[p]
        return _remote(ins[a].at[2 * px + py, pl.ds(r0, nr), :], q[a].at[j, pl.ds(r0, nr), :], ss, rs, 3 * p + j, (px, py, c))

    def each(ins, outs, alias, ss, rs, fn):
        x, y, c = _place()
        q = outs if q_prev is None else alias
        for p in range(len(parts)):
            for j, (px, py) in enumerate(_other_chips(x, y)):
                fn(descr(ins, q, ss, rs, p, j, px, py, c))

    def start(ins, outs, alias, ss, rs):
        each(ins, outs, alias, ss, rs, lambda d: d.start())

    def wait(ins, outs, alias, ss, rs):
        each(ins, outs, alias, ss, rs, lambda d: d.wait())

    fresh = [_sds((3,) + p.shape[1:], p.dtype) for p in pslabs] if q_prev is None else []
    return Comm(list(pslabs), fresh, [] if q_prev is None else list(q_prev), 3 * len(parts), start, wait)


def comm_join(fulls):
    def region(alias, a, hc):
        hr = fulls[a].shape[0] // 2
        return alias[a].at[pl.ds(hc * hr, hr), :]

    def start(ins, outs, alias, ss, rs):
        x, y, c = _place()
        for a in range(len(fulls)):
            mine = region(alias, a, c)
            _remote(mine, mine, ss, rs, a, (x, y, 1 - c)).start()

    def wait(ins, outs, alias, ss, rs):
        x, y, c = _place()
        for a in range(len(fulls)):
            other = region(alias, a, 1 - c)
            _remote(other, other, ss, rs, a, (x, y, 1 - c)).wait_recv()
            mine = region(alias, a, c)
            _remote(mine, mine, ss, rs, a, (x, y, 1 - c)).wait_send()

    return Comm([], [], list(fulls), len(fulls), start, wait)


def run_comm(name, comm):
    n_in, n_out, n_al = len(comm.ins), len(comm.out_shapes), len(comm.alias)

    def body(*refs):
        ins = refs[:n_in]
        outs = refs[n_in + n_al:n_in + n_al + n_out]
        alias = refs[n_in + n_al + n_out:n_in + n_al + n_out + n_al]
        ss, rs = refs[-2], refs[-1]
        comm.start(ins, outs, alias, ss, rs)
        comm.wait(ins, outs, alias, ss, rs)

    return pl.pallas_call(
        body, name=name, in_specs=[ANY] * (n_in + n_al), out_specs=[ANY] * (n_out + n_al),
        out_shape=list(comm.out_shapes) + [_sds(b.shape, b.dtype) for b in comm.alias],
        input_output_aliases={n_in + t: n_out + t for t in range(n_al)},
        scratch_shapes=[pltpu.SemaphoreType.DMA((comm.n_sems,)), pltpu.SemaphoreType.DMA((comm.n_sems,))],
    )(*comm.ins, *comm.alias)


def allgather_devices(name, v):
    def body(v_ref, out_ref, send_sems, recv_sems, local_sem):
        x, y, c = _place()
        me = 4 * x + 2 * y + c
        mine = pltpu.make_async_copy(v_ref, out_ref.at[me], local_sem)
        mine.start()

        def peer(k):
            return (1 - x if k & 4 else x, 1 - y if k & 2 else y, 1 - c if k & 1 else c)

        def copy(k, src, dst):
            return pltpu.make_async_remote_copy(src_ref=src, dst_ref=dst, send_sem=send_sems.at[k - 1],
                                                recv_sem=recv_sems.at[k - 1], device_id=peer(k), device_id_type=MESH)

        sends = [copy(k, v_ref, out_ref.at[me]) for k in range(1, N_DEV)]
        for cp in sends:
            cp.start()
        for k in range(1, N_DEV):
            px, py, pc = peer(k)
            them = 4 * px + 2 * py + pc
            copy(k, out_ref.at[them], out_ref.at[them]).wait_recv()
        for cp in sends:
            cp.wait_send()
        mine.wait()

    return pl.pallas_call(
        body, name=name, in_specs=[ANY], out_specs=ANY, out_shape=_sds((N_DEV,) + v.shape, v.dtype),
        scratch_shapes=[pltpu.SemaphoreType.DMA((N_DEV - 1,)), pltpu.SemaphoreType.DMA((N_DEV - 1,)), pltpu.SemaphoreType.DMA],
    )(v)


NN = ((1,), (0,))
NT = ((1,), (1,))
TN = ((0,), (0,))


def _mm(name, a, b, *, dims, grid, a_spec, b_spec, out_spec, out_shape, acc_shape, extra=(), extra_specs=(),
        prologue=None, epilogue=None, slab=None, b_reshape=None, comm=None):
    gi, gj, nk = grid
    n_extra = len(extra)
    n_slab = 1 if slab is not None else 0
    n_in, n_out, n_al = (len(comm.ins), len(comm.out_shapes), len(comm.alias)) if comm is not None else (0, 0, 0)

    def body(*refs):
        a_ref, b_ref = refs[0], refs[1]
        ex = refs[2:2 + n_extra]
        pos = 2 + n_extra + n_slab
        c_ins = refs[pos:pos + n_in]
        pos += n_in + n_al
        o_ref = refs[pos]
        c_outs = refs[pos + 1:pos + 1 + n_out]
        c_alias = refs[pos + 1 + n_out:pos + 1 + n_out + n_al]
        scratch = refs[pos + 1 + n_out + n_al:]
        i, j, k = pl.program_id(0), pl.program_id(1), pl.program_id(2)
        if comm is not None:
            ss, rs = scratch[-2], scratch[-1]

            @pl.when(jnp.logical_and(jnp.logical_and(i == 0, j == 0), k == 0))
            def _():
                comm.start(c_ins, c_outs, c_alias, ss, rs)

        av = a_ref[...]
        if prologue is not None:
            av = prologue(av)
        bv = b_ref[...]
        if b_reshape is not None:
            bv = bv.reshape(b_reshape)
        p = lax.dot_general(av, bv, (dims, ((), ())), preferred_element_type=F32)

        def finish(acc):
            r = acc if epilogue is None else epilogue(acc, *ex)
            o_ref[...] = r.astype(o_ref.dtype)

        if nk == 1:
            finish(p)
        else:
            acc_ref = scratch[0]

            @pl.when(k == 0)
            def _():
                acc_ref[...] = p

            @pl.when(jnp.logical_and(k > 0, k < nk - 1))
            def _():
                acc_ref[...] += p

            @pl.when(k == nk - 1)
            def _():
                finish(acc_ref[...] + p)

        if comm is not None:
            @pl.when(jnp.logical_and(jnp.logical_and(i == gi - 1, j == gj - 1), k == nk - 1))
            def _():
                comm.wait(c_ins, c_outs, c_alias, ss, rs)

    inputs = [a, b, *extra]
    in_specs = [a_spec, b_spec, *extra_specs]
    out_shapes, out_specs, aliases = [out_shape], [out_spec], {}
    if slab is not None:
        aliases[len(inputs)] = 0
        inputs.append(slab)
        in_specs.append(ANY)
    scratch_shapes = [pltpu.VMEM(acc_shape, F32)] if nk > 1 else []
    sem = ("parallel", "parallel", "arbitrary")
    if comm is not None:
        inputs += list(comm.ins)
        for t, buf in enumerate(comm.alias):
            aliases[len(inputs)] = 1 + n_out + t
            inputs.append(buf)
        in_specs += [ANY] * (n_in + n_al)
        out_shapes += list(comm.out_shapes) + [_sds(buf.shape, buf.dtype) for buf in comm.alias]
        out_specs += [ANY] * (n_out + n_al)
        scratch_shapes += [pltpu.SemaphoreType.DMA((comm.n_sems,)), pltpu.SemaphoreType.DMA((comm.n_sems,))]
        sem = ("arbitrary", "arbitrary", "arbitrary")
    res = pl.pallas_call(
        body, name=name, grid=grid, in_specs=in_specs, out_specs=out_specs if comm is not None else out_spec,
        out_shape=out_shapes if comm is not None else out_shape,
        scratch_shapes=scratch_shapes, input_output_aliases=aliases, compiler_params=_cparams(sem),
    )(*inputs)
    return res


def mm_fwd_col(name, h, slab, off, n_out, epilogue=None, out_dtype=BF16, comm=None):
    S, D = h.shape
    W = slab.shape[2]
    tm, tn = min(1024, S), _pick(W, 1024)
    wps = W // tn
    return _mm(name, h, slab, dims=NN, grid=(S // tm, n_out // tn, 1),
               a_spec=pl.BlockSpec((tm, D), lambda i, j, k: (i, 0)),
               b_spec=pl.BlockSpec((None, D, tn), lambda i, j, k: (j // wps, off // D, j % wps)),
               out_spec=pl.BlockSpec((tm, tn), lambda i, j, k: (i, j)),
               out_shape=_sds((S, n_out), out_dtype), acc_shape=(tm, tn), epilogue=epilogue, comm=comm)


def mm_fwd_row(name, a, slab, off, ksh, res, prologue=None, comm=None):
    S, K = a.shape
    D = slab.shape[2]
    tm, tn, tk = min(1024, S), min(1024, D), min(2048, ksh)
    res_spec = pl.BlockSpec((tm, tn), lambda i, j, k: (i, j))
    if K <= 2048 and ksh < K:
        return _mm(name, a, slab, dims=NN, grid=(S // tm, D // tn, 1),
                   a_spec=pl.BlockSpec((tm, K), lambda i, j, k: (i, 0)),
                   b_spec=pl.BlockSpec((N_CHIPS, ksh, tn), lambda i, j, k: (0, off // ksh, j)),
                   out_spec=res_spec, out_shape=_sds((S, D), F32), acc_shape=(tm, tn),
                   extra=(res,), extra_specs=(res_spec,), prologue=prologue, epilogue=lambda acc, r: acc + r[...],
                   b_reshape=(K, tn), comm=comm)
    kps = ksh // tk
    return _mm(name, a, slab, dims=NN, grid=(S // tm, D // tn, K // tk),
               a_spec=pl.BlockSpec((tm, tk), lambda i, j, k: (i, k)),
               b_spec=pl.BlockSpec((None, tk, tn), lambda i, j, k: (k // kps, off // tk + k % kps, j)),
               out_spec=res_spec, out_shape=_sds((S, D), F32), acc_shape=(tm, tn),
               extra=(res,), extra_specs=(res_spec,), prologue=prologue, epilogue=lambda acc, r: acc + r[...], comm=comm)


def mm_bwd_col_x(name, dz, slab, off, D, add=None, comm=None):
    S, N = dz.shape
    W = slab.shape[2]
    tm, to, tr = min(1024, S), min(1024, D), _pick(W, 2048)
    wps = W // tr
    extra, extra_specs, epi = (), (), None
    if add is not None:
        extra, extra_specs = (add,), (pl.BlockSpec((tm, to), lambda i, j, k: (i, j)),)
        epi = lambda acc, r: acc + r[...]
    return _mm(name, dz, slab, dims=NT, grid=(S // tm, D // to, N // tr),
               a_spec=pl.BlockSpec((tm, tr), lambda i, j, k: (i, k)),
               b_spec=pl.BlockSpec((None, to, tr), lambda i, j, k: (k // wps, off // to + j, k % wps)),
               out_spec=pl.BlockSpec((tm, to), lambda i, j, k: (i, j)),
               out_shape=_sds((S, D), F32), acc_shape=(tm, to), extra=extra, extra_specs=extra_specs, epilogue=epi, comm=comm)


def mm_bwd_row_x(name, dxb, slab, off, ksh, mul=None, comm=None):
    S, D = dxb.shape
    tm, tn = min(1024, S), min(1024, ksh)
    kps = ksh // tn
    extra, extra_specs, epi = (), (), None
    if mul is not None:
        extra, extra_specs = (mul,), (pl.BlockSpec((tm, tn), lambda i, j, k: (i, j)),)
        epi = lambda acc, r: acc * (2.0 * r[...].astype(F32))
    return _mm(name, dxb, slab, dims=NT, grid=(S // tm, 4 * kps, 1),
               a_spec=pl.BlockSpec((tm, D), lambda i, j, k: (i, 0)),
               b_spec=pl.BlockSpec((None, tn, D), lambda i, j, k: (j // kps, off // tn + j % kps, 0)),
               out_spec=pl.BlockSpec((tm, tn), lambda i, j, k: (i, j)),
               out_shape=_sds((S, 4 * ksh), BF16), acc_shape=(tm, tn), extra=extra, extra_specs=extra_specs, epilogue=epi, comm=comm)


def mm_bwd_col_w(name, h, dz, gslab, off, comm=None):
    S, D = h.shape
    N = dz.shape[1]
    W = gslab.shape[2]
    tk, tn, ts = min(1024, D), _pick(W, 1024), min(2048, S)
    wps = W // tn
    return _mm(name, h, dz, dims=TN, grid=(D // tk, N // tn, S // ts),
               a_spec=pl.BlockSpec((ts, tk), lambda i, j, k: (k, i)),
               b_spec=pl.BlockSpec((ts, tn), lambda i, j, k: (k, j)),
               out_spec=pl.BlockSpec((None, tk, tn), lambda i, j, k: (j // wps, off // tk + i, j % wps)),
               out_shape=_sds(gslab.shape, gslab.dtype), acc_shape=(tk, tn), slab=gslab, comm=comm)


def mm_bwd_row_w(name, a, dxb, gslab, off, ksh, prologue=None, comm=None):
    S, K = a.shape
    D = dxb.shape[1]
    tk, tn, ts = min(1024, ksh), min(1024, D), min(2048, S)
    kps = ksh // tk
    return _mm(name, a, dxb, dims=TN, grid=(K // tk, D // tn, S // ts),
               a_spec=pl.BlockSpec((ts, tk), lambda i, j, k: (k, i)),
               b_spec=pl.BlockSpec((ts, tn), lambda i, j, k: (k, j)),
               out_spec=pl.BlockSpec((None, tk, tn), lambda i, j, k: (i // kps, off // tk + i % kps, j)),
               out_shape=_sds(gslab.shape, gslab.dtype), acc_shape=(tk, tn), slab=gslab, prologue=prologue, comm=comm)


def mm_plain(name, a, b, dims, out_dtype, add=None):
    if dims == NN:
        M, N = a.shape[0], b.shape[1]
    elif dims == NT:
        M, N = a.shape[0], b.shape[0]
    else:
        M, N = a.shape[1], b.shape[1]
    red = a.shape[0] if dims == TN else a.shape[1]
    tm, tn = min(1024, M), min(1024, N)
    tr = min(1024, red) if dims == TN else red
    nk = red // tr
    if dims == TN:
        a_spec = pl.BlockSpec((tr, tm), lambda i, j, k: (k, i))
        b_spec = pl.BlockSpec((tr, tn), lambda i, j, k: (k, j))
    elif dims == NN:
        a_spec = pl.BlockSpec((tm, tr), lambda i, j, k: (i, k))
        b_spec = pl.BlockSpec((tr, tn), lambda i, j, k: (k, j))
    else:
        a_spec = pl.BlockSpec((tm, tr), lambda i, j, k: (i, k))
        b_spec = pl.BlockSpec((tn, tr), lambda i, j, k: (j, k))
    extra, extra_specs, epi = (), (), None
    if add is not None:
        extra, extra_specs = (add,), (pl.BlockSpec((tm, tn), lambda i, j, k: (i, j)),)
        epi = lambda acc, r: acc + r[...]
    return _mm(name, a, b, dims=dims, grid=(M // tm, N // tn, nk), a_spec=a_spec, b_spec=b_spec,
               out_spec=pl.BlockSpec((tm, tn), lambda i, j, k: (i, j)),
               out_shape=_sds((M, N), out_dtype), acc_shape=(tm, tn), extra=extra, extra_specs=extra_specs, epilogue=epi)


ROWS = 256


def rmsnorm_fwd(name, x, g):
    S, D = x.shape
    tr = min(ROWS, S)

    def body(x_ref, g_ref, h_ref):
        xv = x_ref[...]
        rstd = lax.rsqrt(jnp.mean(xv * xv, axis=-1, keepdims=True) + EPS)
        h_ref[...] = (xv * rstd * g_ref[...]).astype(BF16)

    return pl.pallas_call(
        body, name=name, grid=(S // tr,),
        in_specs=[pl.BlockSpec((tr, D), lambda i: (i, 0)), pl.BlockSpec((1, D), lambda i: (0, 0))],
        out_specs=pl.BlockSpec((tr, D), lambda i: (i, 0)), out_shape=_sds((S, D), BF16),
        compiler_params=_cparams(("parallel",)),
    )(x, g)


def rmsnorm_bwd(name, x, g, dh, dres):
    S, D = x.shape
    tr = min(ROWS, S)

    def body(x_ref, g_ref, dh_ref, dres_ref, dx_ref, dxb_ref, dg_ref):
        xv = x_ref[...]
        rstd = lax.rsqrt(jnp.mean(xv * xv, axis=-1, keepdims=True) + EPS)
        xh = xv * rstd
        dy = dh_ref[...]
        dxh = dy * g_ref[...]
        dx = dres_ref[...] + rstd * (dxh - xh * jnp.mean(dxh * xh, axis=-1, keepdims=True))
        dx_ref[...] = dx
        dxb_ref[...] = dx.astype(BF16)

        @pl.when(pl.program_id(0) == 0)
        def _():
            dg_ref[...] = jnp.zeros_like(dg_ref)

        dg_ref[...] += jnp.sum(dy * xh, axis=0, keepdims=True)

    row = pl.BlockSpec((tr, D), lambda i: (i, 0))
    vec = pl.BlockSpec((1, D), lambda i: (0, 0))
    return pl.pallas_call(
        body, name=name, grid=(S // tr,), in_specs=[row, vec, row, row], out_specs=[row, row, vec],
        out_shape=[_sds((S, D), F32), _sds((S, D), BF16), _sds((1, D), F32)],
        compiler_params=_cparams(("arbitrary",)),
    )(x, g, dh, dres)


def final_loss(name, x, g, target):
    S, D = x.shape
    tr = min(ROWS, S)

    def body(x_ref, g_ref, t_ref, loss_ref, dx_ref, dxb_ref, dg_ref):
        xv = x_ref[...]
        gv = g_ref[...]
        rstd = lax.rsqrt(jnp.mean(xv * xv, axis=-1, keepdims=True) + EPS)
        xh = xv * rstd
        err = xh * gv - t_ref[...]
        dy = err * (1.0 / D)
        dxh = dy * gv
        dx = rstd * (dxh - xh * jnp.mean(dxh * xh, axis=-1, keepdims=True))
        dx_ref[...] = dx
        dxb_ref[...] = dx.astype(BF16)

        @pl.when(pl.program_id(0) == 0)
        def _():
            dg_ref[...] = jnp.zeros_like(dg_ref)
            loss_ref[...] = jnp.zeros_like(loss_ref)

        dg_ref[...] += jnp.sum(dy * xh, axis=0, keepdims=True)
        loss_ref[...] += 0.5 * jnp.sum(jnp.mean(err * err, axis=-1, keepdims=True))

    row = pl.BlockSpec((tr, D), lambda i: (i, 0))
    vec = pl.BlockSpec((1, D), lambda i: (0, 0))
    return pl.pallas_call(
        body, name=name, grid=(S // tr,), in_specs=[row, vec, row],
        out_specs=[pl.BlockSpec((SUBLANES, LANES), lambda i: (0, 0)), row, row, vec],
        out_shape=[_sds((SUBLANES, LANES), F32), _sds((S, D), F32), _sds((S, D), BF16), _sds((1, D), F32)],
        compiler_params=_cparams(("arbitrary",)),
    )(x, g, target)


def _gelu(x):
    return 0.5 * x * (1.0 + lax.erf(x * 0.7071067811865476))


def _gelu_grad(x):
    return 0.5 * (1.0 + lax.erf(x * 0.7071067811865476)) + x * jnp.exp(-0.5 * x * x) * 0.3989422804014327


def _gm_common(zp, lng, lnb, D):
    z = _gelu(zp)
    u, v = z[:, :D], z[:, D:]
    xc = v - jnp.mean(v, axis=-1, keepdims=True)
    rstd = lax.rsqrt(jnp.mean(xc * xc, axis=-1, keepdims=True) + EPS)
    xh = xc * rstd
    return u, xh, rstd, xh * lng + lnb


def gm_mid_fwd(name, zp, lng, lnb, wm, bT):
    S, D2 = zp.shape
    D = D2 // 2
    dg = D // GM_GROUPS
    P = GM_BLOCK

    def body(z_ref, lng_ref, lnb_ref, wm_ref, bT_ref, o_ref):
        u, _, _, vn = _gm_common(z_ref[...].astype(F32), lng_ref[...], lnb_ref[...], D)
        vnb = vn.astype(BF16)
        for gi in range(GM_GROUPS):
            cols = slice(gi * dg, (gi + 1) * dg)
            mixed = jnp.dot(wm_ref[gi], vnb[:, cols], preferred_element_type=F32) + bT_ref[:, gi:gi + 1]
            o_ref[:, cols] = (u[:, cols] * mixed).astype(BF16)

    vec = pl.BlockSpec((1, D), lambda i: (0, 0))
    return pl.pallas_call(
        body, name=name, grid=(S // P,),
        in_specs=[pl.BlockSpec((P, D2), lambda i: (i, 0)), vec, vec,
                  pl.BlockSpec((GM_GROUPS, P, P), lambda i: (0, 0, 0)), pl.BlockSpec((P, GM_GROUPS), lambda i: (0, 0))],
        out_specs=pl.BlockSpec((P, D), lambda i: (i, 0)), out_shape=_sds((S, D), BF16),
        compiler_params=_cparams(("parallel",)),
    )(zp, lng, lnb, wm, bT)


def gm_mid_bwd(name, zp, dgated, lng, lnb, wm, wmT, bT):
    S, D2 = zp.shape
    D = D2 // 2
    dg = D // GM_GROUPS
    P = GM_BLOCK

    def body(z_ref, dgt_ref, lng_ref, lnb_ref, wm_ref, wmT_ref, bT_ref, dz_ref, dlng_ref, dlnb_ref, dw_ref, dbT_ref, dvn_ref):
        @pl.when(pl.program_id(0) == 0)
        def _():
            dlng_ref[...] = jnp.zeros_like(dlng_ref)
            dlnb_ref[...] = jnp.zeros_like(dlnb_ref)
            dw_ref[...] = jnp.zeros_like(dw_ref)
            dbT_ref[...] = jnp.zeros_like(dbT_ref)

        zp_v = z_ref[...].astype(F32)
        lng_v = lng_ref[...]
        u, xh, rstd, vn = _gm_common(zp_v, lng_v, lnb_ref[...], D)
        vnb = vn.astype(BF16)
        dgt = dgt_ref[...].astype(F32)
        for gi in range(GM_GROUPS):
            cols = slice(gi * dg, (gi + 1) * dg)
            mixed = jnp.dot(wm_ref[gi], vnb[:, cols], preferred_element_type=F32) + bT_ref[:, gi:gi + 1]
            dm = dgt[:, cols] * u[:, cols]
            dmb = dm.astype(BF16)
            dz_ref[:, cols] = (dgt[:, cols] * mixed * _gelu_grad(zp_v[:, cols])).astype(BF16)
            dbT_ref[:, gi:gi + 1] += jnp.sum(dm, axis=1, keepdims=True)
            dw_ref[gi] += lax.dot_general(dmb, vnb[:, cols], (NT, ((), ())), preferred_element_type=F32)
            dvn_ref[:, cols] = jnp.dot(wmT_ref[gi], dmb, preferred_element_type=F32)
        dvn = dvn_ref[...]
        dlng_ref[...] += jnp.sum(dvn * xh, axis=0, keepdims=True)
        dlnb_ref[...] += jnp.sum(dvn, axis=0, keepdims=True)
        dyg = dvn * lng_v
        dv = rstd * (dyg - jnp.mean(dyg, axis=-1, keepdims=True) - xh * jnp.mean(dyg * xh, axis=-1, keepdims=True))
        dz_ref[:, D:] = (dv * _gelu_grad(zp_v[:, D:])).astype(BF16)

    vec = pl.BlockSpec((1, D), lambda i: (0, 0))
    wsp = pl.BlockSpec((GM_GROUPS, P, P), lambda i: (0, 0, 0))
    bsp = pl.BlockSpec((P, GM_GROUPS), lambda i: (0, 0))
    return pl.pallas_call(
        body, name=name, grid=(S // P,),
        in_specs=[pl.BlockSpec((P, D2), lambda i: (i, 0)), pl.BlockSpec((P, D), lambda i: (i, 0)), vec, vec, wsp, wsp, bsp],
        out_specs=[pl.BlockSpec((P, D2), lambda i: (i, 0)), vec, vec, wsp, bsp],
        out_shape=[_sds((S, D2), BF16), _sds((1, D), F32), _sds((1, D), F32), _sds((GM_GROUPS, P, P), F32), _sds((P, GM_GROUPS), F32)],
        scratch_shapes=[pltpu.VMEM((P, D), F32)],
        compiler_params=_cparams(("arbitrary",)),
    )(zp, dgated, lng, lnb, wm, wmT, bT)


def _gla_gate(lr, w2, ba, tri):
    logit = jnp.dot(lr, w2, preferred_element_type=F32) + ba
    la = (jnp.minimum(logit, 0.0) - jnp.log1p(jnp.exp(-jnp.abs(logit)))) * (1.0 / GATE_TAU)
    g = jnp.dot(tri, la, preferred_element_type=F32, precision=HIGHEST)
    return logit, g


def gla_scan_fwd(name, proj, lr, w2p, ba, tri):
    S, D3 = proj.shape
    D = D3 // 3
    H, C = GLA_HEADS, CHUNK
    dk, dv = D // 2 // H, D // H
    NC = S // C
    scale = dk ** -0.5

    def body(q_ref, k_ref, v_ref, lr_ref, w2_ref, ba_ref, tri_ref, o_ref, st_ref, state):
        @pl.when(pl.program_id(0) == 0)
        def _():
            state[...] = jnp.zeros_like(state)

        _, g = _gla_gate(lr_ref[...], w2_ref[...], ba_ref[...], tri_ref[...])
        gend = g[C - 1:C, :]
        kdec = (k_ref[...].astype(F32) * jnp.exp(gend - g)).astype(BF16)
        dec = jnp.exp(gend)
        qs = (q_ref[...].astype(F32) * scale).astype(BF16)
        for hh in range(H):
            kc, vc = slice(hh * dk, (hh + 1) * dk), slice(hh * dv, (hh + 1) * dv)
            kv = lax.dot_general(v_ref[:, vc], kdec[:, kc], (TN, ((), ())), preferred_element_type=F32)
            new = dec[:, kc] * state[hh] + kv
            state[hh] = new
            nb = new.astype(BF16)
            st_ref[hh] = nb
            o_ref[:, vc] = lax.dot_general(qs[:, kc], nb, (NT, ((), ())), preferred_element_type=F32).astype(BF16)

    return pl.pallas_call(
        body, name=name, grid=(NC,),
        in_specs=[pl.BlockSpec((C, D // 2), lambda t: (t, 0)), pl.BlockSpec((C, D // 2), lambda t: (t, 1)),
                  pl.BlockSpec((C, D), lambda t: (t, 1)), pl.BlockSpec((C, LOW), lambda t: (t, 0)),
                  pl.BlockSpec((LOW, D // 2), lambda t: (0, 0)), pl.BlockSpec((1, D // 2), lambda t: (0, 0)),
                  pl.BlockSpec((C, C), lambda t: (0, 0))],
        out_specs=[pl.BlockSpec((C, D), lambda t: (t, 0)), pl.BlockSpec((None, H, dv, dk), lambda t: (t, 0, 0, 0))],
        out_shape=[_sds((S, D), BF16), _sds((NC, H, dv, dk), BF16)],
        scratch_shapes=[pltpu.VMEM((H, dv, dk), F32)],
        compiler_params=_cparams(("arbitrary",)),
    )(proj, proj, proj, lr, w2p, ba, tri)


def gla_scan_bwd(name, proj, lr, w2p, ba, tri, triT, states, do):
    S, D3 = proj.shape
    D = D3 // 3
    H, C = GLA_HEADS, CHUNK
    dk, dv = D // 2 // H, D // H
    NC = S // C
    scale = dk ** -0.5

    def body(q_ref, k_ref, v_ref, lr_ref, w2_ref, ba_ref, tri_ref, triT_ref, st_ref, sp_ref, do_ref,
             dq_ref, dk_ref, dv_ref, dl_ref, dba_ref, dstate, dkd_ref, ddec_ref):
        t = pl.program_id(0)

        @pl.when(t == 0)
        def _():
            dstate[...] = jnp.zeros_like(dstate)
            dba_ref[...] = jnp.zeros_like(dba_ref)

        logit, g = _gla_gate(lr_ref[...], w2_ref[...], ba_ref[...], tri_ref[...])
        gend = g[C - 1:C, :]
        e = jnp.exp(gend - g)
        kf = k_ref[...].astype(F32)
        kdec = (kf * e).astype(BF16)
        dec = jnp.exp(gend)
        qs = (q_ref[...].astype(F32) * scale).astype(BF16)
        has_prev = (t < NC - 1).astype(F32)
        for hh in range(H):
            kc, vc = slice(hh * dk, (hh + 1) * dk), slice(hh * dv, (hh + 1) * dv)
            dob = do_ref[:, vc]
            dq_ref[:, kc] = (jnp.dot(dob, st_ref[hh], preferred_element_type=F32) * scale).astype(BF16)
            ds = dstate[hh] + lax.dot_general(dob, qs[:, kc], (TN, ((), ())), preferred_element_type=F32)
            dsb = ds.astype(BF16)
            dkd_ref[:, kc] = jnp.dot(v_ref[:, vc], dsb, preferred_element_type=F32)
            dv_ref[:, vc] = lax.dot_general(kdec[:, kc], dsb, (NT, ((), ())), preferred_element_type=F32).astype(BF16)
            ddec_ref[:, kc] = jnp.sum(ds * sp_ref[hh].astype(F32), axis=0, keepdims=True) * has_prev
            dstate[hh] = dec[:, kc] * ds
        dkdec = dkd_ref[...]
        dk_ref[...] = (dkdec * e).astype(BF16)
        dd = dkdec * kf * e
        dgend = jnp.sum(dd, axis=0, keepdims=True) + ddec_ref[...] * dec
        last = lax.broadcasted_iota(jnp.int32, (C, 1), 0) == C - 1
        dg = jnp.where(last, dgend, 0.0) - dd
        dla = jnp.dot(triT_ref[...], dg, preferred_element_type=F32, precision=HIGHEST)
        dlogit = dla * (1.0 / GATE_TAU) * (1.0 - jax.nn.sigmoid(logit))
        dl_ref[...] = dlogit.astype(BF16)
        dba_ref[...] += jnp.sum(dlogit, axis=0, keepdims=True)

    rev = lambda t: NC - 1 - t
    half = pl.BlockSpec((C, D // 2), lambda t: (rev(t), 0))
    full = pl.BlockSpec((C, D), lambda t: (rev(t), 0))
    return pl.pallas_call(
        body, name=name, grid=(NC,),
        in_specs=[half, pl.BlockSpec((C, D // 2), lambda t: (rev(t), 1)), pl.BlockSpec((C, D), lambda t: (rev(t), 1)),
                  pl.BlockSpec((C, LOW), lambda t: (rev(t), 0)),
                  pl.BlockSpec((LOW, D // 2), lambda t: (0, 0)), pl.BlockSpec((1, D // 2), lambda t: (0, 0)),
                  pl.BlockSpec((C, C), lambda t: (0, 0)), pl.BlockSpec((C, C), lambda t: (0, 0)),
                  pl.BlockSpec((None, H, dv, dk), lambda t: (rev(t), 0, 0, 0)),
                  pl.BlockSpec((None, H, dv, dk), lambda t: (jnp.maximum(rev(t) - 1, 0), 0, 0, 0)),
                  full],
        out_specs=[half, half, full, half, pl.BlockSpec((1, D // 2), lambda t: (0, 0))],
        out_shape=[_sds((S, D // 2), BF16), _sds((S, D // 2), BF16), _sds((S, D), BF16), _sds((S, D // 2), BF16),
                   _sds((1, D // 2), F32)],
        scratch_shapes=[pltpu.VMEM((H, dv, dk), F32), pltpu.VMEM((C, D // 2), F32), pltpu.VMEM((1, D // 2), F32)],
        compiler_params=_cparams(("arbitrary",)),
    )(proj, proj, proj, lr, w2p, ba, tri, triT, states, states, do)


def _gla_post_common(o, r, ng):
    rs = lax.rsqrt(jnp.mean(o * o, axis=-1, keepdims=True) + EPS)
    oh = o * rs
    sig = jax.nn.sigmoid(r)
    return rs, oh, oh * ng, sig, r * sig


def gla_post_fwd(name, o_raw, proj, ng):
    S, D = o_raw.shape
    dv = D // GLA_HEADS
    tr = min(ROWS, S)

    def body(o_ref, r_ref, ng_ref, og_ref):
        for hh in range(GLA_HEADS):
            cols = slice(hh * dv, (hh + 1) * dv)
            _, _, on, _, sil = _gla_post_common(o_ref[:, cols].astype(F32), r_ref[:, cols].astype(F32), ng_ref[:, cols])
            og_ref[:, cols] = (on * sil).astype(BF16)

    row = pl.BlockSpec((tr, D), lambda i: (i, 0))
    return pl.pallas_call(
        body, name=name, grid=(S // tr,),
        in_specs=[row, pl.BlockSpec((tr, D), lambda i: (i, 2)), pl.BlockSpec((1, D), lambda i: (0, 0))],
        out_specs=row, out_shape=_sds((S, D), BF16), compiler_params=_cparams(("parallel",)),
    )(o_raw, proj, ng)


def gla_post_bwd(name, dog, o_raw, proj, ng):
    S, D = o_raw.shape
    dv = D // GLA_HEADS
    tr = min(ROWS, S)

    def body(dog_ref, o_ref, r_ref, ng_ref, do_ref, dr_ref, dng_ref):
        @pl.when(pl.program_id(0) == 0)
        def _():
            dng_ref[...] = jnp.zeros_like(dng_ref)

        for hh in range(GLA_HEADS):
            cols = slice(hh * dv, (hh + 1) * dv)
            r = r_ref[:, cols].astype(F32)
            ngv = ng_ref[:, cols]
            rs, oh, on, sig, sil = _gla_post_common(o_ref[:, cols].astype(F32), r, ngv)
            dogv = dog_ref[:, cols].astype(F32)
            don = dogv * sil
            dr_ref[:, cols] = (dogv * on * (sig * (1.0 + r * (1.0 - sig)))).astype(BF16)
            dng_ref[:, cols] += jnp.sum(don * oh, axis=0, keepdims=True)
            doh = don * ngv
            do_ref[:, cols] = (rs * (doh - oh * jnp.mean(doh * oh, axis=-1, keepdims=True))).astype(BF16)

    row = pl.BlockSpec((tr, D), lambda i: (i, 0))
    vec = pl.BlockSpec((1, D), lambda i: (0, 0))
    return pl.pallas_call(
        body, name=name, grid=(S // tr,),
        in_specs=[row, row, pl.BlockSpec((tr, D), lambda i: (i, 2)), vec],
        out_specs=[row, row, vec], out_shape=[_sds((S, D), BF16), _sds((S, D), BF16), _sds((1, D), F32)],
        compiler_params=_cparams(("arbitrary",)),
    )(dog, o_raw, proj, ng)


def cast_into(name, w, w_off, rows, buf, off, chip_idx):
    W = w.shape[1]
    tr = _rows(rows, 512)

    def body(p_ref, w_ref, buf_ref, o_ref):
        o_ref[...] = w_ref[...].astype(BF16)

    return pl.pallas_call(
        body, name=name,
        grid_spec=pltpu.PrefetchScalarGridSpec(
            num_scalar_prefetch=1, grid=(rows // tr,),
            in_specs=[pl.BlockSpec((tr, W), lambda i, p: (w_off // tr + i, 0)), ANY],
            out_specs=pl.BlockSpec((None, tr, W), lambda i, p: (p[0], off // tr + i, 0))),
        out_shape=_sds(buf.shape, buf.dtype), input_output_aliases={2: 0},
        compiler_params=_cparams(("parallel",)),
    )(chip_idx, w, buf)


def pair_add(name, gslab, rsib, c_idx):
    _, R, W = gslab.shape
    hr = R // 2
    tr = _rows(hr, 512)
    nb = hr // tr

    def body(c_ref, a_ref, b_ref, o_ref):
        o_ref[...] = (a_ref[...].astype(F32) + b_ref[...].astype(F32)).astype(BF16)

    return pl.pallas_call(
        body, name=name,
        grid_spec=pltpu.PrefetchScalarGridSpec(
            num_scalar_prefetch=1, grid=(N_CHIPS, nb),
            in_specs=[pl.BlockSpec((None, tr, W), lambda s, i, c: (s, c[0] * nb + i, 0)),
                      pl.BlockSpec((None, tr, W), lambda s, i, c: (s, i, 0))],
            out_specs=pl.BlockSpec((None, tr, W), lambda s, i, c: (s, i, 0))),
        out_shape=_sds((N_CHIPS, hr, W), BF16), compiler_params=_cparams(("parallel", "parallel")),
    )(c_idx, gslab, rsib)


def sum_chips(name, pslab, q, chip_idx, c_idx):
    _, hr, W = pslab.shape
    tr = _rows(hr, 512)
    nb = hr // tr

    def body(s_ref, c_ref, p_ref, q0_ref, q1_ref, q2_ref, o_ref):
        o_ref[...] = ((p_ref[...].astype(F32) + q0_ref[...].astype(F32)) + q1_ref[...].astype(F32)) + q2_ref[...].astype(F32)

    def qspec(j):
        return pl.BlockSpec((None, tr, W), lambda i, s, c: (j, i, 0))

    return pl.pallas_call(
        body, name=name,
        grid_spec=pltpu.PrefetchScalarGridSpec(
            num_scalar_prefetch=2, grid=(nb,),
            in_specs=[pl.BlockSpec((None, tr, W), lambda i, s, c: (s[0], i, 0)), qspec(0), qspec(1), qspec(2)],
            out_specs=pl.BlockSpec((tr, W), lambda i, s, c: (c[0] * nb + i, 0))),
        out_shape=_sds((2 * hr, W), F32), compiler_params=_cparams(("parallel",)),
    )(chip_idx, c_idx, pslab, q, q, q)


def sum_devices(name, parts):
    _, R, W = parts.shape

    def body(p_ref, o_ref):
        acc = p_ref[0]
        for d in range(1, N_DEV):
            acc = acc + p_ref[d]
        o_ref[...] = acc

    tr = _rows(R, 512)
    return pl.pallas_call(
        body, name=name, grid=(R // tr,), in_specs=[pl.BlockSpec((N_DEV, tr, W), lambda i: (0, i, 0))],
        out_specs=pl.BlockSpec((tr, W), lambda i: (i, 0)), out_shape=_sds((R, W), F32),
        compiler_params=_cparams(("parallel",)),
    )(parts)


def adamw(name, g, g_off, w, m, v, w_off, rows, prev=None):
    W = w.shape[1]
    tr = _rows(rows, 256)

    def body(*refs):
        g_ref, w_ref, m_ref, v_ref = refs[:4]
        go_ref, d_ref, mo_ref, vo_ref = refs[-4:]
        gv = g_ref[...]
        mn = ADAM_B1 * m_ref[...] + (1.0 - ADAM_B1) * gv
        vn = ADAM_B2 * v_ref[...] + (1.0 - ADAM_B2) * (gv * gv)
        m_hat = mn / (1.0 - ADAM_B1 ** ADAM_STEP)
        v_hat = vn / (1.0 - ADAM_B2 ** ADAM_STEP)
        go_ref[...] = gv
        d_ref[...] = -ADAM_LR * (m_hat / (jnp.sqrt(v_hat) + ADAM_EPS) + ADAM_WD * w_ref[...])
        mo_ref[...] = mn
        vo_ref[...] = vn

    blk = pl.BlockSpec((tr, W), lambda i: (w_off // tr + i, 0))
    inputs = [g, w, m, v]
    in_specs = [pl.BlockSpec((tr, W), lambda i: (g_off // tr + i, 0)), blk, blk, blk]
    aliases = {}
    if prev is not None:
        aliases = {4 + t: t for t in range(4)}
        inputs += list(prev)
        in_specs += [ANY] * 4
    return pl.pallas_call(
        body, name=name, grid=(rows // tr,), in_specs=in_specs, out_specs=[blk] * 4,
        out_shape=[_sds(w.shape, F32)] * 4, input_output_aliases=aliases, compiler_params=_cparams(("parallel",)),
    )(*inputs)


def _pack(arrs):
    flat = jnp.concatenate([a.reshape(-1).astype(F32) for a in arrs])
    tile = SUBLANES * LANES * 2
    pad = (-flat.shape[0]) % tile
    return jnp.pad(flat, (0, pad)).reshape(-1, LANES)


def _unpack(packed, shapes):
    flat = packed.reshape(-1)
    out, pos = [], 0
    for s in shapes:
        n = 1
        for d in s:
            n *= d
        out.append(flat[pos:pos + n].reshape(s))
        pos += n
    return out


def kernel(x, norm_mix_g, norm_ffn_g, final_g, gm_w_in, gm_ln_g, gm_ln_b, gm_w_s, gm_b_s, gm_w_out, gla_w_in, gla_w_a1, gla_w_a2, gla_b_a, gla_norm_g, gla_w_o, ffn_w_up, ffn_w_down, loss_target, m_norm_mix_g, m_norm_ffn_g, m_final_g, m_gm_w_in, m_gm_ln_g, m_gm_ln_b, m_gm_w_s, m_gm_b_s, m_gm_w_out, m_gla_w_in, m_gla_w_a1, m_gla_w_a2, m_gla_b_a, m_gla_norm_g, m_gla_w_o, m_ffn_w_up, m_ffn_w_down, v_norm_mix_g, v_norm_ffn_g, v_final_g, v_gm_w_in, v_gm_ln_g, v_gm_ln_b, v_gm_w_s, v_gm_b_s, v_gm_w_out, v_gla_w_in, v_gla_w_a1, v_gla_w_a2, v_gla_b_a, v_gla_norm_g, v_gla_w_o, v_ffn_w_up, v_ffn_w_down):
    S, D = x.shape[1], x.shape[2]
    depth = norm_mix_g.shape[0]
    n_gm, n_gla = gm_w_in.shape[0], gla_w_in.shape[0]
    F = 4 * D
    P = GM_BLOCK
    xi, yi, ci = lax.axis_index("x"), lax.axis_index("y"), lax.axis_index("c")
    chip = 2 * xi + yi
    chip_idx = jnp.reshape(chip, (1,)).astype(jnp.int32)
    c_idx = jnp.reshape(ci, (1,)).astype(jnp.int32)

    UP, DOWN, OUT = 0, D, 2 * D
    rows_a = 2 * D + D // 4
    ha = rows_a // 2
    is_gm = [i % 2 == 0 for i in range(depth)]
    w_mix_in = [(gm_w_in, m_gm_w_in, v_gm_w_in) if is_gm[i] else (gla_w_in, m_gla_w_in, v_gla_w_in) for i in range(depth)]
    w_mix_out = [(gm_w_out, m_gm_w_out, v_gm_w_out) if is_gm[i] else (gla_w_o, m_gla_w_o, v_gla_w_o) for i in range(depth)]
    wm_cols = [D // 2 if is_gm[i] else 3 * D // 4 for i in range(depth)]

    def flat2(w):
        return w.reshape(w.shape[0] * w.shape[1], w.shape[2])

    wa, wmx = [], []
    for i in range(depth):
        j = i // 2
        buf = lax.empty((N_CHIPS, rows_a, D), BF16)
        buf = cast_into(f"cast_up_{i}", flat2(ffn_w_up), i * D, D, buf, UP, chip_idx)
        buf = cast_into(f"cast_down_{i}", flat2(ffn_w_down), i * D, D, buf, DOWN, chip_idx)
        buf = cast_into(f"cast_out_{i}", flat2(w_mix_out[i][0]), j * (D // 4), D // 4, buf, OUT, chip_idx)
        wa.append(buf)
        wmx.append(cast_into(f"cast_in_{i}", flat2(w_mix_in[i][0]), j * D, D, lax.empty((N_CHIPS, D, wm_cols[i]), BF16), 0, chip_idx))
    wa[0], wmx[0] = run_comm("allgather_layer0", comm_gather_all([wa[0], wmx[0]]))

    small_w = [gla_w_a1, gla_w_a2, gla_b_a, gla_norm_g]
    gs = allgather_devices("allgather_small_weights", _pack(small_w))
    per_chip = [_unpack(gs[2 * s], [a.shape for a in small_w]) for s in range(N_CHIPS)]
    w_a1 = jnp.concatenate([p[0] for p in per_chip], axis=1)
    w_a2 = jnp.concatenate([p[1] for p in per_chip], axis=2)
    b_a = jnp.concatenate([p[2] for p in per_chip], axis=1)
    gnorm = jnp.concatenate([p[3] for p in per_chip], axis=1)
    w_a1p = jnp.pad(w_a1, ((0, 0), (0, 0), (0, LOW - GATE_RANK))).astype(BF16)
    w_a2p = jnp.pad(w_a2, ((0, 0), (0, LOW - GATE_RANK), (0, 0))).astype(BF16)

    chunk_id = jnp.arange(P) // CHUNK
    mask = chunk_id[None, :] <= chunk_id[:, None]
    wm_all = jnp.where(mask[None, None], gm_w_s, 0.0)
    tri = jnp.tril(jnp.ones((CHUNK, CHUNK), F32))
    triT = tri.T

    q1 = _rows(ha // 4, ha // 4)
    xs = x[0]
    saved = []
    for i in range(depth):
        j = i // 2
        nxt = i + 1 < depth
        h1 = rmsnorm_fwd(f"norm_mix_{i}", xs, norm_mix_g[i][None])
        com = comm_gather_ici([wa[i + 1], wmx[i + 1]], [(1, 0, D // 2), (0, 0, q1)]) if nxt else None
        if is_gm[i]:
            zp = mm_fwd_col(f"gm_in_{i}", h1, wmx[i], 0, 2 * D, comm=com)
            if nxt:
                zp, wa[i + 1], wmx[i + 1] = zp
            wm = wm_all[j].astype(BF16)
            gated = gm_mid_fwd(f"gm_mid_{i}", zp, gm_ln_g[j][None], gm_ln_b[j][None], wm, gm_b_s[j].T)
            x_mid = mm_fwd_row(f"gm_out_{i}", gated, wa[i], OUT, D // 4, xs)
            mix = (h1, zp, gated)
        else:
            proj = mm_fwd_col(f"gla_in_{i}", h1, wmx[i], 0, 3 * D, comm=com)
            if nxt:
                proj, wa[i + 1], wmx[i + 1] = proj
            lr = mm_plain(f"gla_low_{i}", h1, w_a1p[j], NN, BF16)
            o_raw, states = gla_scan_fwd(f"gla_scan_{i}", proj, lr, w_a2p[j], b_a[j][None], tri)
            og = gla_post_fwd(f"gla_post_{i}", o_raw, proj, gnorm[j][None])
            x_mid = mm_fwd_row(f"gla_out_{i}", og, wa[i], OUT, D // 4, xs)
            mix = (h1, proj, lr, o_raw, states, og)
        h2 = rmsnorm_fwd(f"norm_ffn_{i}", x_mid, norm_ffn_g[i][None])
        com = comm_gather_ici([wa[i + 1]], [(0, q1, ha - q1)]) if nxt else None
        act = mm_fwd_col(f"ffn_up_{i}", h2, wa[i], UP, F, epilogue=lambda acc: jnp.maximum(acc, 0.0), comm=com)
        if nxt:
            act, wa[i + 1] = act
        com = comm_gather_forward([wa[i + 1], wmx[i + 1]]) if nxt else None
        x_out = mm_fwd_row(f"ffn_down_{i}", act, wa[i], DOWN, D, x_mid, prologue=lambda a: a * a, comm=com)
        if nxt:
            x_out, wa[i + 1], wmx[i + 1] = x_out
        saved.append((xs, x_mid, h2, act, mix))
        xs = x_out

    loss_part, dx, dxb, d_final_g = final_loss("final_loss", xs, final_g[None], loss_target[0])
    loss = lax.psum(loss_part[0, 0], ("x", "y", "c"))

    d_mix_g, d_ffn_g = [None] * depth, [None] * depth
    d_ln_g, d_ln_b, d_w_s, d_b_s = [None] * n_gm, [None] * n_gm, [None] * n_gm, [None] * n_gm
    d_a1, d_a2, d_ba, d_gn = [None] * n_gla, [None] * n_gla, [None] * n_gla, [None] * n_gla
    fa, fm = [None] * depth, [None] * depth
    pend = None
    s1 = _rows(ha // 2, ha // 2)
    for i in reversed(range(depth)):
        j = i // 2
        x_in, x_mid, h2, act, mix = saved[i]
        da = lax.empty((N_CHIPS, rows_a, D), BF16)
        dm = lax.empty((N_CHIPS, D, wm_cols[i]), BF16)

        com = comm_exchange(list(pend)) if pend else None
        d_apre = mm_bwd_row_x(f"ffn_down_dx_{i}", dxb, wa[i], DOWN, D, mul=act, comm=com)
        if pend:
            d_apre, ra, rm = d_apre
            pa = pair_add(f"pair_add_a_{i + 1}", pend[0], ra, c_idx)
            pm = pair_add(f"pair_add_m_{i + 1}", pend[1], rm, c_idx)
        com = comm_scatter([pa, pm], [(1, 0, D // 2), (0, 0, s1)]) if pend else None
        da = mm_bwd_row_w(f"ffn_down_dw_{i}", act, dxb, da, DOWN, D, prologue=lambda a: a * a, comm=com)
        if pend:
            da, qa, qm = da
        com = comm_scatter([pa], [(0, s1, ha - s1)], q_prev=[qa]) if pend else None
        da = mm_bwd_col_w(f"ffn_up_dw_{i}", h2, d_apre, da, UP, comm=com)
        if pend:
            da, qa = da
            fa[i + 1] = sum_chips(f"sum_chips_a_{i + 1}", pa, qa, chip_idx, c_idx)
            fm[i + 1] = sum_chips(f"sum_chips_m_{i + 1}", pm, qm, chip_idx, c_idx)
        com = comm_join([fa[i + 1], fm[i + 1]]) if pend else None
        dh2 = mm_bwd_col_x(f"ffn_up_dx_{i}", d_apre, wa[i], UP, D, comm=com)
        if pend:
            dh2, fa[i + 1], fm[i + 1] = dh2
        dx, dxb, d_ffn_g[i] = rmsnorm_bwd(f"norm_ffn_bwd_{i}", x_mid, norm_ffn_g[i][None], dh2, dx)
        if is_gm[i]:
            h1, zp, gated = mix
            d_gated = mm_bwd_row_x(f"gm_out_dx_{i}", dxb, wa[i], OUT, D // 4)
            da = mm_bwd_row_w(f"gm_out_dw_{i}", gated, dxb, da, OUT, D // 4)
            wm = wm_all[j].astype(BF16)
            wmT = jnp.swapaxes(wm_all[j], 1, 2).astype(BF16)
            dzp, d_ln_g[j], d_ln_b[j], dw, dbT = gm_mid_bwd(f"gm_mid_bwd_{i}", zp, d_gated, gm_ln_g[j][None], gm_ln_b[j][None],
                                                             wm, wmT, gm_b_s[j].T)
            d_w_s[j] = jnp.where(mask[None], dw, 0.0)
            d_b_s[j] = dbT.T
            dm = mm_bwd_col_w(f"gm_in_dw_{i}", h1, dzp, dm, 0)
            dh1 = mm_bwd_col_x(f"gm_in_dx_{i}", dzp, wmx[i], 0, D)
        else:
            h1, proj, lr, o_raw, states, og = mix
            d_og = mm_bwd_row_x(f"gla_out_dx_{i}", dxb, wa[i], OUT, D // 4)
            da = mm_bwd_row_w(f"gla_out_dw_{i}", og, dxb, da, OUT, D // 4)
            d_oraw, d_r, d_gn[j] = gla_post_bwd(f"gla_post_bwd_{i}", d_og, o_raw, proj, gnorm[j][None])
            dq, dk, dv, dlogit, d_ba[j] = gla_scan_bwd(f"gla_scan_bwd_{i}", proj, lr, w_a2p[j], b_a[j][None], tri, triT, states, d_oraw)
            dproj = jnp.concatenate([dq, dk, dv, d_r], axis=1)
            dm = mm_bwd_col_w(f"gla_in_dw_{i}", h1, dproj, dm, 0)
            dh1 = mm_bwd_col_x(f"gla_in_dx_{i}", dproj, wmx[i], 0, D)
            dlr = mm_plain(f"gla_gate_dlow_{i}", dlogit, w_a2p[j], NT, BF16)
            d_a2[j] = mm_plain(f"gla_gate_dw2_{i}", lr, dlogit, TN, F32)[:GATE_RANK]
            d_a1[j] = mm_plain(f"gla_gate_dw1_{i}", h1, dlr, TN, F32)[:, :GATE_RANK]
            dh1 = mm_plain(f"gla_gate_dx_{i}", dlr, w_a1p[j], NT, F32, add=dh1)
        dx, dxb, d_mix_g[i] = rmsnorm_bwd(f"norm_mix_bwd_{i}", x_in, norm_mix_g[i][None], dh1, dx)
        pend = (da, dm)
    grad_x = dx[None]

    ra, rm = run_comm("exchange_layer0", comm_exchange(list(pend)))
    pa = pair_add("pair_add_a_0", pend[0], ra, c_idx)
    pm = pair_add("pair_add_m_0", pend[1], rm, c_idx)
    qa, qm = run_comm("scatter_layer0", comm_scatter([pa, pm], [(0, 0, ha), (1, 0, D // 2)]))
    fa[0] = sum_chips("sum_chips_a_0", pa, qa, chip_idx, c_idx)
    fm[0] = sum_chips("sum_chips_m_0", pm, qm, chip_idx, c_idx)
    fa[0], fm[0] = run_comm("join_layer0", comm_join([fa[0], fm[0]]))

    small_g = [jnp.concatenate(d_mix_g), jnp.concatenate(d_ffn_g), d_final_g[0], jnp.concatenate(d_ln_g), jnp.concatenate(d_ln_b),
               jnp.stack(d_w_s), jnp.stack(d_b_s), jnp.stack(d_a1), jnp.stack(d_a2), jnp.concatenate(d_ba), jnp.concatenate(d_gn)]
    small_shapes = [(depth, D), (depth, D), (D,), (n_gm, D), (n_gm, D), (n_gm, GM_GROUPS, P, P), (n_gm, GM_GROUPS, P),
                    (n_gla, D, GATE_RANK), (n_gla, GATE_RANK, D // 2), (n_gla, D // 2), (n_gla, D)]
    parts = allgather_devices("allgather_small_grads", _pack(small_g))
    red = _unpack(sum_devices("sum_small_grads", parts), small_shapes)
    g_rep = red[:7]
    g_a1 = lax.dynamic_slice_in_dim(red[7], chip * (D // 4), D // 4, axis=1)
    g_a2 = lax.dynamic_slice_in_dim(red[8], chip * (D // 8), D // 8, axis=2)
    g_ba = lax.dynamic_slice_in_dim(red[9], chip * (D // 8), D // 8, axis=1)
    g_gn = lax.dynamic_slice_in_dim(red[10], chip * (D // 4), D // 4, axis=1)
    g_small = g_rep + [g_a1, g_a2, g_ba, g_gn]
    w_small = [norm_mix_g, norm_ffn_g, final_g, gm_ln_g, gm_ln_b, gm_w_s, gm_b_s, gla_w_a1, gla_w_a2, gla_b_a, gla_norm_g]
    m_small = [m_norm_mix_g, m_norm_ffn_g, m_final_g, m_gm_ln_g, m_gm_ln_b, m_gm_w_s, m_gm_b_s, m_gla_w_a1, m_gla_w_a2, m_gla_b_a, m_gla_norm_g]
    v_small = [v_norm_mix_g, v_norm_ffn_g, v_final_g, v_gm_ln_g, v_gm_ln_b, v_gm_w_s, v_gm_b_s, v_gla_w_a1, v_gla_w_a2, v_gla_b_a, v_gla_norm_g]
    shapes_small = [w.shape for w in w_small]
    n_small = _pack(g_small).shape[0]
    sm = adamw("adamw_small", _pack(g_small), 0, _pack(w_small), _pack(m_small), _pack(v_small), 0, n_small)
    sm = [_unpack(o, shapes_small) for o in sm]

    res = {}

    def big(key, i, g, g_off, wmv, w_off, rows):
        w, m, v = (flat2(t) for t in wmv)
        res[key] = adamw(f"adamw_{key}_{i}", g, g_off, w, m, v, w_off, rows, prev=res.get(key))

    for i in reversed(range(depth)):
        j = i // 2
        big("up", i, fa[i], UP, (ffn_w_up, m_ffn_w_up, v_ffn_w_up), i * D, D)
        big("down", i, fa[i], DOWN, (ffn_w_down, m_ffn_w_down, v_ffn_w_down), i * D, D)
        big("gm_out" if is_gm[i] else "gla_o", i, fa[i], OUT, w_mix_out[i], j * (D // 4), D // 4)
        big("gm_in" if is_gm[i] else "gla_in", i, fm[i], 0, w_mix_in[i], j * D, D)

    def shaped(key, like):
        return [o.reshape(like.shape) for o in res[key]]

    o_gm_in, o_gm_out = shaped("gm_in", gm_w_in), shaped("gm_out", gm_w_out)
    o_gla_in, o_gla_o = shaped("gla_in", gla_w_in), shaped("gla_o", gla_w_o)
    o_up, o_down = shaped("up", ffn_w_up), shaped("down", ffn_w_down)

    def ordered(kind):
        s = sm[kind]
        return [s[0], s[1], s[2], o_gm_in[kind], s[3], s[4], s[5], s[6], o_gm_out[kind], o_gla_in[kind],
                s[7], s[8], s[9], s[10], o_gla_o[kind], o_up[kind], o_down[kind]]

    return (loss, grad_x, *ordered(0), *ordered(1), *ordered(2), *ordered(3))
```

```python
import jax
import jax.numpy as jnp
from jax import lax
from jax.experimental import pallas as pl
from jax.experimental.pallas import tpu as pltpu

F32 = jnp.float32
BF16 = jnp.bfloat16

EPS = 1e-6
CHUNK = 64
GM_BLOCK = 128
GM_GROUPS = 8
GLA_HEADS = 4
GATE_RANK = 16
GATE_TAU = 16.0
LOW = 128
N_CHIPS = 4
N_DEV = 8

ADAM_LR = 0.001
ADAM_B1 = 0.9
ADAM_B2 = 0.999
ADAM_EPS = 1e-08
ADAM_WD = 0.01
ADAM_STEP = 10

V7X_VMEM_LIMIT = 48 * 1024 * 1024
LANES = 128
SUBLANES = 8
BF16_ROWS = 16
MESH = pl.DeviceIdType.MESH
HIGHEST = lax.Precision.HIGHEST


def _pick(n, cap):
    if n <= cap:
        return n
    best = LANES
    for t in range(LANES, cap + 1, LANES):
        if n % t == 0:
            best = t
    return best


def _rows(n, cap):
    if n <= cap:
        return n
    best = 0
    for t in range(BF16_ROWS, cap + 1, BF16_ROWS):
        if n % t == 0:
            best = t
    return best if best >= LANES else n


def _cparams(sem=None):
    return pltpu.CompilerParams(dimension_semantics=sem, vmem_limit_bytes=V7X_VMEM_LIMIT)


ANY = pl.BlockSpec(memory_space=pl.ANY)


def _sds(shape, dtype):
    return jax.ShapeDtypeStruct(shape, dtype)


def _place():
    return lax.axis_index("x"), lax.axis_index("y"), lax.axis_index("c")


def _other_chips(x, y):
    return [(1 - x, y), (x, 1 - y), (1 - x, 1 - y)]


class Comm:
    def __init__(self, ins, outs, alias, n_sems, start, wait):
        self.ins, self.outs, self.alias, self.n_sems, self.start, self.wait = ins, outs, alias, n_sems, start, wait


class _Shift:
    def __init__(self, sems, by):
        self.sems, self.by = sems, by

    @property
    def at(self):
        return self

    def __getitem__(self, k):
        return self.sems.at[self.by + k]


def comm_merge(*comms):
    comms = [c for c in comms if c is not None]
    if not comms:
        return None
    if len(comms) == 1:
        return comms[0]
    offs, total = [], 0
    for c in comms:
        offs.append(total)
        total += c.n_sems

    def start(R, ss, rs):
        for c, o in zip(comms, offs):
            c.start(R, _Shift(ss, o), _Shift(rs, o))

    def wait(R, ss, rs):
        for c, o in zip(comms, offs):
            c.wait(R, _Shift(ss, o), _Shift(rs, o))

    return Comm(sum((c.ins for c in comms), []), sum((c.outs for c in comms), []), sum((c.alias for c in comms), []),
                total, start, wait)


def _remote(src, dst, ss, rs, k, to):
    return pltpu.make_async_remote_copy(src_ref=src, dst_ref=dst, send_sem=ss.at[k], recv_sem=rs.at[k],
                                        device_id=to, device_id_type=MESH)


def comm_gather_ici(B, keys):
    hr = {k: B[k].shape[1] // 2 for k in keys}

    def region(R, k, chip, hc):
        return R[k].at[chip, pl.ds(hc * hr[k], hr[k]), :]

    def start(R, ss, rs):
        x, y, c = _place()
        for a, k in enumerate(keys):
            for j, (px, py) in enumerate(_other_chips(x, y)):
                mine = region(R, k, 2 * x + y, c)
                _remote(mine, mine, ss, rs, 3 * a + j, (px, py, c)).start()

    def wait(R, ss, rs):
        x, y, c = _place()
        for a, k in enumerate(keys):
            for j, (px, py) in enumerate(_other_chips(x, y)):
                theirs = region(R, k, 2 * px + py, c)
                _remote(theirs, theirs, ss, rs, 3 * a + j, (px, py, c)).wait_recv()
                mine = region(R, k, 2 * x + y, c)
                _remote(mine, mine, ss, rs, 3 * a + j, (px, py, c)).wait_send()

    return Comm([], [], list(keys), 3 * len(keys), start, wait)


def comm_gather_forward(B, keys):
    hr = {k: B[k].shape[1] // 2 for k in keys}

    def region(R, k, chip, hc):
        return R[k].at[chip, pl.ds(hc * hr[k], hr[k]), :]

    def start(R, ss, rs):
        x, y, c = _place()
        for a, k in enumerate(keys):
            for j, (px, py) in enumerate(_other_chips(x, y)):
                got = region(R, k, 2 * px + py, c)
                _remote(got, got, ss, rs, 3 * a + j, (x, y, 1 - c)).start()

    def wait(R, ss, rs):
        x, y, c = _place()
        for a, k in enumerate(keys):
            for j, (px, py) in enumerate(_other_chips(x, y)):
                other = region(R, k, 2 * px + py, 1 - c)
                _remote(other, other, ss, rs, 3 * a + j, (x, y, 1 - c)).wait_recv()
                got = region(R, k, 2 * px + py, c)
                _remote(got, got, ss, rs, 3 * a + j, (x, y, 1 - c)).wait_send()

    return Comm([], [], list(keys), 3 * len(keys), start, wait)


def comm_gather_all(B, keys):
    ici, fwd = comm_gather_ici(B, keys), comm_gather_forward(B, keys)
    n1 = ici.n_sems

    def wait(R, ss, rs):
        ici.wait(R, ss, rs)
        fwd.start(R, _Shift(ss, n1), _Shift(rs, n1))
        fwd.wait(R, _Shift(ss, n1), _Shift(rs, n1))

    return Comm([], [], list(keys), n1 + fwd.n_sems, ici.start, wait)


def comm_exchange(B, src_keys, out_keys):
    hr = {k: B[k].shape[1] // 2 for k in src_keys}

    def descr(R, ss, rs, a):
        x, y, c = _place()
        k = src_keys[a]
        return _remote(R[k].at[:, pl.ds((1 - c) * hr[k], hr[k]), :], R[out_keys[a]], ss, rs, a, (x, y, 1 - c))

    def start(R, ss, rs):
        for a in range(len(src_keys)):
            descr(R, ss, rs, a).start()

    def wait(R, ss, rs):
        for a in range(len(src_keys)):
            descr(R, ss, rs, a).wait()

    outs = [(o, _sds((N_CHIPS, hr[k], B[k].shape[2]), B[k].dtype)) for k, o in zip(src_keys, out_keys)]
    return Comm(list(src_keys), outs, [], len(src_keys), start, wait)


def comm_scatter(B, p_keys, q_keys):
    def each(R, ss, rs, fn):
        x, y, c = _place()
        for a, (pk, qk) in enumerate(zip(p_keys, q_keys)):
            for j, (px, py) in enumerate(_other_chips(x, y)):
                fn(_remote(R[pk].at[2 * px + py], R[qk].at[j], ss, rs, 3 * a + j, (px, py, c)))

    def start(R, ss, rs):
        each(R, ss, rs, lambda d: d.start())

    def wait(R, ss, rs):
        each(R, ss, rs, lambda d: d.wait())

    outs = [(qk, _sds((3,) + B[pk].shape[1:], B[pk].dtype)) for pk, qk in zip(p_keys, q_keys)]
    return Comm(list(p_keys), outs, [], 3 * len(p_keys), start, wait)


def comm_join(B, keys):
    hr = {k: B[k].shape[0] // 2 for k in keys}

    def region(R, k, hc):
        return R[k].at[pl.ds(hc * hr[k], hr[k]), :]

    def start(R, ss, rs):
        x, y, c = _place()
        for a, k in enumerate(keys):
            mine = region(R, k, c)
            _remote(mine, mine, ss, rs, a, (x, y, 1 - c)).start()

    def wait(R, ss, rs):
        x, y, c = _place()
        for a, k in enumerate(keys):
            other = region(R, k, 1 - c)
            _remote(other, other, ss, rs, a, (x, y, 1 - c)).wait_recv()
            mine = region(R, k, c)
            _remote(mine, mine, ss, rs, a, (x, y, 1 - c)).wait_send()

    return Comm([], [], list(keys), len(keys), start, wait)


def run_comm(name, comm, B):
    n_in, n_out, n_al = len(comm.ins), len(comm.outs), len(comm.alias)

    def body(*refs):
        R = dict(zip(comm.ins, refs[:n_in]))
        R.update(zip([k for k, _ in comm.outs], refs[n_in + n_al:n_in + n_al + n_out]))
        R.update(zip(comm.alias, refs[n_in + n_al + n_out:n_in + n_al + n_out + n_al]))
        ss, rs = refs[-2], refs[-1]
        comm.start(R, ss, rs)
        comm.wait(R, ss, rs)

    res = pl.pallas_call(
        body, name=name, in_specs=[ANY] * (n_in + n_al), out_specs=[ANY] * (n_out + n_al),
        out_shape=[s for _, s in comm.outs] + [_sds(B[k].shape, B[k].dtype) for k in comm.alias],
        input_output_aliases={n_in + t: n_out + t for t in range(n_al)},
        scratch_shapes=[pltpu.SemaphoreType.DMA((comm.n_sems,)), pltpu.SemaphoreType.DMA((comm.n_sems,))],
    )(*[B[k] for k in comm.ins], *[B[k] for k in comm.alias])
    B.update(zip([k for k, _ in comm.outs] + list(comm.alias), res))


def allgather_devices(name, v):
    def body(v_ref, out_ref, send_sems, recv_sems, local_sem):
        x, y, c = _place()
        me = 4 * x + 2 * y + c
        mine = pltpu.make_async_copy(v_ref, out_ref.at[me], local_sem)
        mine.start()

        def peer(k):
            return (1 - x if k & 4 else x, 1 - y if k & 2 else y, 1 - c if k & 1 else c)

        def copy(k, src, dst):
            return pltpu.make_async_remote_copy(src_ref=src, dst_ref=dst, send_sem=send_sems.at[k - 1],
                                                recv_sem=recv_sems.at[k - 1], device_id=peer(k), device_id_type=MESH)

        sends = [copy(k, v_ref, out_ref.at[me]) for k in range(1, N_DEV)]
        for cp in sends:
            cp.start()
        for k in range(1, N_DEV):
            px, py, pc = peer(k)
            them = 4 * px + 2 * py + pc
            copy(k, out_ref.at[them], out_ref.at[them]).wait_recv()
        for cp in sends:
            cp.wait_send()
        mine.wait()

    return pl.pallas_call(
        body, name=name, in_specs=[ANY], out_specs=ANY, out_shape=_sds((N_DEV,) + v.shape, v.dtype),
        scratch_shapes=[pltpu.SemaphoreType.DMA((N_DEV - 1,)), pltpu.SemaphoreType.DMA((N_DEV - 1,)), pltpu.SemaphoreType.DMA],
    )(v)


NN = ((1,), (0,))
NT = ((1,), (1,))
TN = ((0,), (0,))


def _mm(name, a, b, *, dims, grid, a_spec, b_spec, out_spec, out_shape, acc_shape, extra=(), extra_specs=(),
        prologue=None, epilogue=None, b_reshape=None, comm=None, B=None):
    gi, gj, nk = grid
    n_extra = len(extra)
    n_in, n_out, n_al = (len(comm.ins), len(comm.outs), len(comm.alias)) if comm is not None else (0, 0, 0)

    def body(*refs):
        a_ref, b_ref = refs[0], refs[1]
        ex = refs[2:2 + n_extra]
        pos = 2 + n_extra
        c_ins = refs[pos:pos + n_in]
        pos += n_in + n_al
        o_ref = refs[pos]
        c_outs = refs[pos + 1:pos + 1 + n_out]
        c_alias = refs[pos + 1 + n_out:pos + 1 + n_out + n_al]
        scratch = refs[pos + 1 + n_out + n_al:]
        i, j, k = pl.program_id(0), pl.program_id(1), pl.program_id(2)
        if comm is not None:
            ss, rs = scratch[-2], scratch[-1]
            R = dict(zip(comm.ins, c_ins))
            R.update(zip([key for key, _ in comm.outs], c_outs))
            R.update(zip(comm.alias, c_alias))

            @pl.when(jnp.logical_and(jnp.logical_and(i == 0, j == 0), k == 0))
            def _():
                comm.start(R, ss, rs)

        av = a_ref[...]
        if prologue is not None:
            av = prologue(av)
        bv = b_ref[...]
        if b_reshape is not None:
            bv = bv.reshape(b_reshape)
        p = lax.dot_general(av, bv, (dims, ((), ())), preferred_element_type=F32)

        def finish(acc):
            r = acc if epilogue is None else epilogue(acc, *ex)
            o_ref[...] = r.astype(o_ref.dtype)

        if nk == 1:
            finish(p)
        else:
            acc_ref = scratch[0]

            @pl.when(k == 0)
            def _():
                acc_ref[...] = p

            @pl.when(jnp.logical_and(k > 0, k < nk - 1))
            def _():
                acc_ref[...] += p

            @pl.when(k == nk - 1)
            def _():
                finish(acc_ref[...] + p)

        if comm is not None:
            @pl.when(jnp.logical_and(jnp.logical_and(i == gi - 1, j == gj - 1), k == nk - 1))
            def _():
                comm.wait(R, ss, rs)

    inputs = [a, b, *extra]
    in_specs = [a_spec, b_spec, *extra_specs]
    scratch_shapes = [pltpu.VMEM(acc_shape, F32)] if nk > 1 else []
    if comm is None:
        return pl.pallas_call(
            body, name=name, grid=grid, in_specs=in_specs, out_specs=out_spec, out_shape=out_shape,
            scratch_shapes=scratch_shapes, compiler_params=_cparams(("parallel", "parallel", "arbitrary")),
        )(*inputs)
    aliases = {len(inputs) + n_in + t: 1 + n_out + t for t in range(n_al)}
    inputs += [B[key] for key in comm.ins] + [B[key] for key in comm.alias]
    in_specs += [ANY] * (n_in + n_al)
    res = pl.pallas_call(
        body, name=name, grid=grid, in_specs=in_specs, out_specs=[out_spec] + [ANY] * (n_out + n_al),
        out_shape=[out_shape] + [s for _, s in comm.outs] + [_sds(B[key].shape, B[key].dtype) for key in comm.alias],
        scratch_shapes=scratch_shapes + [pltpu.SemaphoreType.DMA((comm.n_sems,)), pltpu.SemaphoreType.DMA((comm.n_sems,))],
        input_output_aliases=aliases, compiler_params=_cparams(("arbitrary", "arbitrary", "arbitrary")),
    )(*inputs)
    B.update(zip([key for key, _ in comm.outs] + list(comm.alias), res[1:]))
    return res[0]


def mm_fwd_col(name, h, wg, n_out, epilogue=None, out_dtype=BF16, **kw):
    S, D = h.shape
    W = wg.shape[2]
    tm, tn = min(1024, S), _pick(W, 1024)
    wps = W // tn
    return _mm(name, h, wg, dims=NN, grid=(S // tm, n_out // tn, 1),
               a_spec=pl.BlockSpec((tm, D), lambda i, j, k: (i, 0)),
               b_spec=pl.BlockSpec((None, D, tn), lambda i, j, k: (j // wps, 0, j % wps)),
               out_spec=pl.BlockSpec((tm, tn), lambda i, j, k: (i, j)),
               out_shape=_sds((S, n_out), out_dtype), acc_shape=(tm, tn), epilogue=epilogue, **kw)


def mm_fwd_row(name, a, wg, res, prologue=None, **kw):
    S, K = a.shape
    ksh, D = wg.shape[1], wg.shape[2]
    tm, tn, tk = min(1024, S), min(1024, D), min(2048, ksh)
    res_spec = pl.BlockSpec((tm, tn), lambda i, j, k: (i, j))
    if K <= 2048:
        return _mm(name, a, wg, dims=NN, grid=(S // tm, D // tn, 1),
                   a_spec=pl.BlockSpec((tm, K), lambda i, j, k: (i, 0)),
                   b_spec=pl.BlockSpec((N_CHIPS, ksh, tn), lambda i, j, k: (0, 0, j)),
                   out_spec=res_spec, out_shape=_sds((S, D), F32), acc_shape=(tm, tn),
                   extra=(res,), extra_specs=(res_spec,), prologue=prologue, epilogue=lambda acc, r: acc + r[...],
                   b_reshape=(K, tn), **kw)
    kps = ksh // tk
    return _mm(name, a, wg, dims=NN, grid=(S // tm, D // tn, K // tk),
               a_spec=pl.BlockSpec((tm, tk), lambda i, j, k: (i, k)),
               b_spec=pl.BlockSpec((None, tk, tn), lambda i, j, k: (k // kps, k % kps, j)),
               out_spec=res_spec, out_shape=_sds((S, D), F32), acc_shape=(tm, tn),
               extra=(res,), extra_specs=(res_spec,), prologue=prologue, epilogue=lambda acc, r: acc + r[...], **kw)


def mm_bwd_col_x(name, dz, wg, add=None, **kw):
    S, N = dz.shape
    D, W = wg.shape[1], wg.shape[2]
    tm, to, tr = min(1024, S), min(1024, D), _pick(W, 2048)
    wps = W // tr
    extra, extra_specs, epi = (), (), None
    if add is not None:
        extra, extra_specs = (add,), (pl.BlockSpec((tm, to), lambda i, j, k: (i, j)),)
        epi = lambda acc, r: acc + r[...]
    return _mm(name, dz, wg, dims=NT, grid=(S // tm, D // to, N // tr),
               a_spec=pl.BlockSpec((tm, tr), lambda i, j, k: (i, k)),
               b_spec=pl.BlockSpec((None, to, tr), lambda i, j, k: (k // wps, j, k % wps)),
               out_spec=pl.BlockSpec((tm, to), lambda i, j, k: (i, j)),
               out_shape=_sds((S, D), F32), acc_shape=(tm, to), extra=extra, extra_specs=extra_specs, epilogue=epi, **kw)


def mm_bwd_row_x(name, dxb, wg, mul=None, **kw):
    S, D = dxb.shape
    ksh = wg.shape[1]
    tm, tn = min(1024, S), min(1024, ksh)
    kps = ksh // tn
    extra, extra_specs, epi = (), (), None
    if mul is not None:
        extra, extra_specs = (mul,), (pl.BlockSpec((tm, tn), lambda i, j, k: (i, j)),)
        epi = lambda acc, r: acc * (2.0 * r[...].astype(F32))
    return _mm(name, dxb, wg, dims=NT, grid=(S // tm, 4 * kps, 1),
               a_spec=pl.BlockSpec((tm, D), lambda i, j, k: (i, 0)),
               b_spec=pl.BlockSpec((None, tn, D), lambda i, j, k: (j // kps, j % kps, 0)),
               out_spec=pl.BlockSpec((tm, tn), lambda i, j, k: (i, j)),
               out_shape=_sds((S, 4 * ksh), BF16), acc_shape=(tm, tn), extra=extra, extra_specs=extra_specs, epilogue=epi, **kw)


def mm_bwd_col_w(name, h, dz, **kw):
    S, D = h.shape
    N = dz.shape[1]
    W = N // N_CHIPS
    tk, tn, ts = min(1024, D), _pick(W, 1024), min(2048, S)
    wps = W // tn
    return _mm(name, h, dz, dims=TN, grid=(D // tk, N // tn, S // ts),
               a_spec=pl.BlockSpec((ts, tk), lambda i, j, k: (k, i)),
               b_spec=pl.BlockSpec((ts, tn), lambda i, j, k: (k, j)),
               out_spec=pl.BlockSpec((None, tk, tn), lambda i, j, k: (j // wps, i, j % wps)),
               out_shape=_sds((N_CHIPS, D, W), BF16), acc_shape=(tk, tn), **kw)


def mm_bwd_row_w(name, a, dxb, prologue=None, **kw):
    S, K = a.shape
    D = dxb.shape[1]
    ksh = K // N_CHIPS
    tk, tn, ts = min(1024, ksh), min(1024, D), min(2048, S)
    kps = ksh // tk
    return _mm(name, a, dxb, dims=TN, grid=(K // tk, D // tn, S // ts),
               a_spec=pl.BlockSpec((ts, tk), lambda i, j, k: (k, i)),
               b_spec=pl.BlockSpec((ts, tn), lambda i, j, k: (k, j)),
               out_spec=pl.BlockSpec((None, tk, tn), lambda i, j, k: (i // kps, i % kps, j)),
               out_shape=_sds((N_CHIPS, ksh, D), BF16), acc_shape=(tk, tn), prologue=prologue, **kw)


def mm_plain(name, a, b, dims, out_dtype, add=None):
    if dims == NN:
        M, N = a.shape[0], b.shape[1]
    elif dims == NT:
        M, N = a.shape[0], b.shape[0]
    else:
        M, N = a.shape[1], b.shape[1]
    red = a.shape[0] if dims == TN else a.shape[1]
    tm, tn = min(1024, M), min(1024, N)
    tr = min(1024, red) if dims == TN else red
    nk = red // tr
    if dims == TN:
        a_spec = pl.BlockSpec((tr, tm), lambda i, j, k: (k, i))
        b_spec = pl.BlockSpec((tr, tn), lambda i, j, k: (k, j))
    elif dims == NN:
        a_spec = pl.BlockSpec((tm, tr), lambda i, j, k: (i, k))
        b_spec = pl.BlockSpec((tr, tn), lambda i, j, k: (k, j))
    else:
        a_spec = pl.BlockSpec((tm, tr), lambda i, j, k: (i, k))
        b_spec = pl.BlockSpec((tn, tr), lambda i, j, k: (j, k))
    extra, extra_specs, epi = (), (), None
    if add is not None:
        extra, extra_specs = (add,), (pl.BlockSpec((tm, tn), lambda i, j, k: (i, j)),)
        epi = lambda acc, r: acc + r[...]
    return _mm(name, a, b, dims=dims, grid=(M // tm, N // tn, nk), a_spec=a_spec, b_spec=b_spec,
               out_spec=pl.BlockSpec((tm, tn), lambda i, j, k: (i, j)),
               out_shape=_sds((M, N), out_dtype), acc_shape=(tm, tn), extra=extra, extra_specs=extra_specs, epilogue=epi)


ROWS = 256


def rmsnorm_fwd(name, x, g):
    S, D = x.shape
    tr = min(ROWS, S)

    def body(x_ref, g_ref, h_ref):
        xv = x_ref[...]
        rstd = lax.rsqrt(jnp.mean(xv * xv, axis=-1, keepdims=True) + EPS)
        h_ref[...] = (xv * rstd * g_ref[...]).astype(BF16)

    return pl.pallas_call(
        body, name=name, grid=(S // tr,),
        in_specs=[pl.BlockSpec((tr, D), lambda i: (i, 0)), pl.BlockSpec((1, D), lambda i: (0, 0))],
        out_specs=pl.BlockSpec((tr, D), lambda i: (i, 0)), out_shape=_sds((S, D), BF16),
        compiler_params=_cparams(("parallel",)),
    )(x, g)


def rmsnorm_bwd(name, x, g, dh, dres):
    S, D = x.shape
    tr = min(ROWS, S)

    def body(x_ref, g_ref, dh_ref, dres_ref, dx_ref, dxb_ref, dg_ref):
        xv = x_ref[...]
        rstd = lax.rsqrt(jnp.mean(xv * xv, axis=-1, keepdims=True) + EPS)
        xh = xv * rstd
        dy = dh_ref[...]
        dxh = dy * g_ref[...]
        dx = dres_ref[...] + rstd * (dxh - xh * jnp.mean(dxh * xh, axis=-1, keepdims=True))
        dx_ref[...] = dx
        dxb_ref[...] = dx.astype(BF16)

        @pl.when(pl.program_id(0) == 0)
        def _():
            dg_ref[...] = jnp.zeros_like(dg_ref)

        dg_ref[...] += jnp.sum(dy * xh, axis=0, keepdims=True)

    row = pl.BlockSpec((tr, D), lambda i: (i, 0))
    vec = pl.BlockSpec((1, D), lambda i: (0, 0))
    return pl.pallas_call(
        body, name=name, grid=(S // tr,), in_specs=[row, vec, row, row], out_specs=[row, row, vec],
        out_shape=[_sds((S, D), F32), _sds((S, D), BF16), _sds((1, D), F32)],
        compiler_params=_cparams(("arbitrary",)),
    )(x, g, dh, dres)


def final_loss(name, x, g, target):
    S, D = x.shape
    tr = min(ROWS, S)

    def body(x_ref, g_ref, t_ref, loss_ref, dx_ref, dxb_ref, dg_ref):
        xv = x_ref[...]
        gv = g_ref[...]
        rstd = lax.rsqrt(jnp.mean(xv * xv, axis=-1, keepdims=True) + EPS)
        xh = xv * rstd
        err = xh * gv - t_ref[...]
        dy = err * (1.0 / D)
        dxh = dy * gv
        dx = rstd * (dxh - xh * jnp.mean(dxh * xh, axis=-1, keepdims=True))
        dx_ref[...] = dx
        dxb_ref[...] = dx.astype(BF16)

        @pl.when(pl.program_id(0) == 0)
        def _():
            dg_ref[...] = jnp.zeros_like(dg_ref)
            loss_ref[...] = jnp.zeros_like(loss_ref)

        dg_ref[...] += jnp.sum(dy * xh, axis=0, keepdims=True)
        loss_ref[...] += 0.5 * jnp.sum(jnp.mean(err * err, axis=-1, keepdims=True))

    row = pl.BlockSpec((tr, D), lambda i: (i, 0))
    vec = pl.BlockSpec((1, D), lambda i: (0, 0))
    return pl.pallas_call(
        body, name=name, grid=(S // tr,), in_specs=[row, vec, row],
        out_specs=[pl.BlockSpec((SUBLANES, LANES), lambda i: (0, 0)), row, row, vec],
        out_shape=[_sds((SUBLANES, LANES), F32), _sds((S, D), F32), _sds((S, D), BF16), _sds((1, D), F32)],
        compiler_params=_cparams(("arbitrary",)),
    )(x, g, target)


def _gelu(x):
    return 0.5 * x * (1.0 + lax.erf(x * 0.7071067811865476))


def _gelu_grad(x):
    return 0.5 * (1.0 + lax.erf(x * 0.7071067811865476)) + x * jnp.exp(-0.5 * x * x) * 0.3989422804014327


def _gm_common(zp, lng, lnb, D):
    z = _gelu(zp)
    u, v = z[:, :D], z[:, D:]
    xc = v - jnp.mean(v, axis=-1, keepdims=True)
    rstd = lax.rsqrt(jnp.mean(xc * xc, axis=-1, keepdims=True) + EPS)
    xh = xc * rstd
    return u, xh, rstd, xh * lng + lnb


def gm_mid_fwd(name, zp, lng, lnb, wm, bT):
    S, D2 = zp.shape
    D = D2 // 2
    dg = D // GM_GROUPS
    P = GM_BLOCK

    def body(z_ref, lng_ref, lnb_ref, wm_ref, bT_ref, o_ref):
        u, _, _, vn = _gm_common(z_ref[...].astype(F32), lng_ref[...], lnb_ref[...], D)
        vnb = vn.astype(BF16)
        for gi in range(GM_GROUPS):
            cols = slice(gi * dg, (gi + 1) * dg)
            mixed = jnp.dot(wm_ref[gi], vnb[:, cols], preferred_element_type=F32) + bT_ref[:, gi:gi + 1]
            o_ref[:, cols] = (u[:, cols] * mixed).astype(BF16)

    vec = pl.BlockSpec((1, D), lambda i: (0, 0))
    return pl.pallas_call(
        body, name=name, grid=(S // P,),
        in_specs=[pl.BlockSpec((P, D2), lambda i: (i, 0)), vec, vec,
                  pl.BlockSpec((GM_GROUPS, P, P), lambda i: (0, 0, 0)), pl.BlockSpec((P, GM_GROUPS), lambda i: (0, 0))],
        out_specs=pl.BlockSpec((P, D), lambda i: (i, 0)), out_shape=_sds((S, D), BF16),
        compiler_params=_cparams(("parallel",)),
    )(zp, lng, lnb, wm, bT)


def gm_mid_bwd(name, zp, dgated, lng, lnb, wm, wmT, bT):
    S, D2 = zp.shape
    D = D2 // 2
    dg = D // GM_GROUPS
    P = GM_BLOCK

    def body(z_ref, dgt_ref, lng_ref, lnb_ref, wm_ref, wmT_ref, bT_ref, dz_ref, dlng_ref, dlnb_ref, dw_ref, dbT_ref, dvn_ref):
        @pl.when(pl.program_id(0) == 0)
        def _():
            dlng_ref[...] = jnp.zeros_like(dlng_ref)
            dlnb_ref[...] = jnp.zeros_like(dlnb_ref)
            dw_ref[...] = jnp.zeros_like(dw_ref)
            dbT_ref[...] = jnp.zeros_like(dbT_ref)

        zp_v = z_ref[...].astype(F32)
        lng_v = lng_ref[...]
        u, xh, rstd, vn = _gm_common(zp_v, lng_v, lnb_ref[...], D)
        vnb = vn.astype(BF16)
        dgt = dgt_ref[...].astype(F32)
        for gi in range(GM_GROUPS):
            cols = slice(gi * dg, (gi + 1) * dg)
            mixed = jnp.dot(wm_ref[gi], vnb[:, cols], preferred_element_type=F32) + bT_ref[:, gi:gi + 1]
            dm = dgt[:, cols] * u[:, cols]
            dmb = dm.astype(BF16)
            dz_ref[:, cols] = (dgt[:, cols] * mixed * _gelu_grad(zp_v[:, cols])).astype(BF16)
            dbT_ref[:, gi:gi + 1] += jnp.sum(dm, axis=1, keepdims=True)
            dw_ref[gi] += lax.dot_general(dmb, vnb[:, cols], (NT, ((), ())), preferred_element_type=F32)
            dvn_ref[:, cols] = jnp.dot(wmT_ref[gi], dmb, preferred_element_type=F32)
        dvn = dvn_ref[...]
        dlng_ref[...] += jnp.sum(dvn * xh, axis=0, keepdims=True)
        dlnb_ref[...] += jnp.sum(dvn, axis=0, keepdims=True)
        dyg = dvn * lng_v
        dv = rstd * (dyg - jnp.mean(dyg, axis=-1, keepdims=True) - xh * jnp.mean(dyg * xh, axis=-1, keepdims=True))
        dz_ref[:, D:] = (dv * _gelu_grad(zp_v[:, D:])).astype(BF16)

    vec = pl.BlockSpec((1, D), lambda i: (0, 0))
    wsp = pl.BlockSpec((GM_GROUPS, P, P), lambda i: (0, 0, 0))
    bsp = pl.BlockSpec((P, GM_GROUPS), lambda i: (0, 0))
    return pl.pallas_call(
        body, name=name, grid=(S // P,),
        in_specs=[pl.BlockSpec((P, D2), lambda i: (i, 0)), pl.BlockSpec((P, D), lambda i: (i, 0)), vec, vec, wsp, wsp, bsp],
        out_specs=[pl.BlockSpec((P, D2), lambda i: (i, 0)), vec, vec, wsp, bsp],
        out_shape=[_sds((S, D2), BF16), _sds((1, D), F32), _sds((1, D), F32), _sds((GM_GROUPS, P, P), F32), _sds((P, GM_GROUPS), F32)],
        scratch_shapes=[pltpu.VMEM((P, D), F32)],
        compiler_params=_cparams(("arbitrary",)),
    )(zp, dgated, lng, lnb, wm, wmT, bT)


def _gla_gate(lr, w2, ba, tri):
    logit = jnp.dot(lr, w2, preferred_element_type=F32) + ba
    la = (jnp.minimum(logit, 0.0) - jnp.log1p(jnp.exp(-jnp.abs(logit)))) * (1.0 / GATE_TAU)
    g = jnp.dot(tri, la, preferred_element_type=F32, precision=HIGHEST)
    return logit, g


def gla_scan_fwd(name, proj, lr, w2p, ba, tri):
    S, D3 = proj.shape
    D = D3 // 3
    H, C = GLA_HEADS, CHUNK
    dk, dv = D // 2 // H, D // H
    NC = S // C
    scale = dk ** -0.5

    def body(q_ref, k_ref, v_ref, lr_ref, w2_ref, ba_ref, tri_ref, o_ref, st_ref, state):
        @pl.when(pl.program_id(0) == 0)
        def _():
            state[...] = jnp.zeros_like(state)

        _, g = _gla_gate(lr_ref[...], w2_ref[...], ba_ref[...], tri_ref[...])
        gend = g[C - 1:C, :]
        kdec = (k_ref[...].astype(F32) * jnp.exp(gend - g)).astype(BF16)
        dec = jnp.exp(gend)
        qs = (q_ref[...].astype(F32) * scale).astype(BF16)
        for hh in range(H):
            kc, vc = slice(hh * dk, (hh + 1) * dk), slice(hh * dv, (hh + 1) * dv)
            kv = lax.dot_general(v_ref[:, vc], kdec[:, kc], (TN, ((), ())), preferred_element_type=F32)
            new = dec[:, kc] * state[hh] + kv
            state[hh] = new
            nb = new.astype(BF16)
            st_ref[hh] = nb
            o_ref[:, vc] = lax.dot_general(qs[:, kc], nb, (NT, ((), ())), preferred_element_type=F32).astype(BF16)

    return pl.pallas_call(
        body, name=name, grid=(NC,),
        in_specs=[pl.BlockSpec((C, D // 2), lambda t: (t, 0)), pl.BlockSpec((C, D // 2), lambda t: (t, 1)),
                  pl.BlockSpec((C, D), lambda t: (t, 1)), pl.BlockSpec((C, LOW), lambda t: (t, 0)),
                  pl.BlockSpec((LOW, D // 2), lambda t: (0, 0)), pl.BlockSpec((1, D // 2), lambda t: (0, 0)),
                  pl.BlockSpec((C, C), lambda t: (0, 0))],
        out_specs=[pl.BlockSpec((C, D), lambda t: (t, 0)), pl.BlockSpec((None, H, dv, dk), lambda t: (t, 0, 0, 0))],
        out_shape=[_sds((S, D), BF16), _sds((NC, H, dv, dk), BF16)],
        scratch_shapes=[pltpu.VMEM((H, dv, dk), F32)],
        compiler_params=_cparams(("arbitrary",)),
    )(proj, proj, proj, lr, w2p, ba, tri)


def gla_scan_bwd(name, proj, lr, w2p, ba, tri, triT, states, do):
    S, D3 = proj.shape
    D = D3 // 3
    H, C = GLA_HEADS, CHUNK
    dk, dv = D // 2 // H, D // H
    NC = S // C
    scale = dk ** -0.5

    def body(q_ref, k_ref, v_ref, lr_ref, w2_ref, ba_ref, tri_ref, triT_ref, st_ref, sp_ref, do_ref,
             dq_ref, dk_ref, dv_ref, dl_ref, dba_ref, dstate, dkd_ref, ddec_ref):
        t = pl.program_id(0)

        @pl.when(t == 0)
        def _():
            dstate[...] = jnp.zeros_like(dstate)
            dba_ref[...] = jnp.zeros_like(dba_ref)

        logit, g = _gla_gate(lr_ref[...], w2_ref[...], ba_ref[...], tri_ref[...])
        gend = g[C - 1:C, :]
        e = jnp.exp(gend - g)
        kf = k_ref[...].astype(F32)
        kdec = (kf * e).astype(BF16)
        dec = jnp.exp(gend)
        qs = (q_ref[...].astype(F32) * scale).astype(BF16)
        has_prev = (t < NC - 1).astype(F32)
        for hh in range(H):
            kc, vc = slice(hh * dk, (hh + 1) * dk), slice(hh * dv, (hh + 1) * dv)
            dob = do_ref[:, vc]
            dq_ref[:, kc] = (jnp.dot(dob, st_ref[hh], preferred_element_type=F32) * scale).astype(BF16)
            ds = dstate[hh] + lax.dot_general(dob, qs[:, kc], (TN, ((), ())), preferred_element_type=F32)
            dsb = ds.astype(BF16)
            dkd_ref[:, kc] = jnp.dot(v_ref[:, vc], dsb, preferred_element_type=F32)
            dv_ref[:, vc] = lax.dot_general(kdec[:, kc], dsb, (NT, ((), ())), preferred_element_type=F32).astype(BF16)
            ddec_ref[:, kc] = jnp.sum(ds * sp_ref[hh].astype(F32), axis=0, keepdims=True) * has_prev
            dstate[hh] = dec[:, kc] * ds
        dkdec = dkd_ref[...]
        dk_ref[...] = (dkdec * e).astype(BF16)
        dd = dkdec * kf * e
        dgend = jnp.sum(dd, axis=0, keepdims=True) + ddec_ref[...] * dec
        last = lax.broadcasted_iota(jnp.int32, (C, 1), 0) == C - 1
        dg = jnp.where(last, dgend, 0.0) - dd
        dla = jnp.dot(triT_ref[...], dg, preferred_element_type=F32, precision=HIGHEST)
        dlogit = dla * (1.0 / GATE_TAU) * (1.0 - jax.nn.sigmoid(logit))
        dl_ref[...] = dlogit.astype(BF16)
        dba_ref[...] += jnp.sum(dlogit, axis=0, keepdims=True)

    rev = lambda t: NC - 1 - t
    half = pl.BlockSpec((C, D // 2), lambda t: (rev(t), 0))
    full = pl.BlockSpec((C, D), lambda t: (rev(t), 0))
    return pl.pallas_call(
        body, name=name, grid=(NC,),
        in_specs=[half, pl.BlockSpec((C, D // 2), lambda t: (rev(t), 1)), pl.BlockSpec((C, D), lambda t: (rev(t), 1)),
                  pl.BlockSpec((C, LOW), lambda t: (rev(t), 0)),
                  pl.BlockSpec((LOW, D // 2), lambda t: (0, 0)), pl.BlockSpec((1, D // 2), lambda t: (0, 0)),
                  pl.BlockSpec((C, C), lambda t: (0, 0)), pl.BlockSpec((C, C), lambda t: (0, 0)),
                  pl.BlockSpec((None, H, dv, dk), lambda t: (rev(t), 0, 0, 0)),
                  pl.BlockSpec((None, H, dv, dk), lambda t: (jnp.maximum(rev(t) - 1, 0), 0, 0, 0)),
                  full],
        out_specs=[half, half, full, half, pl.BlockSpec((1, D // 2), lambda t: (0, 0))],
        out_shape=[_sds((S, D // 2), BF16), _sds((S, D // 2), BF16), _sds((S, D), BF16), _sds((S, D // 2), BF16),
                   _sds((1, D // 2), F32)],
        scratch_shapes=[pltpu.VMEM((H, dv, dk), F32), pltpu.VMEM((C, D // 2), F32), pltpu.VMEM((1, D // 2), F32)],
        compiler_params=_cparams(("arbitrary",)),
    )(proj, proj, proj, lr, w2p, ba, tri, triT, states, states, do)


def _gla_post_common(o, r, ng):
    rs = lax.rsqrt(jnp.mean(o * o, axis=-1, keepdims=True) + EPS)
    oh = o * rs
    sig = jax.nn.sigmoid(r)
    return rs, oh, oh * ng, sig, r * sig


def gla_post_fwd(name, o_raw, proj, ng):
    S, D = o_raw.shape
    dv = D // GLA_HEADS
    tr = min(ROWS, S)

    def body(o_ref, r_ref, ng_ref, og_ref):
        for hh in range(GLA_HEADS):
            cols = slice(hh * dv, (hh + 1) * dv)
            _, _, on, _, sil = _gla_post_common(o_ref[:, cols].astype(F32), r_ref[:, cols].astype(F32), ng_ref[:, cols])
            og_ref[:, cols] = (on * sil).astype(BF16)

    row = pl.BlockSpec((tr, D), lambda i: (i, 0))
    return pl.pallas_call(
        body, name=name, grid=(S // tr,),
        in_specs=[row, pl.BlockSpec((tr, D), lambda i: (i, 2)), pl.BlockSpec((1, D), lambda i: (0, 0))],
        out_specs=row, out_shape=_sds((S, D), BF16), compiler_params=_cparams(("parallel",)),
    )(o_raw, proj, ng)


def gla_post_bwd(name, dog, o_raw, proj, ng):
    S, D = o_raw.shape
    dv = D // GLA_HEADS
    tr = min(ROWS, S)

    def body(dog_ref, o_ref, r_ref, ng_ref, do_ref, dr_ref, dng_ref):
        @pl.when(pl.program_id(0) == 0)
        def _():
            dng_ref[...] = jnp.zeros_like(dng_ref)

        for hh in range(GLA_HEADS):
            cols = slice(hh * dv, (hh + 1) * dv)
            r = r_ref[:, cols].astype(F32)
            ngv = ng_ref[:, cols]
            rs, oh, on, sig, sil = _gla_post_common(o_ref[:, cols].astype(F32), r, ngv)
            dogv = dog_ref[:, cols].astype(F32)
            don = dogv * sil
            dr_ref[:, cols] = (dogv * on * (sig * (1.0 + r * (1.0 - sig)))).astype(BF16)
            dng_ref[:, cols] += jnp.sum(don * oh, axis=0, keepdims=True)
            doh = don * ngv
            do_ref[:, cols] = (rs * (doh - oh * jnp.mean(doh * oh, axis=-1, keepdims=True))).astype(BF16)

    row = pl.BlockSpec((tr, D), lambda i: (i, 0))
    vec = pl.BlockSpec((1, D), lambda i: (0, 0))
    return pl.pallas_call(
        body, name=name, grid=(S // tr,),
        in_specs=[row, row, pl.BlockSpec((tr, D), lambda i: (i, 2)), vec],
        out_specs=[row, row, vec], out_shape=[_sds((S, D), BF16), _sds((S, D), BF16), _sds((1, D), F32)],
        compiler_params=_cparams(("arbitrary",)),
    )(dog, o_raw, proj, ng)


def cast_into(name, w, w_off, rows, chip_idx):
    W = w.shape[1]
    tr = _rows(rows, 512)

    def body(p_ref, w_ref, o_ref):
        o_ref[...] = w_ref[...].astype(BF16)

    return pl.pallas_call(
        body, name=name,
        grid_spec=pltpu.PrefetchScalarGridSpec(
            num_scalar_prefetch=1, grid=(rows // tr,),
            in_specs=[pl.BlockSpec((tr, W), lambda i, p: (w_off // tr + i, 0))],
            out_specs=pl.BlockSpec((None, tr, W), lambda i, p: (p[0], i, 0))),
        out_shape=_sds((N_CHIPS, rows, W), BF16), compiler_params=_cparams(("parallel",)),
    )(chip_idx, w)


def pair_add(name, gbuf, rsib, c_idx):
    _, R, W = gbuf.shape
    hr = R // 2
    tr = _rows(hr, 512)
    nb = hr // tr

    def body(c_ref, a_ref, b_ref, o_ref):
        o_ref[...] = (a_ref[...].astype(F32) + b_ref[...].astype(F32)).astype(BF16)

    return pl.pallas_call(
        body, name=name,
        grid_spec=pltpu.PrefetchScalarGridSpec(
            num_scalar_prefetch=1, grid=(N_CHIPS, nb),
            in_specs=[pl.BlockSpec((None, tr, W), lambda s, i, c: (s, c[0] * nb + i, 0)),
                      pl.BlockSpec((None, tr, W), lambda s, i, c: (s, i, 0))],
            out_specs=pl.BlockSpec((None, tr, W), lambda s, i, c: (s, i, 0))),
        out_shape=_sds((N_CHIPS, hr, W), BF16), compiler_params=_cparams(("parallel", "parallel")),
    )(c_idx, gbuf, rsib)


def sum_chips(name, p, q, chip_idx, c_idx):
    _, hr, W = p.shape
    tr = _rows(hr, 512)
    nb = hr // tr

    def body(s_ref, c_ref, p_ref, q0_ref, q1_ref, q2_ref, o_ref):
        o_ref[...] = ((p_ref[...].astype(F32) + q0_ref[...].astype(F32)) + q1_ref[...].astype(F32)) + q2_ref[...].astype(F32)

    def qspec(j):
        return pl.BlockSpec((None, tr, W), lambda i, s, c: (j, i, 0))

    return pl.pallas_call(
        body, name=name,
        grid_spec=pltpu.PrefetchScalarGridSpec(
            num_scalar_prefetch=2, grid=(nb,),
            in_specs=[pl.BlockSpec((None, tr, W), lambda i, s, c: (s[0], i, 0)), qspec(0), qspec(1), qspec(2)],
            out_specs=pl.BlockSpec((tr, W), lambda i, s, c: (c[0] * nb + i, 0))),
        out_shape=_sds((2 * hr, W), F32), compiler_params=_cparams(("parallel",)),
    )(chip_idx, c_idx, p, q, q, q)


def sum_devices(name, parts):
    _, R, W = parts.shape

    def body(p_ref, o_ref):
        acc = p_ref[0]
        for d in range(1, N_DEV):
            acc = acc + p_ref[d]
        o_ref[...] = acc

    tr = _rows(R, 512)
    return pl.pallas_call(
        body, name=name, grid=(R // tr,), in_specs=[pl.BlockSpec((N_DEV, tr, W), lambda i: (0, i, 0))],
        out_specs=pl.BlockSpec((tr, W), lambda i: (i, 0)), out_shape=_sds((R, W), F32),
        compiler_params=_cparams(("parallel",)),
    )(parts)


def adamw(name, g, w, m, v, w_off, prev=None):
    rows, W = g.shape
    tr = _rows(rows, 256)

    def body(*refs):
        g_ref, w_ref, m_ref, v_ref = refs[:4]
        go_ref, d_ref, mo_ref, vo_ref = refs[-4:]
        gv = g_ref[...]
        mn = ADAM_B1 * m_ref[...] + (1.0 - ADAM_B1) * gv
        vn = ADAM_B2 * v_ref[...] + (1.0 - ADAM_B2) * (gv * gv)
        m_hat = mn / (1.0 - ADAM_B1 ** ADAM_STEP)
        v_hat = vn / (1.0 - ADAM_B2 ** ADAM_STEP)
        go_ref[...] = gv
        d_ref[...] = -ADAM_LR * (m_hat / (jnp.sqrt(v_hat) + ADAM_EPS) + ADAM_WD * w_ref[...])
        mo_ref[...] = mn
        vo_ref[...] = vn

    blk = pl.BlockSpec((tr, W), lambda i: (w_off // tr + i, 0))
    inputs = [g, w, m, v]
    in_specs = [pl.BlockSpec((tr, W), lambda i: (i, 0)), blk, blk, blk]
    aliases = {}
    if prev is not None:
        aliases = {4 + t: t for t in range(4)}
        inputs += list(prev)
        in_specs += [ANY] * 4
    return pl.pallas_call(
        body, name=name, grid=(rows // tr,), in_specs=in_specs, out_specs=[blk] * 4,
        out_shape=[_sds(w.shape, F32)] * 4, input_output_aliases=aliases, compiler_params=_cparams(("parallel",)),
    )(*inputs)


def _pack(arrs):
    flat = jnp.concatenate([a.reshape(-1).astype(F32) for a in arrs])
    tile = SUBLANES * LANES * 2
    pad = (-flat.shape[0]) % tile
    return jnp.pad(flat, (0, pad)).reshape(-1, LANES)


def _unpack(packed, shapes):
    flat = packed.reshape(-1)
    out, pos = [], 0
    for s in shapes:
        n = 1
        for d in s:
            n *= d
        out.append(flat[pos:pos + n].reshape(s))
        pos += n
    return out


def kernel(x, norm_mix_g, norm_ffn_g, final_g, gm_w_in, gm_ln_g, gm_ln_b, gm_w_s, gm_b_s, gm_w_out, gla_w_in, gla_w_a1, gla_w_a2, gla_b_a, gla_norm_g, gla_w_o, ffn_w_up, ffn_w_down, loss_target, m_norm_mix_g, m_norm_ffn_g, m_final_g, m_gm_w_in, m_gm_ln_g, m_gm_ln_b, m_gm_w_s, m_gm_b_s, m_gm_w_out, m_gla_w_in, m_gla_w_a1, m_gla_w_a2, m_gla_b_a, m_gla_norm_g, m_gla_w_o, m_ffn_w_up, m_ffn_w_down, v_norm_mix_g, v_norm_ffn_g, v_final_g, v_gm_w_in, v_gm_ln_g, v_gm_ln_b, v_gm_w_s, v_gm_b_s, v_gm_w_out, v_gla_w_in, v_gla_w_a1, v_gla_w_a2, v_gla_b_a, v_gla_norm_g, v_gla_w_o, v_ffn_w_up, v_ffn_w_down):
    S, D = x.shape[1], x.shape[2]
    depth = norm_mix_g.shape[0]
    n_gm, n_gla = gm_w_in.shape[0], gla_w_in.shape[0]
    F = 4 * D
    P = GM_BLOCK
    xi, yi, ci = lax.axis_index("x"), lax.axis_index("y"), lax.axis_index("c")
    chip = 2 * xi + yi
    chip_idx = jnp.reshape(chip, (1,)).astype(jnp.int32)
    c_idx = jnp.reshape(ci, (1,)).astype(jnp.int32)

    is_gm = [i % 2 == 0 for i in range(depth)]
    w_mix_in = [(gm_w_in, m_gm_w_in, v_gm_w_in) if is_gm[i] else (gla_w_in, m_gla_w_in, v_gla_w_in) for i in range(depth)]
    w_mix_out = [(gm_w_out, m_gm_w_out, v_gm_w_out) if is_gm[i] else (gla_w_o, m_gla_w_o, v_gla_w_o) for i in range(depth)]
    w_up = (ffn_w_up, m_ffn_w_up, v_ffn_w_up)
    w_down = (ffn_w_down, m_ffn_w_down, v_ffn_w_down)

    def flat2(w):
        return w.reshape(w.shape[0] * w.shape[1], w.shape[2])

    B = {}

    for i in range(depth):
        j = i // 2
        B[f"w_in_{i}"] = cast_into(f"cast_in_{i}", flat2(w_mix_in[i][0]), j * D, D, chip_idx)
        B[f"w_out_{i}"] = cast_into(f"cast_out_{i}", flat2(w_mix_out[i][0]), j * (D // 4), D // 4, chip_idx)
        B[f"w_up_{i}"] = cast_into(f"cast_up_{i}", flat2(ffn_w_up), i * D, D, chip_idx)
        B[f"w_down_{i}"] = cast_into(f"cast_down_{i}", flat2(ffn_w_down), i * D, D, chip_idx)
    run_comm("allgather_mixer0", comm_gather_all(B, ["w_in_0", "w_out_0"]), B)

    small_w = [gla_w_a1, gla_w_a2, gla_b_a, gla_norm_g]
    gs = allgather_devices("allgather_small_weights", _pack(small_w))
    per_chip = [_unpack(gs[2 * s], [a.shape for a in small_w]) for s in range(N_CHIPS)]
    w_a1 = jnp.concatenate([p[0] for p in per_chip], axis=1)
    w_a2 = jnp.concatenate([p[1] for p in per_chip], axis=2)
    b_a = jnp.concatenate([p[2] for p in per_chip], axis=1)
    gnorm = jnp.concatenate([p[3] for p in per_chip], axis=1)
    w_a1p = jnp.pad(w_a1, ((0, 0), (0, 0), (0, LOW - GATE_RANK))).astype(BF16)
    w_a2p = jnp.pad(w_a2, ((0, 0), (0, LOW - GATE_RANK), (0, 0))).astype(BF16)

    chunk_id = jnp.arange(P) // CHUNK
    mask = chunk_id[None, :] <= chunk_id[:, None]
    wm_all = jnp.where(mask[None, None], gm_w_s, 0.0)
    tri = jnp.tril(jnp.ones((CHUNK, CHUNK), F32))
    triT = tri.T

    def ici(*keys):
        return comm_gather_ici(B, list(keys))

    def fwd(*keys):
        return comm_gather_forward(B, list(keys))

    xs = x[0]
    saved = []
    for i in range(depth):
        j = i // 2
        nxt = i + 1 < depth
        c_in = ici("w_up_0") if i == 0 else comm_merge(fwd(f"w_up_{i}"), ici(f"w_down_{i}"))
        c_out = comm_merge(fwd("w_up_0"), ici("w_down_0")) if i == 0 else fwd(f"w_down_{i}")
        c_up = comm_merge(fwd("w_down_0") if i == 0 else None, ici(f"w_in_{i + 1}", f"w_out_{i + 1}") if nxt else None)
        c_down = comm_merge(fwd(f"w_in_{i + 1}", f"w_out_{i + 1}"), ici(f"w_up_{i + 1}")) if nxt else None
        h1 = rmsnorm_fwd(f"norm_mix_{i}", xs, norm_mix_g[i][None])
        if is_gm[i]:
            zp = mm_fwd_col(f"gm_in_{i}", h1, B[f"w_in_{i}"], 2 * D, comm=c_in, B=B)
            wm = wm_all[j].astype(BF16)
            gated = gm_mid_fwd(f"gm_mid_{i}", zp, gm_ln_g[j][None], gm_ln_b[j][None], wm, gm_b_s[j].T)
            x_mid = mm_fwd_row(f"gm_out_{i}", gated, B[f"w_out_{i}"], xs, comm=c_out, B=B)
            mix = (h1, zp, gated)
        else:
            proj = mm_fwd_col(f"gla_in_{i}", h1, B[f"w_in_{i}"], 3 * D, comm=c_in, B=B)
            lr = mm_plain(f"gla_low_{i}", h1, w_a1p[j], NN, BF16)
            o_raw, states = gla_scan_fwd(f"gla_scan_{i}", proj, lr, w_a2p[j], b_a[j][None], tri)
            og = gla_post_fwd(f"gla_post_{i}", o_raw, proj, gnorm[j][None])
            x_mid = mm_fwd_row(f"gla_out_{i}", og, B[f"w_out_{i}"], xs, comm=c_out, B=B)
            mix = (h1, proj, lr, o_raw, states, og)
        h2 = rmsnorm_fwd(f"norm_ffn_{i}", x_mid, norm_ffn_g[i][None])
        act = mm_fwd_col(f"ffn_up_{i}", h2, B[f"w_up_{i}"], F, epilogue=lambda acc: jnp.maximum(acc, 0.0), comm=c_up, B=B)
        x_out = mm_fwd_row(f"ffn_down_{i}", act, B[f"w_down_{i}"], x_mid, prologue=lambda a: a * a, comm=c_down, B=B)
        saved.append((xs, x_mid, h2, act, mix))
        xs = x_out

    loss_part, dx, dxb, d_final_g = final_loss("final_loss", xs, final_g[None], loss_target[0])
    loss = lax.psum(loss_part[0, 0], ("x", "y", "c"))

    def exchange(i, kinds):
        return comm_exchange(B, [f"d_{k}_{i}" for k in kinds], [f"r_{k}_{i}" for k in kinds])

    def scatter(i, kinds):
        return comm_scatter(B, [f"p_{k}_{i}" for k in kinds], [f"q_{k}_{i}" for k in kinds])

    def join(i, kinds):
        return comm_join(B, [f"f_{k}_{i}" for k in kinds])

    def pair_sums(i, kinds):
        for k in kinds:
            B[f"p_{k}_{i}"] = pair_add(f"pair_add_{k}_{i}", B[f"d_{k}_{i}"], B[f"r_{k}_{i}"], c_idx)

    def chip_sums(i, kinds):
        for k in kinds:
            B[f"f_{k}_{i}"] = sum_chips(f"sum_chips_{k}_{i}", B[f"p_{k}_{i}"], B[f"q_{k}_{i}"], chip_idx, c_idx)

    FFN, MIX = ("up", "down"), ("out", "in")
    d_mix_g, d_ffn_g = [None] * depth, [None] * depth
    d_ln_g, d_ln_b, d_w_s, d_b_s = [None] * n_gm, [None] * n_gm, [None] * n_gm, [None] * n_gm
    d_a1, d_a2, d_ba, d_gn = [None] * n_gla, [None] * n_gla, [None] * n_gla, [None] * n_gla
    for i in reversed(range(depth)):
        j = i // 2
        up = i + 1 < depth
        x_in, x_mid, h2, act, mix = saved[i]
        com = comm_merge(join(i + 1, FFN), exchange(i + 1, MIX)) if up else None
        d_apre = mm_bwd_row_x(f"ffn_down_dx_{i}", dxb, B[f"w_down_{i}"], mul=act, comm=com, B=B)
        if up:
            pair_sums(i + 1, MIX)
        B[f"d_down_{i}"] = mm_bwd_row_w(f"ffn_down_dw_{i}", act, dxb, prologue=lambda a: a * a,
                                        comm=scatter(i + 1, MIX) if up else None, B=B)
        if up:
            chip_sums(i + 1, MIX)
        B[f"d_up_{i}"] = mm_bwd_col_w(f"ffn_up_dw_{i}", h2, d_apre, comm=join(i + 1, MIX) if up else None, B=B)
        dh2 = mm_bwd_col_x(f"ffn_up_dx_{i}", d_apre, B[f"w_up_{i}"])
        dx, dxb, d_ffn_g[i] = rmsnorm_bwd(f"norm_ffn_bwd_{i}", x_mid, norm_ffn_g[i][None], dh2, dx)
        if is_gm[i]:
            h1, zp, gated = mix
            d_gated = mm_bwd_row_x(f"gm_out_dx_{i}", dxb, B[f"w_out_{i}"], comm=exchange(i, FFN), B=B)
            pair_sums(i, FFN)
            B[f"d_out_{i}"] = mm_bwd_row_w(f"gm_out_dw_{i}", gated, dxb)
            wm = wm_all[j].astype(BF16)
            wmT = jnp.swapaxes(wm_all[j], 1, 2).astype(BF16)
            dzp, d_ln_g[j], d_ln_b[j], dw, dbT = gm_mid_bwd(f"gm_mid_bwd_{i}", zp, d_gated, gm_ln_g[j][None], gm_ln_b[j][None],
                                                             wm, wmT, gm_b_s[j].T)
            d_w_s[j] = jnp.where(mask[None], dw, 0.0)
            d_b_s[j] = dbT.T
            B[f"d_in_{i}"] = mm_bwd_col_w(f"gm_in_dw_{i}", h1, dzp, comm=scatter(i, ("up",)), B=B)
            dh1 = mm_bwd_col_x(f"gm_in_dx_{i}", dzp, B[f"w_in_{i}"], comm=scatter(i, ("down",)), B=B)
        else:
            h1, proj, lr, o_raw, states, og = mix
            d_og = mm_bwd_row_x(f"gla_out_dx_{i}", dxb, B[f"w_out_{i}"], comm=exchange(i, FFN), B=B)
            pair_sums(i, FFN)
            B[f"d_out_{i}"] = mm_bwd_row_w(f"gla_out_dw_{i}", og, dxb)
            d_oraw, d_r, d_gn[j] = gla_post_bwd(f"gla_post_bwd_{i}", d_og, o_raw, proj, gnorm[j][None])
            dq, dk, dv, dlogit, d_ba[j] = gla_scan_bwd(f"gla_scan_bwd_{i}", proj, lr, w_a2p[j], b_a[j][None], tri, triT, states, d_oraw)
            dproj = jnp.concatenate([dq, dk, dv, d_r], axis=1)
            B[f"d_in_{i}"] = mm_bwd_col_w(f"gla_in_dw_{i}", h1, dproj, comm=scatter(i, ("up",)), B=B)
            dh1 = mm_bwd_col_x(f"gla_in_dx_{i}", dproj, B[f"w_in_{i}"], comm=scatter(i, ("down",)), B=B)
            dlr = mm_plain(f"gla_gate_dlow_{i}", dlogit, w_a2p[j], NT, BF16)
            d_a2[j] = mm_plain(f"gla_gate_dw2_{i}", lr, dlogit, TN, F32)[:GATE_RANK]
            d_a1[j] = mm_plain(f"gla_gate_dw1_{i}", h1, dlr, TN, F32)[:, :GATE_RANK]
            dh1 = mm_plain(f"gla_gate_dx_{i}", dlr, w_a1p[j], NT, F32, add=dh1)
        chip_sums(i, FFN)
        dx, dxb, d_mix_g[i] = rmsnorm_bwd(f"norm_mix_bwd_{i}", x_in, norm_mix_g[i][None], dh1, dx)
    grad_x = dx[None]

    run_comm("join_ffn0_exchange_mixer0", comm_merge(join(0, FFN), exchange(0, MIX)), B)
    pair_sums(0, MIX)
    run_comm("scatter_mixer0", scatter(0, MIX), B)
    chip_sums(0, MIX)
    run_comm("join_mixer0", join(0, MIX), B)

    small_g = [jnp.concatenate(d_mix_g), jnp.concatenate(d_ffn_g), d_final_g[0], jnp.concatenate(d_ln_g), jnp.concatenate(d_ln_b),
               jnp.stack(d_w_s), jnp.stack(d_b_s), jnp.stack(d_a1), jnp.stack(d_a2), jnp.concatenate(d_ba), jnp.concatenate(d_gn)]
    small_shapes = [(depth, D), (depth, D), (D,), (n_gm, D), (n_gm, D), (n_gm, GM_GROUPS, P, P), (n_gm, GM_GROUPS, P),
                    (n_gla, D, GATE_RANK), (n_gla, GATE_RANK, D // 2), (n_gla, D // 2), (n_gla, D)]
    parts = allgather_devices("allgather_small_grads", _pack(small_g))
    red = _unpack(sum_devices("sum_small_grads", parts), small_shapes)
    g_rep = red[:7]
    g_a1 = lax.dynamic_slice_in_dim(red[7], chip * (D // 4), D // 4, axis=1)
    g_a2 = lax.dynamic_slice_in_dim(red[8], chip * (D // 8), D // 8, axis=2)
    g_ba = lax.dynamic_slice_in_dim(red[9], chip * (D // 8), D // 8, axis=1)
    g_gn = lax.dynamic_slice_in_dim(red[10], chip * (D // 4), D // 4, axis=1)
    g_small = g_rep + [g_a1, g_a2, g_ba, g_gn]
    w_small = [norm_mix_g, norm_ffn_g, final_g, gm_ln_g, gm_ln_b, gm_w_s, gm_b_s, gla_w_a1, gla_w_a2, gla_b_a, gla_norm_g]
    m_small = [m_norm_mix_g, m_norm_ffn_g, m_final_g, m_gm_ln_g, m_gm_ln_b, m_gm_w_s, m_gm_b_s, m_gla_w_a1, m_gla_w_a2, m_gla_b_a, m_gla_norm_g]
    v_small = [v_norm_mix_g, v_norm_ffn_g, v_final_g, v_gm_ln_g, v_gm_ln_b, v_gm_w_s, v_gm_b_s, v_gla_w_a1, v_gla_w_a2, v_gla_b_a, v_gla_norm_g]
    shapes_small = [w.shape for w in w_small]
    sm = adamw("adamw_small", _pack(g_small), _pack(w_small), _pack(m_small), _pack(v_small), 0)
    sm = [_unpack(o, shapes_small) for o in sm]

    res = {}

    def big(key, i, g, wmv, w_off):
        w, m, v = (flat2(t) for t in wmv)
        res[key] = adamw(f"adamw_{key}_{i}", g, w, m, v, w_off, prev=res.get(key))

    for i in reversed(range(depth)):
        j = i // 2
        big("up", i, B[f"f_up_{i}"], w_up, i * D)
        big("down", i, B[f"f_down_{i}"], w_down, i * D)
        big("gm_out" if is_gm[i] else "gla_o", i, B[f"f_out_{i}"], w_mix_out[i], j * (D // 4))
        big("gm_in" if is_gm[i] else "gla_in", i, B[f"f_in_{i}"], w_mix_in[i], j * D)

    def shaped(key, like):
        return [o.reshape(like.shape) for o in res[key]]

    o_gm_in, o_gm_out = shaped("gm_in", gm_w_in), shaped("gm_out", gm_w_out)
    o_gla_in, o_gla_o = shaped("gla_in", gla_w_in), shaped("gla_o", gla_w_o)
    o_up, o_down = shaped("up", ffn_w_up), shaped("down", ffn_w_down)

    def ordered(kind):
        s = sm[kind]
        return [s[0], s[1], s[2], o_gm_in[kind], s[3], s[4], s[5], s[6], o_gm_out[kind], o_gla_in[kind],
                s[7], s[8], s[9], s[10], o_gla_o[kind], o_up[kind], o_down[kind]]

    return (loss, grad_x, *ordered(0), *ordered(1), *ordered(2), *ordered(3))
```

```python
import jax
import jax.numpy as jnp
from jax import lax
from jax.experimental import pallas as pl
from jax.experimental.pallas import tpu as pltpu

F32 = jnp.float32
BF16 = jnp.bfloat16

EPS = 1e-6
CHUNK = 64
GM_BLOCK = 128
GM_GROUPS = 8
GLA_HEADS = 4
GATE_RANK = 16
GATE_TAU = 16.0
LOW = 128
N_CHIPS = 4
N_DEV = 8

ADAM_LR = 0.001
ADAM_B1 = 0.9
ADAM_B2 = 0.999
ADAM_EPS = 1e-08
ADAM_WD = 0.01
ADAM_STEP = 10

V7X_VMEM_LIMIT = 48 * 1024 * 1024
LANES = 128
SUBLANES = 8
BF16_ROWS = 16
MESH = pl.DeviceIdType.MESH
HIGHEST = lax.Precision.HIGHEST


def _pick(n, cap):
    if n <= cap:
        return n
    best = LANES
    for t in range(LANES, cap + 1, LANES):
        if n % t == 0:
            best = t
    return best


def _rows(n, cap):
    if n <= cap:
        return n
    best = 0
    for t in range(BF16_ROWS, cap + 1, BF16_ROWS):
        if n % t == 0:
            best = t
    return best if best >= LANES else n


def _cparams(sem=None):
    return pltpu.CompilerParams(dimension_semantics=sem, vmem_limit_bytes=V7X_VMEM_LIMIT)


ANY = pl.BlockSpec(memory_space=pl.ANY)


def _sds(shape, dtype):
    return jax.ShapeDtypeStruct(shape, dtype)


def _place():
    return lax.axis_index("x"), lax.axis_index("y"), lax.axis_index("c")


def _other_chips(x, y):
    return [(1 - x, y), (x, 1 - y), (1 - x, 1 - y)]


class Comm:
    def __init__(self, ins, outs, alias, n_sems, start, wait):
        self.ins, self.outs, self.alias, self.n_sems, self.start, self.wait = ins, outs, alias, n_sems, start, wait


class _Shift:
    def __init__(self, sems, by):
        self.sems, self.by = sems, by

    @property
    def at(self):
        return self

    def __getitem__(self, k):
        return self.sems.at[self.by + k]


def comm_merge(*comms):
    comms = [c for c in comms if c is not None]
    if not comms:
        return None
    if len(comms) == 1:
        return comms[0]
    offs, total = [], 0
    for c in comms:
        offs.append(total)
        total += c.n_sems

    def start(R, ss, rs):
        for c, o in zip(comms, offs):
            c.start(R, _Shift(ss, o), _Shift(rs, o))

    def wait(R, ss, rs):
        for c, o in zip(comms, offs):
            c.wait(R, _Shift(ss, o), _Shift(rs, o))

    return Comm(sum((c.ins for c in comms), []), sum((c.outs for c in comms), []), sum((c.alias for c in comms), []),
                total, start, wait)


def _remote(src, dst, ss, rs, k, to):
    return pltpu.make_async_remote_copy(src_ref=src, dst_ref=dst, send_sem=ss.at[k], recv_sem=rs.at[k],
                                        device_id=to, device_id_type=MESH)


def comm_gather_ici(B, keys):
    hr = {k: B[k].shape[1] // 2 for k in keys}

    def region(R, k, chip, hc):
        return R[k].at[chip, pl.ds(hc * hr[k], hr[k]), :]

    def start(R, ss, rs):
        x, y, c = _place()
        for a, k in enumerate(keys):
            for j, (px, py) in enumerate(_other_chips(x, y)):
                mine = region(R, k, 2 * x + y, c)
                _remote(mine, mine, ss, rs, 3 * a + j, (px, py, c)).start()

    def wait(R, ss, rs):
        x, y, c = _place()
        for a, k in enumerate(keys):
            for j, (px, py) in enumerate(_other_chips(x, y)):
                theirs = region(R, k, 2 * px + py, c)
                _remote(theirs, theirs, ss, rs, 3 * a + j, (px, py, c)).wait_recv()
                mine = region(R, k, 2 * x + y, c)
                _remote(mine, mine, ss, rs, 3 * a + j, (px, py, c)).wait_send()

    return Comm([], [], list(keys), 3 * len(keys), start, wait)


def comm_gather_forward(B, keys):
    hr = {k: B[k].shape[1] // 2 for k in keys}

    def region(R, k, chip, hc):
        return R[k].at[chip, pl.ds(hc * hr[k], hr[k]), :]

    def start(R, ss, rs):
        x, y, c = _place()
        for a, k in enumerate(keys):
            for j, (px, py) in enumerate(_other_chips(x, y)):
                got = region(R, k, 2 * px + py, c)
                _remote(got, got, ss, rs, 3 * a + j, (x, y, 1 - c)).start()

    def wait(R, ss, rs):
        x, y, c = _place()
        for a, k in enumerate(keys):
            for j, (px, py) in enumerate(_other_chips(x, y)):
                other = region(R, k, 2 * px + py, 1 - c)
                _remote(other, other, ss, rs, 3 * a + j, (x, y, 1 - c)).wait_recv()
                got = region(R, k, 2 * px + py, c)
                _remote(got, got, ss, rs, 3 * a + j, (x, y, 1 - c)).wait_send()

    return Comm([], [], list(keys), 3 * len(keys), start, wait)


def comm_gather_all(B, keys):
    ici, fwd = comm_gather_ici(B, keys), comm_gather_forward(B, keys)
    n1 = ici.n_sems

    def wait(R, ss, rs):
        ici.wait(R, ss, rs)
        fwd.start(R, _Shift(ss, n1), _Shift(rs, n1))
        fwd.wait(R, _Shift(ss, n1), _Shift(rs, n1))

    return Comm([], [], list(keys), n1 + fwd.n_sems, ici.start, wait)


def comm_exchange(B, src_keys, out_keys):
    hr = {k: B[k].shape[1] // 2 for k in src_keys}

    def descr(R, ss, rs, a):
        x, y, c = _place()
        k = src_keys[a]
        return _remote(R[k].at[:, pl.ds((1 - c) * hr[k], hr[k]), :], R[out_keys[a]], ss, rs, a, (x, y, 1 - c))

    def start(R, ss, rs):
        for a in range(len(src_keys)):
            descr(R, ss, rs, a).start()

    def wait(R, ss, rs):
        for a in range(len(src_keys)):
            descr(R, ss, rs, a).wait()

    outs = [(o, _sds((N_CHIPS, hr[k], B[k].shape[2]), B[k].dtype)) for k, o in zip(src_keys, out_keys)]
    return Comm(list(src_keys), outs, [], len(src_keys), start, wait)


def comm_scatter(B, p_keys, q_keys):
    def each(R, ss, rs, fn):
        x, y, c = _place()
        for a, (pk, qk) in enumerate(zip(p_keys, q_keys)):
            for j, (px, py) in enumerate(_other_chips(x, y)):
                fn(_remote(R[pk].at[2 * px + py], R[qk].at[j], ss, rs, 3 * a + j, (px, py, c)))

    def start(R, ss, rs):
        each(R, ss, rs, lambda d: d.start())

    def wait(R, ss, rs):
        each(R, ss, rs, lambda d: d.wait())

    outs = [(qk, _sds((3,) + B[pk].shape[1:], B[pk].dtype)) for pk, qk in zip(p_keys, q_keys)]
    return Comm(list(p_keys), outs, [], 3 * len(p_keys), start, wait)


def comm_join(B, keys):
    hr = {k: B[k].shape[0] // 2 for k in keys}

    def region(R, k, hc):
        return R[k].at[pl.ds(hc * hr[k], hr[k]), :]

    def start(R, ss, rs):
        x, y, c = _place()
        for a, k in enumerate(keys):
            mine = region(R, k, c)
            _remote(mine, mine, ss, rs, a, (x, y, 1 - c)).start()

    def wait(R, ss, rs):
        x, y, c = _place()
        for a, k in enumerate(keys):
            other = region(R, k, 1 - c)
            _remote(other, other, ss, rs, a, (x, y, 1 - c)).wait_recv()
            mine = region(R, k, c)
            _remote(mine, mine, ss, rs, a, (x, y, 1 - c)).wait_send()

    return Comm([], [], list(keys), len(keys), start, wait)


def comm_allgather_rows(B, src_key, out_key, r0, nr, fresh):
    def peer(x, y, c, k):
        return (1 - x if k & 4 else x, 1 - y if k & 2 else y, 1 - c if k & 1 else c)

    def start(R, ss, rs):
        x, y, c = _place()
        me = 4 * x + 2 * y + c
        src = R[src_key].at[pl.ds(r0, nr), :]
        dst = R[out_key].at[me, pl.ds(r0, nr), :]
        pltpu.make_async_copy(src, dst, ss.at[N_DEV - 1]).start()
        for k in range(1, N_DEV):
            _remote(src, dst, ss, rs, k - 1, peer(x, y, c, k)).start()

    def wait(R, ss, rs):
        x, y, c = _place()
        me = 4 * x + 2 * y + c
        src = R[src_key].at[pl.ds(r0, nr), :]
        for k in range(1, N_DEV):
            px, py, pc = peer(x, y, c, k)
            theirs = R[out_key].at[4 * px + 2 * py + pc, pl.ds(r0, nr), :]
            _remote(theirs, theirs, ss, rs, k - 1, (px, py, pc)).wait_recv()
            _remote(src, R[out_key].at[me, pl.ds(r0, nr), :], ss, rs, k - 1, (px, py, pc)).wait_send()
        pltpu.make_async_copy(src, R[out_key].at[me, pl.ds(r0, nr), :], ss.at[N_DEV - 1]).wait()

    outs = [(out_key, _sds((N_DEV,) + B[src_key].shape, B[src_key].dtype))] if fresh else []
    return Comm([src_key], outs, [] if fresh else [out_key], N_DEV, start, wait)


def run_comm(name, comm, B):
    n_in, n_out, n_al = len(comm.ins), len(comm.outs), len(comm.alias)

    def body(*refs):
        R = dict(zip(comm.ins, refs[:n_in]))
        R.update(zip([k for k, _ in comm.outs], refs[n_in + n_al:n_in + n_al + n_out]))
        R.update(zip(comm.alias, refs[n_in + n_al + n_out:n_in + n_al + n_out + n_al]))
        ss, rs = refs[-2], refs[-1]
        comm.start(R, ss, rs)
        comm.wait(R, ss, rs)

    res = pl.pallas_call(
        body, name=name, in_specs=[ANY] * (n_in + n_al), out_specs=[ANY] * (n_out + n_al),
        out_shape=[s for _, s in comm.outs] + [_sds(B[k].shape, B[k].dtype) for k in comm.alias],
        input_output_aliases={n_in + t: n_out + t for t in range(n_al)},
        scratch_shapes=[pltpu.SemaphoreType.DMA((comm.n_sems,)), pltpu.SemaphoreType.DMA((comm.n_sems,))],
    )(*[B[k] for k in comm.ins], *[B[k] for k in comm.alias])
    B.update(zip([k for k, _ in comm.outs] + list(comm.alias), res))


def allgather_devices(name, v):
    def body(v_ref, out_ref, send_sems, recv_sems, local_sem):
        x, y, c = _place()
        me = 4 * x + 2 * y + c
        mine = pltpu.make_async_copy(v_ref, out_ref.at[me], local_sem)
        mine.start()

        def peer(k):
            return (1 - x if k & 4 else x, 1 - y if k & 2 else y, 1 - c if k & 1 else c)

        def copy(k, src, dst):
            return pltpu.make_async_remote_copy(src_ref=src, dst_ref=dst, send_sem=send_sems.at[k - 1],
                                                recv_sem=recv_sems.at[k - 1], device_id=peer(k), device_id_type=MESH)

        sends = [copy(k, v_ref, out_ref.at[me]) for k in range(1, N_DEV)]
        for cp in sends:
            cp.start()
        for k in range(1, N_DEV):
            px, py, pc = peer(k)
            them = 4 * px + 2 * py + pc
            copy(k, out_ref.at[them], out_ref.at[them]).wait_recv()
        for cp in sends:
            cp.wait_send()
        mine.wait()

    return pl.pallas_call(
        body, name=name, in_specs=[ANY], out_specs=ANY, out_shape=_sds((N_DEV,) + v.shape, v.dtype),
        scratch_shapes=[pltpu.SemaphoreType.DMA((N_DEV - 1,)), pltpu.SemaphoreType.DMA((N_DEV - 1,)), pltpu.SemaphoreType.DMA],
    )(v)


NN = ((1,), (0,))
NT = ((1,), (1,))
TN = ((0,), (0,))


def _mm(name, a, b, *, dims, grid, a_spec, b_spec, out_spec, out_shape, acc_shape, extra=(), extra_specs=(),
        prologue=None, epilogue=None, b_reshape=None, dot_fn=None, comm=None, B=None):
    gi, gj, nk = grid
    n_extra = len(extra)
    n_in, n_out, n_al = (len(comm.ins), len(comm.outs), len(comm.alias)) if comm is not None else (0, 0, 0)

    def body(*refs):
        a_ref, b_ref = refs[0], refs[1]
        ex = refs[2:2 + n_extra]
        pos = 2 + n_extra
        c_ins = refs[pos:pos + n_in]
        pos += n_in + n_al
        o_ref = refs[pos]
        c_outs = refs[pos + 1:pos + 1 + n_out]
        c_alias = refs[pos + 1 + n_out:pos + 1 + n_out + n_al]
        scratch = refs[pos + 1 + n_out + n_al:]
        i, j, k = pl.program_id(0), pl.program_id(1), pl.program_id(2)
        if comm is not None:
            ss, rs = scratch[-2], scratch[-1]
            R = dict(zip(comm.ins, c_ins))
            R.update(zip([key for key, _ in comm.outs], c_outs))
            R.update(zip(comm.alias, c_alias))

            @pl.when(jnp.logical_and(jnp.logical_and(i == 0, j == 0), k == 0))
            def _():
                comm.start(R, ss, rs)

        av = a_ref[...]
        if prologue is not None:
            av = prologue(av)
        if dot_fn is not None:
            p = dot_fn(av, b_ref)
        else:
            bv = b_ref[...]
            if b_reshape is not None:
                bv = bv.reshape(b_reshape)
            p = lax.dot_general(av, bv, (dims, ((), ())), preferred_element_type=F32)

        def finish(acc):
            r = acc if epilogue is None else epilogue(acc, *ex)
            o_ref[...] = r.astype(o_ref.dtype)

        if nk == 1:
            finish(p)
        else:
            acc_ref = scratch[0]

            @pl.when(k == 0)
            def _():
                acc_ref[...] = jnp.zeros_like(acc_ref)

            acc_ref[...] += p

            @pl.when(k == nk - 1)
            def _():
                finish(acc_ref[...])

        if comm is not None:
            @pl.when(jnp.logical_and(jnp.logical_and(i == gi - 1, j == gj - 1), k == nk - 1))
            def _():
                comm.wait(R, ss, rs)

    inputs = [a, b, *extra]
    in_specs = [a_spec, b_spec, *extra_specs]
    scratch_shapes = [pltpu.VMEM(acc_shape, F32)] if nk > 1 else []
    if comm is None:
        return pl.pallas_call(
            body, name=name, grid=grid, in_specs=in_specs, out_specs=out_spec, out_shape=out_shape,
            scratch_shapes=scratch_shapes, compiler_params=_cparams(("parallel", "parallel", "arbitrary")),
        )(*inputs)
    aliases = {len(inputs) + n_in + t: 1 + n_out + t for t in range(n_al)}
    inputs += [B[key] for key in comm.ins] + [B[key] for key in comm.alias]
    in_specs += [ANY] * (n_in + n_al)
    res = pl.pallas_call(
        body, name=name, grid=grid, in_specs=in_specs, out_specs=[out_spec] + [ANY] * (n_out + n_al),
        out_shape=[out_shape] + [s for _, s in comm.outs] + [_sds(B[key].shape, B[key].dtype) for key in comm.alias],
        scratch_shapes=scratch_shapes + [pltpu.SemaphoreType.DMA((comm.n_sems,)), pltpu.SemaphoreType.DMA((comm.n_sems,))],
        input_output_aliases=aliases, compiler_params=_cparams(("arbitrary", "arbitrary", "arbitrary")),
    )(*inputs)
    B.update(zip([key for key, _ in comm.outs] + list(comm.alias), res[1:]))
    return res[0]


def mm_fwd_col(name, h, wg, n_out, epilogue=None, out_dtype=BF16, **kw):
    S, D = h.shape
    W = wg.shape[2]
    tm, tn = min(1024, S), _pick(W, 1024)
    wps = W // tn
    return _mm(name, h, wg, dims=NN, grid=(S // tm, n_out // tn, 1),
               a_spec=pl.BlockSpec((tm, D), lambda i, j, k: (i, 0)),
               b_spec=pl.BlockSpec((None, D, tn), lambda i, j, k: (j // wps, 0, j % wps)),
               out_spec=pl.BlockSpec((tm, tn), lambda i, j, k: (i, j)),
               out_shape=_sds((S, n_out), out_dtype), acc_shape=(tm, tn), epilogue=epilogue, **kw)


def mm_fwd_row(name, a, wg, res, prologue=None, **kw):
    S, K = a.shape
    ksh, D = wg.shape[1], wg.shape[2]
    t = 1024 if K <= 2048 else 512
    tm, tn = min(t, S), min(t, D)
    res_spec = pl.BlockSpec((tm, tn), lambda i, j, k: (i, j))
    return _mm(name, a, wg, dims=NN, grid=(S // tm, D // tn, 1),
               a_spec=pl.BlockSpec((tm, K), lambda i, j, k: (i, 0)),
               b_spec=pl.BlockSpec((N_CHIPS, ksh, tn), lambda i, j, k: (0, 0, j)),
               out_spec=res_spec, out_shape=_sds((S, D), F32), acc_shape=(tm, tn),
               extra=(res,), extra_specs=(res_spec,), prologue=prologue, epilogue=lambda acc, r: acc + r[...],
               b_reshape=(K, tn), **kw)


def mm_bwd_col_x(name, dz, wg, add=None, **kw):
    S, N = dz.shape
    D, W = wg.shape[1], wg.shape[2]
    t = 1024 if N <= 4096 else 512
    tm, to = min(t, S), min(t, D)
    extra, extra_specs, epi = (), (), None
    if add is not None:
        extra, extra_specs = (add,), (pl.BlockSpec((tm, to), lambda i, j, k: (i, j)),)
        epi = lambda acc, r: acc + r[...]

    def dot_fn(av, b_ref):
        p = lax.dot_general(av[:, :W], b_ref[0], (NT, ((), ())), preferred_element_type=F32)
        for s in range(1, N_CHIPS):
            p = p + lax.dot_general(av[:, s * W:(s + 1) * W], b_ref[s], (NT, ((), ())), preferred_element_type=F32)
        return p

    return _mm(name, dz, wg, dims=NT, grid=(S // tm, D // to, 1),
               a_spec=pl.BlockSpec((tm, N), lambda i, j, k: (i, 0)),
               b_spec=pl.BlockSpec((N_CHIPS, to, W), lambda i, j, k: (0, j, 0)),
               out_spec=pl.BlockSpec((tm, to), lambda i, j, k: (i, j)),
               out_shape=_sds((S, D), F32), acc_shape=(tm, to), extra=extra, extra_specs=extra_specs, epilogue=epi,
               dot_fn=dot_fn, **kw)


def mm_bwd_row_x(name, dxb, wg, mul=None, **kw):
    S, D = dxb.shape
    ksh = wg.shape[1]
    tm, tn = min(1024, S), min(1024, ksh)
    kps = ksh // tn
    extra, extra_specs, epi = (), (), None
    if mul is not None:
        extra, extra_specs = (mul,), (pl.BlockSpec((tm, tn), lambda i, j, k: (i, j)),)
        epi = lambda acc, r: acc * (2.0 * r[...].astype(F32))
    return _mm(name, dxb, wg, dims=NT, grid=(S // tm, 4 * kps, 1),
               a_spec=pl.BlockSpec((tm, D), lambda i, j, k: (i, 0)),
               b_spec=pl.BlockSpec((None, tn, D), lambda i, j, k: (j // kps, j % kps, 0)),
               out_spec=pl.BlockSpec((tm, tn), lambda i, j, k: (i, j)),
               out_shape=_sds((S, 4 * ksh), BF16), acc_shape=(tm, tn), extra=extra, extra_specs=extra_specs, epilogue=epi, **kw)


def mm_bwd_col_w(name, h, dz, **kw):
    S, D = h.shape
    N = dz.shape[1]
    W = N // N_CHIPS
    tk, tn, ts = min(512, D), _pick(W, 512), S
    wps = W // tn
    return _mm(name, h, dz, dims=TN, grid=(D // tk, N // tn, S // ts),
               a_spec=pl.BlockSpec((ts, tk), lambda i, j, k: (k, i)),
               b_spec=pl.BlockSpec((ts, tn), lambda i, j, k: (k, j)),
               out_spec=pl.BlockSpec((None, tk, tn), lambda i, j, k: (j // wps, i, j % wps)),
               out_shape=_sds((N_CHIPS, D, W), BF16), acc_shape=(tk, tn), **kw)


def mm_bwd_row_w(name, a, dxb, prologue=None, **kw):
    S, K = a.shape
    D = dxb.shape[1]
    ksh = K // N_CHIPS
    tk, tn, ts = min(512, ksh), min(512, D), S
    kps = ksh // tk
    return _mm(name, a, dxb, dims=TN, grid=(K // tk, D // tn, S // ts),
               a_spec=pl.BlockSpec((ts, tk), lambda i, j, k: (k, i)),
               b_spec=pl.BlockSpec((ts, tn), lambda i, j, k: (k, j)),
               out_spec=pl.BlockSpec((None, tk, tn), lambda i, j, k: (i // kps, i % kps, j)),
               out_shape=_sds((N_CHIPS, ksh, D), BF16), acc_shape=(tk, tn), prologue=prologue, **kw)


def mm_plain(name, a, b, dims, out_dtype, add=None):
    if dims == NN:
        M, N = a.shape[0], b.shape[1]
    elif dims == NT:
        M, N = a.shape[0], b.shape[0]
    else:
        M, N = a.shape[1], b.shape[1]
    red = a.shape[0] if dims == TN else a.shape[1]
    tm, tn = min(1024, M), min(1024, N)
    tr = min(1024, red) if dims == TN else red
    nk = red // tr
    if dims == TN:
        a_spec = pl.BlockSpec((tr, tm), lambda i, j, k: (k, i))
        b_spec = pl.BlockSpec((tr, tn), lambda i, j, k: (k, j))
    elif dims == NN:
        a_spec = pl.BlockSpec((tm, tr), lambda i, j, k: (i, k))
        b_spec = pl.BlockSpec((tr, tn), lambda i, j, k: (k, j))
    else:
        a_spec = pl.BlockSpec((tm, tr), lambda i, j, k: (i, k))
        b_spec = pl.BlockSpec((tn, tr), lambda i, j, k: (j, k))
    extra, extra_specs, epi = (), (), None
    if add is not None:
        extra, extra_specs = (add,), (pl.BlockSpec((tm, tn), lambda i, j, k: (i, j)),)
        epi = lambda acc, r: acc + r[...]
    return _mm(name, a, b, dims=dims, grid=(M // tm, N // tn, nk), a_spec=a_spec, b_spec=b_spec,
               out_spec=pl.BlockSpec((tm, tn), lambda i, j, k: (i, j)),
               out_shape=_sds((M, N), out_dtype), acc_shape=(tm, tn), extra=extra, extra_specs=extra_specs, epilogue=epi)


ROWS = 256


def rmsnorm_fwd(name, x, g):
    S, D = x.shape
    tr = min(ROWS, S)

    def body(x_ref, g_ref, h_ref):
        xv = x_ref[...]
        rstd = lax.rsqrt(jnp.mean(xv * xv, axis=-1, keepdims=True) + EPS)
        h_ref[...] = (xv * rstd * g_ref[...]).astype(BF16)

    return pl.pallas_call(
        body, name=name, grid=(S // tr,),
        in_specs=[pl.BlockSpec((tr, D), lambda i: (i, 0)), pl.BlockSpec((1, D), lambda i: (0, 0))],
        out_specs=pl.BlockSpec((tr, D), lambda i: (i, 0)), out_shape=_sds((S, D), BF16),
        compiler_params=_cparams(("parallel",)),
    )(x, g)


def rmsnorm_bwd(name, x, g, dh, dres):
    S, D = x.shape
    tr = min(ROWS, S)

    def body(x_ref, g_ref, dh_ref, dres_ref, dx_ref, dxb_ref, dg_ref):
        xv = x_ref[...]
        rstd = lax.rsqrt(jnp.mean(xv * xv, axis=-1, keepdims=True) + EPS)
        xh = xv * rstd
        dy = dh_ref[...]
        dxh = dy * g_ref[...]
        dx = dres_ref[...] + rstd * (dxh - xh * jnp.mean(dxh * xh, axis=-1, keepdims=True))
        dx_ref[...] = dx
        dxb_ref[...] = dx.astype(BF16)

        @pl.when(pl.program_id(0) == 0)
        def _():
            dg_ref[...] = jnp.zeros_like(dg_ref)

        dg_ref[...] += jnp.sum(dy * xh, axis=0, keepdims=True)

    row = pl.BlockSpec((tr, D), lambda i: (i, 0))
    vec = pl.BlockSpec((1, D), lambda i: (0, 0))
    return pl.pallas_call(
        body, name=name, grid=(S // tr,), in_specs=[row, vec, row, row], out_specs=[row, row, vec],
        out_shape=[_sds((S, D), F32), _sds((S, D), BF16), _sds((1, D), F32)],
        compiler_params=_cparams(("arbitrary",)),
    )(x, g, dh, dres)


def final_loss(name, x, g, target):
    S, D = x.shape
    tr = min(ROWS, S)

    def body(x_ref, g_ref, t_ref, loss_ref, dx_ref, dxb_ref, dg_ref):
        xv = x_ref[...]
        gv = g_ref[...]
        rstd = lax.rsqrt(jnp.mean(xv * xv, axis=-1, keepdims=True) + EPS)
        xh = xv * rstd
        err = xh * gv - t_ref[...]
        dy = err * (1.0 / D)
        dxh = dy * gv
        dx = rstd * (dxh - xh * jnp.mean(dxh * xh, axis=-1, keepdims=True))
        dx_ref[...] = dx
        dxb_ref[...] = dx.astype(BF16)

        @pl.when(pl.program_id(0) == 0)
        def _():
            dg_ref[...] = jnp.zeros_like(dg_ref)
            loss_ref[...] = jnp.zeros_like(loss_ref)

        dg_ref[...] += jnp.sum(dy * xh, axis=0, keepdims=True)
        loss_ref[...] += 0.5 * jnp.sum(jnp.mean(err * err, axis=-1, keepdims=True))

    row = pl.BlockSpec((tr, D), lambda i: (i, 0))
    vec = pl.BlockSpec((1, D), lambda i: (0, 0))
    return pl.pallas_call(
        body, name=name, grid=(S // tr,), in_specs=[row, vec, row],
        out_specs=[pl.BlockSpec((SUBLANES, LANES), lambda i: (0, 0)), row, row, vec],
        out_shape=[_sds((SUBLANES, LANES), F32), _sds((S, D), F32), _sds((S, D), BF16), _sds((1, D), F32)],
        compiler_params=_cparams(("arbitrary",)),
    )(x, g, target)


def _gelu(x):
    return 0.5 * x * (1.0 + lax.erf(x * 0.7071067811865476))


def _gelu_grad(x):
    return 0.5 * (1.0 + lax.erf(x * 0.7071067811865476)) + x * jnp.exp(-0.5 * x * x) * 0.3989422804014327


def _gm_common(zp, lng, lnb, D):
    z = _gelu(zp)
    u, v = z[:, :D], z[:, D:]
    xc = v - jnp.mean(v, axis=-1, keepdims=True)
    rstd = lax.rsqrt(jnp.mean(xc * xc, axis=-1, keepdims=True) + EPS)
    xh = xc * rstd
    return u, xh, rstd, xh * lng + lnb


def gm_mid_fwd(name, zp, lng, lnb, wm, bT):
    S, D2 = zp.shape
    D = D2 // 2
    dg = D // GM_GROUPS
    P = GM_BLOCK

    def body(z_ref, lng_ref, lnb_ref, wm_ref, bT_ref, o_ref):
        u, _, _, vn = _gm_common(z_ref[...].astype(F32), lng_ref[...], lnb_ref[...], D)
        vnb = vn.astype(BF16)
        for gi in range(GM_GROUPS):
            cols = slice(gi * dg, (gi + 1) * dg)
            mixed = jnp.dot(wm_ref[gi], vnb[:, cols], preferred_element_type=F32) + bT_ref[:, gi:gi + 1]
            o_ref[:, cols] = (u[:, cols] * mixed).astype(BF16)

    vec = pl.BlockSpec((1, D), lambda i: (0, 0))
    return pl.pallas_call(
        body, name=name, grid=(S // P,),
        in_specs=[pl.BlockSpec((P, D2), lambda i: (i, 0)), vec, vec,
                  pl.BlockSpec((GM_GROUPS, P, P), lambda i: (0, 0, 0)), pl.BlockSpec((P, GM_GROUPS), lambda i: (0, 0))],
        out_specs=pl.BlockSpec((P, D), lambda i: (i, 0)), out_shape=_sds((S, D), BF16),
        compiler_params=_cparams(("parallel",)),
    )(zp, lng, lnb, wm, bT)


def gm_mid_bwd(name, zp, dgated, lng, lnb, wm, wmT, bT):
    S, D2 = zp.shape
    D = D2 // 2
    dg = D // GM_GROUPS
    P = GM_BLOCK

    def body(z_ref, dgt_ref, lng_ref, lnb_ref, wm_ref, wmT_ref, bT_ref, dz_ref, dlng_ref, dlnb_ref, dw_ref, dbT_ref, dvn_ref):
        @pl.when(pl.program_id(0) == 0)
        def _():
            dlng_ref[...] = jnp.zeros_like(dlng_ref)
            dlnb_ref[...] = jnp.zeros_like(dlnb_ref)
            dw_ref[...] = jnp.zeros_like(dw_ref)
            dbT_ref[...] = jnp.zeros_like(dbT_ref)

        zp_v = z_ref[...].astype(F32)
        lng_v = lng_ref[...]
        u, xh, rstd, vn = _gm_common(zp_v, lng_v, lnb_ref[...], D)
        vnb = vn.astype(BF16)
        dgt = dgt_ref[...].astype(F32)
        for gi in range(GM_GROUPS):
            cols = slice(gi * dg, (gi + 1) * dg)
            mixed = jnp.dot(wm_ref[gi], vnb[:, cols], preferred_element_type=F32) + bT_ref[:, gi:gi + 1]
            dm = dgt[:, cols] * u[:, cols]
            dmb = dm.astype(BF16)
            dz_ref[:, cols] = (dgt[:, cols] * mixed * _gelu_grad(zp_v[:, cols])).astype(BF16)
            dbT_ref[:, gi:gi + 1] += jnp.sum(dm, axis=1, keepdims=True)
            dw_ref[gi] += lax.dot_general(dmb, vnb[:, cols], (NT, ((), ())), preferred_element_type=F32)
            dvn_ref[:, cols] = jnp.dot(wmT_ref[gi], dmb, preferred_element_type=F32)
        dvn = dvn_ref[...]
        dlng_ref[...] += jnp.sum(dvn * xh, axis=0, keepdims=True)
        dlnb_ref[...] += jnp.sum(dvn, axis=0, keepdims=True)
        dyg = dvn * lng_v
        dv = rstd * (dyg - jnp.mean(dyg, axis=-1, keepdims=True) - xh * jnp.mean(dyg * xh, axis=-1, keepdims=True))
        dz_ref[:, D:] = (dv * _gelu_grad(zp_v[:, D:])).astype(BF16)

    vec = pl.BlockSpec((1, D), lambda i: (0, 0))
    wsp = pl.BlockSpec((GM_GROUPS, P, P), lambda i: (0, 0, 0))
    bsp = pl.BlockSpec((P, GM_GROUPS), lambda i: (0, 0))
    return pl.pallas_call(
        body, name=name, grid=(S // P,),
        in_specs=[pl.BlockSpec((P, D2), lambda i: (i, 0)), pl.BlockSpec((P, D), lambda i: (i, 0)), vec, vec, wsp, wsp, bsp],
        out_specs=[pl.BlockSpec((P, D2), lambda i: (i, 0)), vec, vec, wsp, bsp],
        out_shape=[_sds((S, D2), BF16), _sds((1, D), F32), _sds((1, D), F32), _sds((GM_GROUPS, P, P), F32), _sds((P, GM_GROUPS), F32)],
        scratch_shapes=[pltpu.VMEM((P, D), F32)],
        compiler_params=_cparams(("arbitrary",)),
    )(zp, dgated, lng, lnb, wm, wmT, bT)


def _gla_gate(lr, w2, ba, tri):
    logit = jnp.dot(lr, w2, preferred_element_type=F32) + ba
    la = (jnp.minimum(logit, 0.0) - jnp.log1p(jnp.exp(-jnp.abs(logit)))) * (1.0 / GATE_TAU)
    g = jnp.dot(tri, la, preferred_element_type=F32, precision=HIGHEST)
    return logit, g


def gla_scan_fwd(name, proj, lr, w2p, ba, tri):
    S, D3 = proj.shape
    D = D3 // 3
    H, C = GLA_HEADS, CHUNK
    dk, dv = D // 2 // H, D // H
    NC = S // C
    scale = dk ** -0.5

    def body(q_ref, k_ref, v_ref, lr_ref, w2_ref, ba_ref, tri_ref, o_ref, st_ref, state):
        @pl.when(pl.program_id(0) == 0)
        def _():
            state[...] = jnp.zeros_like(state)

        _, g = _gla_gate(lr_ref[...], w2_ref[...], ba_ref[...], tri_ref[...])
        gend = g[C - 1:C, :]
        kdec = (k_ref[...].astype(F32) * jnp.exp(gend - g)).astype(BF16)
        dec = jnp.exp(gend)
        qs = (q_ref[...].astype(F32) * scale).astype(BF16)
        for hh in range(H):
            kc, vc = slice(hh * dk, (hh + 1) * dk), slice(hh * dv, (hh + 1) * dv)
            kv = lax.dot_general(v_ref[:, vc], kdec[:, kc], (TN, ((), ())), preferred_element_type=F32)
            new = dec[:, kc] * state[hh] + kv
            state[hh] = new
            nb = new.astype(BF16)
            st_ref[hh] = nb
            o_ref[:, vc] = lax.dot_general(qs[:, kc], nb, (NT, ((), ())), preferred_element_type=F32).astype(BF16)

    return pl.pallas_call(
        body, name=name, grid=(NC,),
        in_specs=[pl.BlockSpec((C, D // 2), lambda t: (t, 0)), pl.BlockSpec((C, D // 2), lambda t: (t, 1)),
                  pl.BlockSpec((C, D), lambda t: (t, 1)), pl.BlockSpec((C, LOW), lambda t: (t, 0)),
                  pl.BlockSpec((LOW, D // 2), lambda t: (0, 0)), pl.BlockSpec((1, D // 2), lambda t: (0, 0)),
                  pl.BlockSpec((C, C), lambda t: (0, 0))],
        out_specs=[pl.BlockSpec((C, D), lambda t: (t, 0)), pl.BlockSpec((None, H, dv, dk), lambda t: (t, 0, 0, 0))],
        out_shape=[_sds((S, D), BF16), _sds((NC, H, dv, dk), BF16)],
        scratch_shapes=[pltpu.VMEM((H, dv, dk), F32)],
        compiler_params=_cparams(("arbitrary",)),
    )(proj, proj, proj, lr, w2p, ba, tri)


def gla_scan_bwd(name, proj, lr, w2p, ba, tri, triT, states, do):
    S, D3 = proj.shape
    D = D3 // 3
    H, C = GLA_HEADS, CHUNK
    dk, dv = D // 2 // H, D // H
    NC = S // C
    scale = dk ** -0.5

    def body(q_ref, k_ref, v_ref, lr_ref, w2_ref, ba_ref, tri_ref, triT_ref, st_ref, sp_ref, do_ref,
             dq_ref, dk_ref, dv_ref, dl_ref, dba_ref, dstate, dkd_ref, ddec_ref):
        t = pl.program_id(0)

        @pl.when(t == 0)
        def _():
            dstate[...] = jnp.zeros_like(dstate)
            dba_ref[...] = jnp.zeros_like(dba_ref)

        logit, g = _gla_gate(lr_ref[...], w2_ref[...], ba_ref[...], tri_ref[...])
        gend = g[C - 1:C, :]
        e = jnp.exp(gend - g)
        kf = k_ref[...].astype(F32)
        kdec = (kf * e).astype(BF16)
        dec = jnp.exp(gend)
        qs = (q_ref[...].astype(F32) * scale).astype(BF16)
        has_prev = (t < NC - 1).astype(F32)
        for hh in range(H):
            kc, vc = slice(hh * dk, (hh + 1) * dk), slice(hh * dv, (hh + 1) * dv)
            dob = do_ref[:, vc]
            dq_ref[:, kc] = (jnp.dot(dob, st_ref[hh], preferred_element_type=F32) * scale).astype(BF16)
            ds = dstate[hh] + lax.dot_general(dob, qs[:, kc], (TN, ((), ())), preferred_element_type=F32)
            dsb = ds.astype(BF16)
            dkd_ref[:, kc] = jnp.dot(v_ref[:, vc], dsb, preferred_element_type=F32)
            dv_ref[:, vc] = lax.dot_general(kdec[:, kc], dsb, (NT, ((), ())), preferred_element_type=F32).astype(BF16)
            ddec_ref[:, kc] = jnp.sum(ds * sp_ref[hh].astype(F32), axis=0, keepdims=True) * has_prev
            dstate[hh] = dec[:, kc] * ds
        dkdec = dkd_ref[...]
        dk_ref[...] = (dkdec * e).astype(BF16)
        dd = dkdec * kf * e
        dgend = jnp.sum(dd, axis=0, keepdims=True) + ddec_ref[...] * dec
        last = lax.broadcasted_iota(jnp.int32, (C, 1), 0) == C - 1
        dg = jnp.where(last, dgend, 0.0) - dd
        dla = jnp.dot(triT_ref[...], dg, preferred_element_type=F32, precision=HIGHEST)
        dlogit = dla * (1.0 / GATE_TAU) * (1.0 - jax.nn.sigmoid(logit))
        dl_ref[...] = dlogit.astype(BF16)
        dba_ref[...] += jnp.sum(dlogit, axis=0, keepdims=True)

    rev = lambda t: NC - 1 - t
    half = pl.BlockSpec((C, D // 2), lambda t: (rev(t), 0))
    full = pl.BlockSpec((C, D), lambda t: (rev(t), 0))
    return pl.pallas_call(
        body, name=name, grid=(NC,),
        in_specs=[half, pl.BlockSpec((C, D // 2), lambda t: (rev(t), 1)), pl.BlockSpec((C, D), lambda t: (rev(t), 1)),
                  pl.BlockSpec((C, LOW), lambda t: (rev(t), 0)),
                  pl.BlockSpec((LOW, D // 2), lambda t: (0, 0)), pl.BlockSpec((1, D // 2), lambda t: (0, 0)),
                  pl.BlockSpec((C, C), lambda t: (0, 0)), pl.BlockSpec((C, C), lambda t: (0, 0)),
                  pl.BlockSpec((None, H, dv, dk), lambda t: (rev(t), 0, 0, 0)),
                  pl.BlockSpec((None, H, dv, dk), lambda t: (jnp.maximum(rev(t) - 1, 0), 0, 0, 0)),
                  full],
        out_specs=[half, half, full, half, pl.BlockSpec((1, D // 2), lambda t: (0, 0))],
        out_shape=[_sds((S, D // 2), BF16), _sds((S, D // 2), BF16), _sds((S, D), BF16), _sds((S, D // 2), BF16),
                   _sds((1, D // 2), F32)],
        scratch_shapes=[pltpu.VMEM((H, dv, dk), F32), pltpu.VMEM((C, D // 2), F32), pltpu.VMEM((1, D // 2), F32)],
        compiler_params=_cparams(("arbitrary",)),
    )(proj, proj, proj, lr, w2p, ba, tri, triT, states, states, do)


def _gla_post_common(o, r, ng):
    rs = lax.rsqrt(jnp.mean(o * o, axis=-1, keepdims=True) + EPS)
    oh = o * rs
    sig = jax.nn.sigmoid(r)
    return rs, oh, oh * ng, sig, r * sig


def gla_post_fwd(name, o_raw, proj, ng):
    S, D = o_raw.shape
    dv = D // GLA_HEADS
    tr = min(ROWS, S)

    def body(o_ref, r_ref, ng_ref, og_ref):
        for hh in range(GLA_HEADS):
            cols = slice(hh * dv, (hh + 1) * dv)
            _, _, on, _, sil = _gla_post_common(o_ref[:, cols].astype(F32), r_ref[:, cols].astype(F32), ng_ref[:, cols])
            og_ref[:, cols] = (on * sil).astype(BF16)

    row = pl.BlockSpec((tr, D), lambda i: (i, 0))
    return pl.pallas_call(
        body, name=name, grid=(S // tr,),
        in_specs=[row, pl.BlockSpec((tr, D), lambda i: (i, 2)), pl.BlockSpec((1, D), lambda i: (0, 0))],
        out_specs=row, out_shape=_sds((S, D), BF16), compiler_params=_cparams(("parallel",)),
    )(o_raw, proj, ng)


def gla_post_bwd(name, dog, o_raw, proj, ng):
    S, D = o_raw.shape
    dv = D // GLA_HEADS
    tr = min(ROWS, S)

    def body(dog_ref, o_ref, r_ref, ng_ref, do_ref, dr_ref, dng_ref):
        @pl.when(pl.program_id(0) == 0)
        def _():
            dng_ref[...] = jnp.zeros_like(dng_ref)

        for hh in range(GLA_HEADS):
            cols = slice(hh * dv, (hh + 1) * dv)
            r = r_ref[:, cols].astype(F32)
            ngv = ng_ref[:, cols]
            rs, oh, on, sig, sil = _gla_post_common(o_ref[:, cols].astype(F32), r, ngv)
            dogv = dog_ref[:, cols].astype(F32)
            don = dogv * sil
            dr_ref[:, cols] = (dogv * on * (sig * (1.0 + r * (1.0 - sig)))).astype(BF16)
            dng_ref[:, cols] += jnp.sum(don * oh, axis=0, keepdims=True)
            doh = don * ngv
            do_ref[:, cols] = (rs * (doh - oh * jnp.mean(doh * oh, axis=-1, keepdims=True))).astype(BF16)

    row = pl.BlockSpec((tr, D), lambda i: (i, 0))
    vec = pl.BlockSpec((1, D), lambda i: (0, 0))
    return pl.pallas_call(
        body, name=name, grid=(S // tr,),
        in_specs=[row, row, pl.BlockSpec((tr, D), lambda i: (i, 2)), vec],
        out_specs=[row, row, vec], out_shape=[_sds((S, D), BF16), _sds((S, D), BF16), _sds((1, D), F32)],
        compiler_params=_cparams(("arbitrary",)),
    )(dog, o_raw, proj, ng)


def cast_into(name, w, w_off, rows, chip_idx):
    W = w.shape[1]
    tr = _rows(rows, 512)

    def body(p_ref, w_ref, o_ref):
        o_ref[...] = w_ref[...].astype(BF16)

    return pl.pallas_call(
        body, name=name,
        grid_spec=pltpu.PrefetchScalarGridSpec(
            num_scalar_prefetch=1, grid=(rows // tr,),
            in_specs=[pl.BlockSpec((tr, W), lambda i, p: (w_off // tr + i, 0))],
            out_specs=pl.BlockSpec((None, tr, W), lambda i, p: (p[0], i, 0))),
        out_shape=_sds((N_CHIPS, rows, W), BF16), compiler_params=_cparams(("parallel",)),
    )(chip_idx, w)


def pair_add(name, gbuf, rsib, c_idx):
    _, R, W = gbuf.shape
    hr = R // 2
    tr = _rows(hr, 512)
    nb = hr // tr

    def body(c_ref, a_ref, b_ref, o_ref):
        o_ref[...] = (a_ref[...].astype(F32) + b_ref[...].astype(F32)).astype(BF16)

    return pl.pallas_call(
        body, name=name,
        grid_spec=pltpu.PrefetchScalarGridSpec(
            num_scalar_prefetch=1, grid=(N_CHIPS, nb),
            in_specs=[pl.BlockSpec((None, tr, W), lambda s, i, c: (s, c[0] * nb + i, 0)),
                      pl.BlockSpec((None, tr, W), lambda s, i, c: (s, i, 0))],
            out_specs=pl.BlockSpec((None, tr, W), lambda s, i, c: (s, i, 0))),
        out_shape=_sds((N_CHIPS, hr, W), BF16), compiler_params=_cparams(("parallel", "parallel")),
    )(c_idx, gbuf, rsib)


def sum_chips(name, p, q, chip_idx, c_idx):
    _, hr, W = p.shape
    tr = _rows(hr, 512)
    nb = hr // tr

    def body(s_ref, c_ref, p_ref, q0_ref, q1_ref, q2_ref, o_ref):
        o_ref[...] = ((p_ref[...].astype(F32) + q0_ref[...].astype(F32)) + q1_ref[...].astype(F32)) + q2_ref[...].astype(F32)

    def qspec(j):
        return pl.BlockSpec((None, tr, W), lambda i, s, c: (j, i, 0))

    return pl.pallas_call(
        body, name=name,
        grid_spec=pltpu.PrefetchScalarGridSpec(
            num_scalar_prefetch=2, grid=(nb,),
            in_specs=[pl.BlockSpec((None, tr, W), lambda i, s, c: (s[0], i, 0)), qspec(0), qspec(1), qspec(2)],
            out_specs=pl.BlockSpec((tr, W), lambda i, s, c: (c[0] * nb + i, 0))),
        out_shape=_sds((2 * hr, W), F32), compiler_params=_cparams(("parallel",)),
    )(chip_idx, c_idx, p, q, q, q)


def sum_devices(name, parts):
    _, R, W = parts.shape

    def body(p_ref, o_ref):
        acc = p_ref[0]
        for d in range(1, N_DEV):
            acc = acc + p_ref[d]
        o_ref[...] = acc

    tr = _rows(R, 512)
    return pl.pallas_call(
        body, name=name, grid=(R // tr,), in_specs=[pl.BlockSpec((N_DEV, tr, W), lambda i: (0, i, 0))],
        out_specs=pl.BlockSpec((tr, W), lambda i: (i, 0)), out_shape=_sds((R, W), F32),
        compiler_params=_cparams(("parallel",)),
    )(parts)


def adamw(name, g, w, m, v, w_off, prev=None, comm=None, B=None):
    rows, W = g.shape
    tr = _rows(rows, 256)
    n_prev = 4 if prev is not None else 0
    n_in, n_out, n_al = (len(comm.ins), len(comm.outs), len(comm.alias)) if comm is not None else (0, 0, 0)

    def body(*refs):
        g_ref, w_ref, m_ref, v_ref = refs[:4]
        pos = 4 + n_prev
        c_ins = refs[pos:pos + n_in]
        pos += n_in + n_al
        go_ref, d_ref, mo_ref, vo_ref = refs[pos:pos + 4]
        c_outs = refs[pos + 4:pos + 4 + n_out]
        c_alias = refs[pos + 4 + n_out:pos + 4 + n_out + n_al]
        if comm is not None:
            ss, rs = refs[-2], refs[-1]
            R = dict(zip(comm.ins, c_ins))
            R.update(zip([key for key, _ in comm.outs], c_outs))
            R.update(zip(comm.alias, c_alias))

            @pl.when(pl.program_id(0) == 0)
            def _():
                comm.start(R, ss, rs)

        gv = g_ref[...]
        mn = ADAM_B1 * m_ref[...] + (1.0 - ADAM_B1) * gv
        vn = ADAM_B2 * v_ref[...] + (1.0 - ADAM_B2) * (gv * gv)
        m_hat = mn / (1.0 - ADAM_B1 ** ADAM_STEP)
        v_hat = vn / (1.0 - ADAM_B2 ** ADAM_STEP)
        go_ref[...] = gv
        d_ref[...] = -ADAM_LR * (m_hat / (jnp.sqrt(v_hat) + ADAM_EPS) + ADAM_WD * w_ref[...])
        mo_ref[...] = mn
        vo_ref[...] = vn

        if comm is not None:
            @pl.when(pl.program_id(0) == rows // tr - 1)
            def _():
                comm.wait(R, ss, rs)

    blk = pl.BlockSpec((tr, W), lambda i: (w_off // tr + i, 0))
    inputs = [g, w, m, v]
    in_specs = [pl.BlockSpec((tr, W), lambda i: (i, 0)), blk, blk, blk]
    aliases = {}
    if prev is not None:
        aliases = {4 + t: t for t in range(4)}
        inputs += list(prev)
        in_specs += [ANY] * 4
    if comm is None:
        return pl.pallas_call(
            body, name=name, grid=(rows // tr,), in_specs=in_specs, out_specs=[blk] * 4,
            out_shape=[_sds(w.shape, F32)] * 4, input_output_aliases=aliases, compiler_params=_cparams(("parallel",)),
        )(*inputs)
    for t in range(n_al):
        aliases[len(inputs) + n_in + t] = 4 + n_out + t
    inputs += [B[key] for key in comm.ins] + [B[key] for key in comm.alias]
    in_specs += [ANY] * (n_in + n_al)
    res = pl.pallas_call(
        body, name=name, grid=(rows // tr,), in_specs=in_specs, out_specs=[blk] * 4 + [ANY] * (n_out + n_al),
        out_shape=[_sds(w.shape, F32)] * 4 + [s for _, s in comm.outs] + [_sds(B[key].shape, B[key].dtype) for key in comm.alias],
        scratch_shapes=[pltpu.SemaphoreType.DMA((comm.n_sems,)), pltpu.SemaphoreType.DMA((comm.n_sems,))],
        input_output_aliases=aliases, compiler_params=_cparams(("arbitrary",)),
    )(*inputs)
    B.update(zip([key for key, _ in comm.outs] + list(comm.alias), res[4:]))
    return res[:4]


def _pack(arrs):
    flat = jnp.concatenate([a.reshape(-1).astype(F32) for a in arrs])
    tile = SUBLANES * LANES * 2
    pad = (-flat.shape[0]) % tile
    return jnp.pad(flat, (0, pad)).reshape(-1, LANES)


def _unpack(packed, shapes):
    flat = packed.reshape(-1)
    out, pos = [], 0
    for s in shapes:
        n = 1
        for d in s:
            n *= d
        out.append(flat[pos:pos + n].reshape(s))
        pos += n
    return out


def kernel(x, norm_mix_g, norm_ffn_g, final_g, gm_w_in, gm_ln_g, gm_ln_b, gm_w_s, gm_b_s, gm_w_out, gla_w_in, gla_w_a1, gla_w_a2, gla_b_a, gla_norm_g, gla_w_o, ffn_w_up, ffn_w_down, loss_target, m_norm_mix_g, m_norm_ffn_g, m_final_g, m_gm_w_in, m_gm_ln_g, m_gm_ln_b, m_gm_w_s, m_gm_b_s, m_gm_w_out, m_gla_w_in, m_gla_w_a1, m_gla_w_a2, m_gla_b_a, m_gla_norm_g, m_gla_w_o, m_ffn_w_up, m_ffn_w_down, v_norm_mix_g, v_norm_ffn_g, v_final_g, v_gm_w_in, v_gm_ln_g, v_gm_ln_b, v_gm_w_s, v_gm_b_s, v_gm_w_out, v_gla_w_in, v_gla_w_a1, v_gla_w_a2, v_gla_b_a, v_gla_norm_g, v_gla_w_o, v_ffn_w_up, v_ffn_w_down):
    S, D = x.shape[1], x.shape[2]
    depth = norm_mix_g.shape[0]
    n_gm, n_gla = gm_w_in.shape[0], gla_w_in.shape[0]
    F = 4 * D
    P = GM_BLOCK
    xi, yi, ci = lax.axis_index("x"), lax.axis_index("y"), lax.axis_index("c")
    chip = 2 * xi + yi
    chip_idx = jnp.reshape(chip, (1,)).astype(jnp.int32)
    c_idx = jnp.reshape(ci, (1,)).astype(jnp.int32)

    is_gm = [i % 2 == 0 for i in range(depth)]
    w_mix_in = [(gm_w_in, m_gm_w_in, v_gm_w_in) if is_gm[i] else (gla_w_in, m_gla_w_in, v_gla_w_in) for i in range(depth)]
    w_mix_out = [(gm_w_out, m_gm_w_out, v_gm_w_out) if is_gm[i] else (gla_w_o, m_gla_w_o, v_gla_w_o) for i in range(depth)]
    w_up = (ffn_w_up, m_ffn_w_up, v_ffn_w_up)
    w_down = (ffn_w_down, m_ffn_w_down, v_ffn_w_down)

    def flat2(w):
        return w.reshape(w.shape[0] * w.shape[1], w.shape[2])

    B = {}

    for i in range(depth):
        j = i // 2
        B[f"w_in_{i}"] = cast_into(f"cast_in_{i}", flat2(w_mix_in[i][0]), j * D, D, chip_idx)
        B[f"w_out_{i}"] = cast_into(f"cast_out_{i}", flat2(w_mix_out[i][0]), j * (D // 4), D // 4, chip_idx)
        B[f"w_up_{i}"] = cast_into(f"cast_up_{i}", flat2(ffn_w_up), i * D, D, chip_idx)
        B[f"w_down_{i}"] = cast_into(f"cast_down_{i}", flat2(ffn_w_down), i * D, D, chip_idx)
    run_comm("allgather_mixer0", comm_gather_all(B, ["w_in_0", "w_out_0"]), B)

    small_w = [gla_w_a1, gla_w_a2, gla_b_a, gla_norm_g]
    gs = allgather_devices("allgather_small_weights", _pack(small_w))
    per_chip = [_unpack(gs[2 * s], [a.shape for a in small_w]) for s in range(N_CHIPS)]
    w_a1 = jnp.concatenate([p[0] for p in per_chip], axis=1)
    w_a2 = jnp.concatenate([p[1] for p in per_chip], axis=2)
    b_a = jnp.concatenate([p[2] for p in per_chip], axis=1)
    gnorm = jnp.concatenate([p[3] for p in per_chip], axis=1)
    w_a1p = jnp.pad(w_a1, ((0, 0), (0, 0), (0, LOW - GATE_RANK))).astype(BF16)
    w_a2p = jnp.pad(w_a2, ((0, 0), (0, LOW - GATE_RANK), (0, 0))).astype(BF16)

    chunk_id = jnp.arange(P) // CHUNK
    mask = chunk_id[None, :] <= chunk_id[:, None]
    wm_all = jnp.where(mask[None, None], gm_w_s, 0.0)
    tri = jnp.tril(jnp.ones((CHUNK, CHUNK), F32))
    triT = tri.T

    def ici(*keys):
        return comm_gather_ici(B, list(keys))

    def fwd(*keys):
        return comm_gather_forward(B, list(keys))

    xs = x[0]
    saved = []
    for i in range(depth):
        j = i // 2
        nxt = i + 1 < depth
        c_in = ici("w_up_0") if i == 0 else comm_merge(fwd(f"w_up_{i}"), ici(f"w_down_{i}"))
        c_out = comm_merge(fwd("w_up_0"), ici("w_down_0")) if i == 0 else fwd(f"w_down_{i}")
        c_up = comm_merge(fwd("w_down_0") if i == 0 else None, ici(f"w_in_{i + 1}", f"w_out_{i + 1}") if nxt else None)
        c_down = comm_merge(fwd(f"w_in_{i + 1}", f"w_out_{i + 1}"), ici(f"w_up_{i + 1}")) if nxt else None
        h1 = rmsnorm_fwd(f"norm_mix_{i}", xs, norm_mix_g[i][None])
        if is_gm[i]:
            zp = mm_fwd_col(f"gm_in_{i}", h1, B[f"w_in_{i}"], 2 * D, comm=c_in, B=B)
            wm = wm_all[j].astype(BF16)
            gated = gm_mid_fwd(f"gm_mid_{i}", zp, gm_ln_g[j][None], gm_ln_b[j][None], wm, gm_b_s[j].T)
            x_mid = mm_fwd_row(f"gm_out_{i}", gated, B[f"w_out_{i}"], xs, comm=c_out, B=B)
            mix = (h1, zp, gated)
        else:
            proj = mm_fwd_col(f"gla_in_{i}", h1, B[f"w_in_{i}"], 3 * D, comm=c_in, B=B)
            lr = mm_plain(f"gla_low_{i}", h1, w_a1p[j], NN, BF16)
            o_raw, states = gla_scan_fwd(f"gla_scan_{i}", proj, lr, w_a2p[j], b_a[j][None], tri)
            og = gla_post_fwd(f"gla_post_{i}", o_raw, proj, gnorm[j][None])
            x_mid = mm_fwd_row(f"gla_out_{i}", og, B[f"w_out_{i}"], xs, comm=c_out, B=B)
            mix = (h1, proj, lr, o_raw, states, og)
        h2 = rmsnorm_fwd(f"norm_ffn_{i}", x_mid, norm_ffn_g[i][None])
        act = mm_fwd_col(f"ffn_up_{i}", h2, B[f"w_up_{i}"], F, epilogue=lambda acc: jnp.maximum(acc, 0.0), comm=c_up, B=B)
        x_out = mm_fwd_row(f"ffn_down_{i}", act, B[f"w_down_{i}"], x_mid, prologue=lambda a: a * a, comm=c_down, B=B)
        saved.append((xs, x_mid, h2, act, mix))
        xs = x_out

    loss_part, dx, dxb, d_final_g = final_loss("final_loss", xs, final_g[None], loss_target[0])
    loss = lax.psum(loss_part[0, 0], ("x", "y", "c"))

    def exchange(i, kinds):
        return comm_exchange(B, [f"d_{k}_{i}" for k in kinds], [f"r_{k}_{i}" for k in kinds])

    def scatter(i, kinds):
        return comm_scatter(B, [f"p_{k}_{i}" for k in kinds], [f"q_{k}_{i}" for k in kinds])

    def join(i, kinds):
        return comm_join(B, [f"f_{k}_{i}" for k in kinds])

    def pair_sums(i, kinds):
        for k in kinds:
            B[f"p_{k}_{i}"] = pair_add(f"pair_add_{k}_{i}", B[f"d_{k}_{i}"], B[f"r_{k}_{i}"], c_idx)

    def chip_sums(i, kinds):
        for k in kinds:
            B[f"f_{k}_{i}"] = sum_chips(f"sum_chips_{k}_{i}", B[f"p_{k}_{i}"], B[f"q_{k}_{i}"], chip_idx, c_idx)

    FFN, MIX = ("up", "down"), ("out", "in")
    d_mix_g, d_ffn_g = [None] * depth, [None] * depth
    d_ln_g, d_ln_b, d_w_s, d_b_s = [None] * n_gm, [None] * n_gm, [None] * n_gm, [None] * n_gm
    d_a1, d_a2, d_ba, d_gn = [None] * n_gla, [None] * n_gla, [None] * n_gla, [None] * n_gla
    for i in reversed(range(depth)):
        j = i // 2
        up = i + 1 < depth
        x_in, x_mid, h2, act, mix = saved[i]
        com = comm_merge(join(i + 1, FFN), exchange(i + 1, MIX)) if up else None
        d_apre = mm_bwd_row_x(f"ffn_down_dx_{i}", dxb, B[f"w_down_{i}"], mul=act, comm=com, B=B)
        if up:
            pair_sums(i + 1, MIX)
        B[f"d_down_{i}"] = mm_bwd_row_w(f"ffn_down_dw_{i}", act, dxb, prologue=lambda a: a * a,
                                        comm=scatter(i + 1, MIX) if up else None, B=B)
        if up:
            chip_sums(i + 1, MIX)
        B[f"d_up_{i}"] = mm_bwd_col_w(f"ffn_up_dw_{i}", h2, d_apre, comm=join(i + 1, MIX) if up else None, B=B)
        dh2 = mm_bwd_col_x(f"ffn_up_dx_{i}", d_apre, B[f"w_up_{i}"])
        dx, dxb, d_ffn_g[i] = rmsnorm_bwd(f"norm_ffn_bwd_{i}", x_mid, norm_ffn_g[i][None], dh2, dx)
        if is_gm[i]:
            h1, zp, gated = mix
            d_gated = mm_bwd_row_x(f"gm_out_dx_{i}", dxb, B[f"w_out_{i}"], comm=exchange(i, FFN), B=B)
            pair_sums(i, FFN)
            B[f"d_out_{i}"] = mm_bwd_row_w(f"gm_out_dw_{i}", gated, dxb)
            wm = wm_all[j].astype(BF16)
            wmT = jnp.swapaxes(wm_all[j], 1, 2).astype(BF16)
            dzp, d_ln_g[j], d_ln_b[j], dw, dbT = gm_mid_bwd(f"gm_mid_bwd_{i}", zp, d_gated, gm_ln_g[j][None], gm_ln_b[j][None],
                                                             wm, wmT, gm_b_s[j].T)
            d_w_s[j] = jnp.where(mask[None], dw, 0.0)
            d_b_s[j] = dbT.T
            B[f"d_in_{i}"] = mm_bwd_col_w(f"gm_in_dw_{i}", h1, dzp, comm=scatter(i, ("up",)), B=B)
            dh1 = mm_bwd_col_x(f"gm_in_dx_{i}", dzp, B[f"w_in_{i}"], comm=scatter(i, ("down",)), B=B)
        else:
            h1, proj, lr, o_raw, states, og = mix
            d_og = mm_bwd_row_x(f"gla_out_dx_{i}", dxb, B[f"w_out_{i}"], comm=exchange(i, FFN), B=B)
            pair_sums(i, FFN)
            B[f"d_out_{i}"] = mm_bwd_row_w(f"gla_out_dw_{i}", og, dxb)
            d_oraw, d_r, d_gn[j] = gla_post_bwd(f"gla_post_bwd_{i}", d_og, o_raw, proj, gnorm[j][None])
            dq, dk, dv, dlogit, d_ba[j] = gla_scan_bwd(f"gla_scan_bwd_{i}", proj, lr, w_a2p[j], b_a[j][None], tri, triT, states, d_oraw)
            dproj = jnp.concatenate([dq, dk, dv, d_r], axis=1)
            B[f"d_in_{i}"] = mm_bwd_col_w(f"gla_in_dw_{i}", h1, dproj, comm=scatter(i, ("up",)), B=B)
            dh1 = mm_bwd_col_x(f"gla_in_dx_{i}", dproj, B[f"w_in_{i}"], comm=scatter(i, ("down",)), B=B)
            dlr = mm_plain(f"gla_gate_dlow_{i}", dlogit, w_a2p[j], NT, BF16)
            d_a2[j] = mm_plain(f"gla_gate_dw2_{i}", lr, dlogit, TN, F32)[:GATE_RANK]
            d_a1[j] = mm_plain(f"gla_gate_dw1_{i}", h1, dlr, TN, F32)[:, :GATE_RANK]
            dh1 = mm_plain(f"gla_gate_dx_{i}", dlr, w_a1p[j], NT, F32, add=dh1)
        chip_sums(i, FFN)
        dx, dxb, d_mix_g[i] = rmsnorm_bwd(f"norm_mix_bwd_{i}", x_in, norm_mix_g[i][None], dh1, dx)
    grad_x = dx[None]

    small_g = [jnp.concatenate(d_mix_g), jnp.concatenate(d_ffn_g), d_final_g[0], jnp.concatenate(d_ln_g), jnp.concatenate(d_ln_b),
               jnp.stack(d_w_s), jnp.stack(d_b_s), jnp.stack(d_a1), jnp.stack(d_a2), jnp.concatenate(d_ba), jnp.concatenate(d_gn)]
    small_shapes = [(depth, D), (depth, D), (D,), (n_gm, D), (n_gm, D), (n_gm, GM_GROUPS, P, P), (n_gm, GM_GROUPS, P),
                    (n_gla, D, GATE_RANK), (n_gla, GATE_RANK, D // 2), (n_gla, D // 2), (n_gla, D)]
    B["small_g"] = _pack(small_g)
    n_small = B["small_g"].shape[0]
    third = (n_small // 3) // SUBLANES * SUBLANES

    def small_rows(t):
        r0 = t * third
        return comm_allgather_rows(B, "small_g", "small_parts", r0, third if t < 2 else n_small - r0, fresh=(t == 0))

    res = {}

    def big(kind, i, comm=None):
        j = i // 2
        key, wmv, w_off = {"up": ("up", w_up, i * D), "down": ("down", w_down, i * D),
                           "out": ("gm_out" if is_gm[i] else "gla_o", w_mix_out[i], j * (D // 4)),
                           "in": ("gm_in" if is_gm[i] else "gla_in", w_mix_in[i], j * D)}[kind]
        w, m, v = (flat2(t) for t in wmv)
        res[key] = adamw(f"adamw_{key}_{i}", B[f"f_{kind}_{i}"], w, m, v, w_off, prev=res.get(key), comm=comm, B=B)

    assert depth >= 4, "the carrier schedule below is written for four layers or more"
    top = depth - 1
    big("up", top, comm_merge(join(0, FFN), exchange(0, MIX)))
    pair_sums(0, MIX)
    big("down", top, scatter(0, ("out",)))
    big("in", top, small_rows(0))
    big("up", top - 1, scatter(0, ("in",)))
    chip_sums(0, MIX)
    big("down", top - 1, comm_merge(join(0, MIX), small_rows(1)))
    big("up", top - 2, small_rows(2))
    done = {("up", top), ("down", top), ("in", top), ("up", top - 1), ("down", top - 1), ("up", top - 2)}
    for i in reversed(range(depth)):
        for kind in ("up", "down", "out", "in"):
            if (kind, i) not in done:
                big(kind, i)

    red = _unpack(sum_devices("sum_small_grads", B["small_parts"]), small_shapes)
    g_rep = red[:7]
    g_a1 = lax.dynamic_slice_in_dim(red[7], chip * (D // 4), D // 4, axis=1)
    g_a2 = lax.dynamic_slice_in_dim(red[8], chip * (D // 8), D // 8, axis=2)
    g_ba = lax.dynamic_slice_in_dim(red[9], chip * (D // 8), D // 8, axis=1)
    g_gn = lax.dynamic_slice_in_dim(red[10], chip * (D // 4), D // 4, axis=1)
    g_small = g_rep + [g_a1, g_a2, g_ba, g_gn]
    w_small = [norm_mix_g, norm_ffn_g, final_g, gm_ln_g, gm_ln_b, gm_w_s, gm_b_s, gla_w_a1, gla_w_a2, gla_b_a, gla_norm_g]
    m_small = [m_norm_mix_g, m_norm_ffn_g, m_final_g, m_gm_ln_g, m_gm_ln_b, m_gm_w_s, m_gm_b_s, m_gla_w_a1, m_gla_w_a2, m_gla_b_a, m_gla_norm_g]
    v_small = [v_norm_mix_g, v_norm_ffn_g, v_final_g, v_gm_ln_g, v_gm_ln_b, v_gm_w_s, v_gm_b_s, v_gla_w_a1, v_gla_w_a2, v_gla_b_a, v_gla_norm_g]
    shapes_small = [w.shape for w in w_small]
    sm = adamw("adamw_small", _pack(g_small), _pack(w_small), _pack(m_small), _pack(v_small), 0)
    sm = [_unpack(o, shapes_small) for o in sm]

    def shaped(key, like):
        return [o.reshape(like.shape) for o in res[key]]

    o_gm_in, o_gm_out = shaped("gm_in", gm_w_in), shaped("gm_out", gm_w_out)
    o_gla_in, o_gla_o = shaped("gla_in", gla_w_in), shaped("gla_o", gla_w_o)
    o_up, o_down = shaped("up", ffn_w_up), shaped("down", ffn_w_down)

    def ordered(kind):
        s = sm[kind]
        return [s[0], s[1], s[2], o_gm_in[kind], s[3], s[4], s[5], s[6], o_gm_out[kind], o_gla_in[kind],
                s[7], s[8], s[9], s[10], o_gla_o[kind], o_up[kind], o_down[kind]]

    return (loss, grad_x, *ordered(0), *ordered(1), *ordered(2), *ordered(3))
```

```python
import jax
import jax.numpy as jnp
from jax import lax
from jax.experimental import pallas as pl
from jax.experimental.pallas import tpu as pltpu

F32 = jnp.float32
BF16 = jnp.bfloat16

EPS = 1e-6
CHUNK = 64
GM_BLOCK = 128
GM_GROUPS = 8
GLA_HEADS = 4
GATE_RANK = 16
GATE_TAU = 16.0
LOW = 128
N_CHIPS = 4
N_DEV = 8

ADAM_LR = 0.001
ADAM_B1 = 0.9
ADAM_B2 = 0.999
ADAM_EPS = 1e-08
ADAM_WD = 0.01
ADAM_STEP = 10

V7X_VMEM_LIMIT = 48 * 1024 * 1024
LANES = 128
SUBLANES = 8
BF16_ROWS = 16
MESH = pl.DeviceIdType.MESH
HIGHEST = lax.Precision.HIGHEST


def _pick(n, cap):
    if n <= cap:
        return n
    best = LANES
    for t in range(LANES, cap + 1, LANES):
        if n % t == 0:
            best = t
    return best


def _rows(n, cap):
    if n <= cap:
        return n
    best = 0
    for t in range(BF16_ROWS, cap + 1, BF16_ROWS):
        if n % t == 0:
            best = t
    return best if best >= LANES else n


def _cparams(sem=None):
    return pltpu.CompilerParams(dimension_semantics=sem, vmem_limit_bytes=V7X_VMEM_LIMIT)


ANY = pl.BlockSpec(memory_space=pl.ANY)


def _sds(shape, dtype):
    return jax.ShapeDtypeStruct(shape, dtype)


def _place():
    return lax.axis_index("x"), lax.axis_index("y"), lax.axis_index("c")


def _other_chips(x, y):
    return [(1 - x, y), (x, 1 - y), (1 - x, 1 - y)]


class Comm:
    def __init__(self, ins, outs, alias, n_sems, start, wait):
        self.ins, self.outs, self.alias, self.n_sems, self.start, self.wait = ins, outs, alias, n_sems, start, wait


class _Shift:
    def __init__(self, sems, by):
        self.sems, self.by = sems, by

    @property
    def at(self):
        return self

    def __getitem__(self, k):
        return self.sems.at[self.by + k]


def comm_merge(*comms):
    comms = [c for c in comms if c is not None]
    if not comms:
        return None
    if len(comms) == 1:
        return comms[0]
    offs, total = [], 0
    for c in comms:
        offs.append(total)
        total += c.n_sems

    def start(R, ss, rs):
        for c, o in zip(comms, offs):
            c.start(R, _Shift(ss, o), _Shift(rs, o))

    def wait(R, ss, rs):
        for c, o in zip(comms, offs):
            c.wait(R, _Shift(ss, o), _Shift(rs, o))

    return Comm(sum((c.ins for c in comms), []), sum((c.outs for c in comms), []), sum((c.alias for c in comms), []),
                total, start, wait)


def _remote(src, dst, ss, rs, k, to):
    return pltpu.make_async_remote_copy(src_ref=src, dst_ref=dst, send_sem=ss.at[k], recv_sem=rs.at[k],
                                        device_id=to, device_id_type=MESH)


def comm_gather_ici(B, keys):
    hr = {k: B[k].shape[1] // 2 for k in keys}

    def region(R, k, chip, hc):
        return R[k].at[chip, pl.ds(hc * hr[k], hr[k]), :]

    def start(R, ss, rs):
        x, y, c = _place()
        for a, k in enumerate(keys):
            for j, (px, py) in enumerate(_other_chips(x, y)):
                mine = region(R, k, 2 * x + y, c)
                _remote(mine, mine, ss, rs, 3 * a + j, (px, py, c)).start()

    def wait(R, ss, rs):
        x, y, c = _place()
        for a, k in enumerate(keys):
            for j, (px, py) in enumerate(_other_chips(x, y)):
                theirs = region(R, k, 2 * px + py, c)
                _remote(theirs, theirs, ss, rs, 3 * a + j, (px, py, c)).wait_recv()
                mine = region(R, k, 2 * x + y, c)
                _remote(mine, mine, ss, rs, 3 * a + j, (px, py, c)).wait_send()

    return Comm([], [], list(keys), 3 * len(keys), start, wait)


def comm_gather_forward(B, keys):
    hr = {k: B[k].shape[1] // 2 for k in keys}

    def region(R, k, chip, hc):
        return R[k].at[chip, pl.ds(hc * hr[k], hr[k]), :]

    def start(R, ss, rs):
        x, y, c = _place()
        for a, k in enumerate(keys):
            for j, (px, py) in enumerate(_other_chips(x, y)):
                got = region(R, k, 2 * px + py, c)
                _remote(got, got, ss, rs, 3 * a + j, (x, y, 1 - c)).start()

    def wait(R, ss, rs):
        x, y, c = _place()
        for a, k in enumerate(keys):
            for j, (px, py) in enumerate(_other_chips(x, y)):
                other = region(R, k, 2 * px + py, 1 - c)
                _remote(other, other, ss, rs, 3 * a + j, (x, y, 1 - c)).wait_recv()
                got = region(R, k, 2 * px + py, c)
                _remote(got, got, ss, rs, 3 * a + j, (x, y, 1 - c)).wait_send()

    return Comm([], [], list(keys), 3 * len(keys), start, wait)


def comm_gather_all(B, keys):
    ici, fwd = comm_gather_ici(B, keys), comm_gather_forward(B, keys)
    n1 = ici.n_sems

    def wait(R, ss, rs):
        ici.wait(R, ss, rs)
        fwd.start(R, _Shift(ss, n1), _Shift(rs, n1))
        fwd.wait(R, _Shift(ss, n1), _Shift(rs, n1))

    return Comm([], [], list(keys), n1 + fwd.n_sems, ici.start, wait)


def comm_exchange(B, src_keys, out_keys):
    hr = {k: B[k].shape[1] // 2 for k in src_keys}

    def descr(R, ss, rs, a):
        x, y, c = _place()
        k = src_keys[a]
        return _remote(R[k].at[:, pl.ds((1 - c) * hr[k], hr[k]), :], R[out_keys[a]], ss, rs, a, (x, y, 1 - c))

    def start(R, ss, rs):
        for a in range(len(src_keys)):
            descr(R, ss, rs, a).start()

    def wait(R, ss, rs):
        for a in range(len(src_keys)):
            descr(R, ss, rs, a).wait()

    outs = [(o, _sds((N_CHIPS, hr[k], B[k].shape[2]), B[k].dtype)) for k, o in zip(src_keys, out_keys)]
    return Comm(list(src_keys), outs, [], len(src_keys), start, wait)


def comm_scatter(B, p_keys, q_keys):
    def each(R, ss, rs, fn):
        x, y, c = _place()
        for a, (pk, qk) in enumerate(zip(p_keys, q_keys)):
            for j, (px, py) in enumerate(_other_chips(x, y)):
                fn(_remote(R[pk].at[2 * px + py], R[qk].at[j], ss, rs, 3 * a + j, (px, py, c)))

    def start(R, ss, rs):
        each(R, ss, rs, lambda d: d.start())

    def wait(R, ss, rs):
        each(R, ss, rs, lambda d: d.wait())

    outs = [(qk, _sds((3,) + B[pk].shape[1:], B[pk].dtype)) for pk, qk in zip(p_keys, q_keys)]
    return Comm(list(p_keys), outs, [], 3 * len(p_keys), start, wait)


def comm_join(B, keys):
    hr = {k: B[k].shape[0] // 2 for k in keys}

    def region(R, k, hc):
        return R[k].at[pl.ds(hc * hr[k], hr[k]), :]

    def start(R, ss, rs):
        x, y, c = _place()
        for a, k in enumerate(keys):
            mine = region(R, k, c)
            _remote(mine, mine, ss, rs, a, (x, y, 1 - c)).start()

    def wait(R, ss, rs):
        x, y, c = _place()
        for a, k in enumerate(keys):
            other = region(R, k, 1 - c)
            _remote(other, other, ss, rs, a, (x, y, 1 - c)).wait_recv()
            mine = region(R, k, c)
            _remote(mine, mine, ss, rs, a, (x, y, 1 - c)).wait_send()

    return Comm([], [], list(keys), len(keys), start, wait)


def comm_allgather_rows(B, src_key, out_key, r0, nr, fresh):
    def peer(x, y, c, k):
        return (1 - x if k & 4 else x, 1 - y if k & 2 else y, 1 - c if k & 1 else c)

    def start(R, ss, rs):
        x, y, c = _place()
        me = 4 * x + 2 * y + c
        src = R[src_key].at[pl.ds(r0, nr), :]
        dst = R[out_key].at[me, pl.ds(r0, nr), :]
        pltpu.make_async_copy(src, dst, ss.at[N_DEV - 1]).start()
        for k in range(1, N_DEV):
            _remote(src, dst, ss, rs, k - 1, peer(x, y, c, k)).start()

    def wait(R, ss, rs):
        x, y, c = _place()
        me = 4 * x + 2 * y + c
        src = R[src_key].at[pl.ds(r0, nr), :]
        for k in range(1, N_DEV):
            px, py, pc = peer(x, y, c, k)
            theirs = R[out_key].at[4 * px + 2 * py + pc, pl.ds(r0, nr), :]
            _remote(theirs, theirs, ss, rs, k - 1, (px, py, pc)).wait_recv()
            _remote(src, R[out_key].at[me, pl.ds(r0, nr), :], ss, rs, k - 1, (px, py, pc)).wait_send()
        pltpu.make_async_copy(src, R[out_key].at[me, pl.ds(r0, nr), :], ss.at[N_DEV - 1]).wait()

    outs = [(out_key, _sds((N_DEV,) + B[src_key].shape, B[src_key].dtype))] if fresh else []
    return Comm([src_key], outs, [] if fresh else [out_key], N_DEV, start, wait)


def run_comm(name, comm, B):
    n_in, n_out, n_al = len(comm.ins), len(comm.outs), len(comm.alias)

    def body(*refs):
        R = dict(zip(comm.ins, refs[:n_in]))
        R.update(zip([k for k, _ in comm.outs], refs[n_in + n_al:n_in + n_al + n_out]))
        R.update(zip(comm.alias, refs[n_in + n_al + n_out:n_in + n_al + n_out + n_al]))
        ss, rs = refs[-2], refs[-1]
        comm.start(R, ss, rs)
        comm.wait(R, ss, rs)

    res = pl.pallas_call(
        body, name=name, in_specs=[ANY] * (n_in + n_al), out_specs=[ANY] * (n_out + n_al),
        out_shape=[s for _, s in comm.outs] + [_sds(B[k].shape, B[k].dtype) for k in comm.alias],
        input_output_aliases={n_in + t: n_out + t for t in range(n_al)},
        scratch_shapes=[pltpu.SemaphoreType.DMA((comm.n_sems,)), pltpu.SemaphoreType.DMA((comm.n_sems,))],
    )(*[B[k] for k in comm.ins], *[B[k] for k in comm.alias])
    B.update(zip([k for k, _ in comm.outs] + list(comm.alias), res))


def allgather_devices(name, v):
    def body(v_ref, out_ref, send_sems, recv_sems, local_sem):
        x, y, c = _place()
        me = 4 * x + 2 * y + c
        mine = pltpu.make_async_copy(v_ref, out_ref.at[me], local_sem)
        mine.start()

        def peer(k):
            return (1 - x if k & 4 else x, 1 - y if k & 2 else y, 1 - c if k & 1 else c)

        def copy(k, src, dst):
            return pltpu.make_async_remote_copy(src_ref=src, dst_ref=dst, send_sem=send_sems.at[k - 1],
                                                recv_sem=recv_sems.at[k - 1], device_id=peer(k), device_id_type=MESH)

        sends = [copy(k, v_ref, out_ref.at[me]) for k in range(1, N_DEV)]
        for cp in sends:
            cp.start()
        for k in range(1, N_DEV):
            px, py, pc = peer(k)
            them = 4 * px + 2 * py + pc
            copy(k, out_ref.at[them], out_ref.at[them]).wait_recv()
        for cp in sends:
            cp.wait_send()
        mine.wait()

    return pl.pallas_call(
        body, name=name, in_specs=[ANY], out_specs=ANY, out_shape=_sds((N_DEV,) + v.shape, v.dtype),
        scratch_shapes=[pltpu.SemaphoreType.DMA((N_DEV - 1,)), pltpu.SemaphoreType.DMA((N_DEV - 1,)), pltpu.SemaphoreType.DMA],
    )(v)


NN = ((1,), (0,))
NT = ((1,), (1,))
TN = ((0,), (0,))


def _mm(name, a, b, *, dims, grid, a_spec, b_spec, out_spec, out_shape, acc_shape, extra=(), extra_specs=(),
        prologue=None, epilogue=None, b_reshape=None, dot_fn=None, comm=None, B=None):
    gi, gj, nk = grid
    n_extra = len(extra)
    n_in, n_out, n_al = (len(comm.ins), len(comm.outs), len(comm.alias)) if comm is not None else (0, 0, 0)

    def body(*refs):
        a_ref, b_ref = refs[0], refs[1]
        ex = refs[2:2 + n_extra]
        pos = 2 + n_extra
        c_ins = refs[pos:pos + n_in]
        pos += n_in + n_al
        o_ref = refs[pos]
        c_outs = refs[pos + 1:pos + 1 + n_out]
        c_alias = refs[pos + 1 + n_out:pos + 1 + n_out + n_al]
        scratch = refs[pos + 1 + n_out + n_al:]
        i, j, k = pl.program_id(0), pl.program_id(1), pl.program_id(2)
        if comm is not None:
            ss, rs = scratch[-2], scratch[-1]
            R = dict(zip(comm.ins, c_ins))
            R.update(zip([key for key, _ in comm.outs], c_outs))
            R.update(zip(comm.alias, c_alias))

            @pl.when(jnp.logical_and(jnp.logical_and(i == 0, j == 0), k == 0))
            def _():
                comm.start(R, ss, rs)

        av = a_ref[...]
        if prologue is not None:
            av = prologue(av)
        if dot_fn is not None:
            p = dot_fn(av, b_ref)
        else:
            bv = b_ref[...]
            if b_reshape is not None:
                bv = bv.reshape(b_reshape)
            p = lax.dot_general(av, bv, (dims, ((), ())), preferred_element_type=F32)

        def finish(acc):
            r = acc if epilogue is None else epilogue(acc, *ex)
            o_ref[...] = r.astype(o_ref.dtype)

        if nk == 1:
            finish(p)
        else:
            acc_ref = scratch[0]

            @pl.when(k == 0)
            def _():
                acc_ref[...] = jnp.zeros_like(acc_ref)

            acc_ref[...] += p

            @pl.when(k == nk - 1)
            def _():
                finish(acc_ref[...])

        if comm is not None:
            @pl.when(jnp.logical_and(jnp.logical_and(i == gi - 1, j == gj - 1), k == nk - 1))
            def _():
                comm.wait(R, ss, rs)

    inputs = [a, b, *extra]
    in_specs = [a_spec, b_spec, *extra_specs]
    scratch_shapes = [pltpu.VMEM(acc_shape, F32)] if nk > 1 else []
    if comm is None:
        return pl.pallas_call(
            body, name=name, grid=grid, in_specs=in_specs, out_specs=out_spec, out_shape=out_shape,
            scratch_shapes=scratch_shapes, compiler_params=_cparams(("parallel", "parallel", "arbitrary")),
        )(*inputs)
    aliases = {len(inputs) + n_in + t: 1 + n_out + t for t in range(n_al)}
    inputs += [B[key] for key in comm.ins] + [B[key] for key in comm.alias]
    in_specs += [ANY] * (n_in + n_al)
    res = pl.pallas_call(
        body, name=name, grid=grid, in_specs=in_specs, out_specs=[out_spec] + [ANY] * (n_out + n_al),
        out_shape=[out_shape] + [s for _, s in comm.outs] + [_sds(B[key].shape, B[key].dtype) for key in comm.alias],
        scratch_shapes=scratch_shapes + [pltpu.SemaphoreType.DMA((comm.n_sems,)), pltpu.SemaphoreType.DMA((comm.n_sems,))],
        input_output_aliases=aliases, compiler_params=_cparams(("arbitrary", "arbitrary", "arbitrary")),
    )(*inputs)
    B.update(zip([key for key, _ in comm.outs] + list(comm.alias), res[1:]))
    return res[0]


def mm_fwd_col(name, h, wg, n_out, epilogue=None, out_dtype=BF16, **kw):
    S, D = h.shape
    W = wg.shape[2]
    tm, tn = min(1024, S), _pick(W, 1024)
    wps = W // tn
    return _mm(name, h, wg, dims=NN, grid=(S // tm, n_out // tn, 1),
               a_spec=pl.BlockSpec((tm, D), lambda i, j, k: (i, 0)),
               b_spec=pl.BlockSpec((None, D, tn), lambda i, j, k: (j // wps, 0, j % wps)),
               out_spec=pl.BlockSpec((tm, tn), lambda i, j, k: (i, j)),
               out_shape=_sds((S, n_out), out_dtype), acc_shape=(tm, tn), epilogue=epilogue, **kw)


def mm_fwd_row(name, a, wg, res, prologue=None, **kw):
    S, K = a.shape
    ksh, D = wg.shape[1], wg.shape[2]
    tm, tn, tk = min(1024, S), min(1024, D), min(2048, ksh)
    res_spec = pl.BlockSpec((tm, tn), lambda i, j, k: (i, j))
    if K <= 2048:
        return _mm(name, a, wg, dims=NN, grid=(S // tm, D // tn, 1),
                   a_spec=pl.BlockSpec((tm, K), lambda i, j, k: (i, 0)),
                   b_spec=pl.BlockSpec((N_CHIPS, ksh, tn), lambda i, j, k: (0, 0, j)),
                   out_spec=res_spec, out_shape=_sds((S, D), F32), acc_shape=(tm, tn),
                   extra=(res,), extra_specs=(res_spec,), prologue=prologue, epilogue=lambda acc, r: acc + r[...],
                   b_reshape=(K, tn), **kw)
    kps = ksh // tk
    return _mm(name, a, wg, dims=NN, grid=(S // tm, D // tn, K // tk),
               a_spec=pl.BlockSpec((tm, tk), lambda i, j, k: (i, k)),
               b_spec=pl.BlockSpec((None, tk, tn), lambda i, j, k: (k // kps, k % kps, j)),
               out_spec=res_spec, out_shape=_sds((S, D), F32), acc_shape=(tm, tn),
               extra=(res,), extra_specs=(res_spec,), prologue=prologue, epilogue=lambda acc, r: acc + r[...], **kw)


def mm_bwd_col_x(name, dz, wg, add=None, **kw):
    S, N = dz.shape
    D, W = wg.shape[1], wg.shape[2]
    t = 1024 if N <= 4096 else 512
    tm, to = min(t, S), min(t, D)
    extra, extra_specs, epi = (), (), None
    if add is not None:
        extra, extra_specs = (add,), (pl.BlockSpec((tm, to), lambda i, j, k: (i, j)),)
        epi = lambda acc, r: acc + r[...]
    if N > 6144:
        tm, to = min(1024, S), min(1024, D)
        return _mm(name, dz, wg, dims=NT, grid=(S // tm, D // to, N_CHIPS),
                   a_spec=pl.BlockSpec((tm, W), lambda i, j, k: (i, k)),
                   b_spec=pl.BlockSpec((None, to, W), lambda i, j, k: (k, j, 0)),
                   out_spec=pl.BlockSpec((tm, to), lambda i, j, k: (i, j)),
                   out_shape=_sds((S, D), F32), acc_shape=(tm, to), extra=extra, extra_specs=extra_specs, epilogue=epi, **kw)

    def dot_fn(av, b_ref):
        p = lax.dot_general(av[:, :W], b_ref[0], (NT, ((), ())), preferred_element_type=F32)
        for s in range(1, N_CHIPS):
            p = p + lax.dot_general(av[:, s * W:(s + 1) * W], b_ref[s], (NT, ((), ())), preferred_element_type=F32)
        return p

    return _mm(name, dz, wg, dims=NT, grid=(S // tm, D // to, 1),
               a_spec=pl.BlockSpec((tm, N), lambda i, j, k: (i, 0)),
               b_spec=pl.BlockSpec((N_CHIPS, to, W), lambda i, j, k: (0, j, 0)),
               out_spec=pl.BlockSpec((tm, to), lambda i, j, k: (i, j)),
               out_shape=_sds((S, D), F32), acc_shape=(tm, to), extra=extra, extra_specs=extra_specs, epilogue=epi,
               dot_fn=dot_fn, **kw)


def mm_bwd_row_x(name, dxb, wg, mul=None, **kw):
    S, D = dxb.shape
    ksh = wg.shape[1]
    tm, tn = min(1024, S), min(1024, ksh)
    kps = ksh // tn
    extra, extra_specs, epi = (), (), None
    if mul is not None:
        extra, extra_specs = (mul,), (pl.BlockSpec((tm, tn), lambda i, j, k: (i, j)),)
        epi = lambda acc, r: acc * (2.0 * r[...].astype(F32))
    return _mm(name, dxb, wg, dims=NT, grid=(S // tm, 4 * kps, 1),
               a_spec=pl.BlockSpec((tm, D), lambda i, j, k: (i, 0)),
               b_spec=pl.BlockSpec((None, tn, D), lambda i, j, k: (j // kps, j % kps, 0)),
               out_spec=pl.BlockSpec((tm, tn), lambda i, j, k: (i, j)),
               out_shape=_sds((S, 4 * ksh), BF16), acc_shape=(tm, tn), extra=extra, extra_specs=extra_specs, epilogue=epi, **kw)


def mm_bwd_col_w(name, h, dz, **kw):
    S, D = h.shape
    N = dz.shape[1]
    W = N // N_CHIPS
    tk, tn, ts = min(512, D), _pick(W, 512), S
    wps = W // tn
    return _mm(name, h, dz, dims=TN, grid=(D // tk, N // tn, S // ts),
               a_spec=pl.BlockSpec((ts, tk), lambda i, j, k: (k, i)),
               b_spec=pl.BlockSpec((ts, tn), lambda i, j, k: (k, j)),
               out_spec=pl.BlockSpec((None, tk, tn), lambda i, j, k: (j // wps, i, j % wps)),
               out_shape=_sds((N_CHIPS, D, W), BF16), acc_shape=(tk, tn), **kw)


def mm_bwd_row_w(name, a, dxb, prologue=None, **kw):
    S, K = a.shape
    D = dxb.shape[1]
    ksh = K // N_CHIPS
    tk, tn, ts = min(512, ksh), min(512, D), S
    kps = ksh // tk
    return _mm(name, a, dxb, dims=TN, grid=(K // tk, D // tn, S // ts),
               a_spec=pl.BlockSpec((ts, tk), lambda i, j, k: (k, i)),
               b_spec=pl.BlockSpec((ts, tn), lambda i, j, k: (k, j)),
               out_spec=pl.BlockSpec((None, tk, tn), lambda i, j, k: (i // kps, i % kps, j)),
               out_shape=_sds((N_CHIPS, ksh, D), BF16), acc_shape=(tk, tn), prologue=prologue, **kw)


def mm_plain(name, a, b, dims, out_dtype, add=None):
    if dims == NN:
        M, N = a.shape[0], b.shape[1]
    elif dims == NT:
        M, N = a.shape[0], b.shape[0]
    else:
        M, N = a.shape[1], b.shape[1]
    red = a.shape[0] if dims == TN else a.shape[1]
    tm, tn = min(1024, M), min(1024, N)
    tr = min(1024, red) if dims == TN else red
    nk = red // tr
    if dims == TN:
        a_spec = pl.BlockSpec((tr, tm), lambda i, j, k: (k, i))
        b_spec = pl.BlockSpec((tr, tn), lambda i, j, k: (k, j))
    elif dims == NN:
        a_spec = pl.BlockSpec((tm, tr), lambda i, j, k: (i, k))
        b_spec = pl.BlockSpec((tr, tn), lambda i, j, k: (k, j))
    else:
        a_spec = pl.BlockSpec((tm, tr), lambda i, j, k: (i, k))
        b_spec = pl.BlockSpec((tn, tr), lambda i, j, k: (j, k))
    extra, extra_specs, epi = (), (), None
    if add is not None:
        extra, extra_specs = (add,), (pl.BlockSpec((tm, tn), lambda i, j, k: (i, j)),)
        epi = lambda acc, r: acc + r[...]
    return _mm(name, a, b, dims=dims, grid=(M // tm, N // tn, nk), a_spec=a_spec, b_spec=b_spec,
               out_spec=pl.BlockSpec((tm, tn), lambda i, j, k: (i, j)),
               out_shape=_sds((M, N), out_dtype), acc_shape=(tm, tn), extra=extra, extra_specs=extra_specs, epilogue=epi)


ROWS = 256


def rmsnorm_fwd(name, x, g):
    S, D = x.shape
    tr = min(ROWS, S)

    def body(x_ref, g_ref, h_ref):
        xv = x_ref[...]
        rstd = lax.rsqrt(jnp.mean(xv * xv, axis=-1, keepdims=True) + EPS)
        h_ref[...] = (xv * rstd * g_ref[...]).astype(BF16)

    return pl.pallas_call(
        body, name=name, grid=(S // tr,),
        in_specs=[pl.BlockSpec((tr, D), lambda i: (i, 0)), pl.BlockSpec((1, D), lambda i: (0, 0))],
        out_specs=pl.BlockSpec((tr, D), lambda i: (i, 0)), out_shape=_sds((S, D), BF16),
        compiler_params=_cparams(("parallel",)),
    )(x, g)


def rmsnorm_bwd(name, x, g, dh, dres):
    S, D = x.shape
    tr = min(ROWS, S)

    def body(x_ref, g_ref, dh_ref, dres_ref, dx_ref, dxb_ref, dg_ref):
        xv = x_ref[...]
        rstd = lax.rsqrt(jnp.mean(xv * xv, axis=-1, keepdims=True) + EPS)
        xh = xv * rstd
        dy = dh_ref[...]
        dxh = dy * g_ref[...]
        dx = dres_ref[...] + rstd * (dxh - xh * jnp.mean(dxh * xh, axis=-1, keepdims=True))
        dx_ref[...] = dx
        dxb_ref[...] = dx.astype(BF16)

        @pl.when(pl.program_id(0) == 0)
        def _():
            dg_ref[...] = jnp.zeros_like(dg_ref)

        dg_ref[...] += jnp.sum(dy * xh, axis=0, keepdims=True)

    row = pl.BlockSpec((tr, D), lambda i: (i, 0))
    vec = pl.BlockSpec((1, D), lambda i: (0, 0))
    return pl.pallas_call(
        body, name=name, grid=(S // tr,), in_specs=[row, vec, row, row], out_specs=[row, row, vec],
        out_shape=[_sds((S, D), F32), _sds((S, D), BF16), _sds((1, D), F32)],
        compiler_params=_cparams(("arbitrary",)),
    )(x, g, dh, dres)


def final_loss(name, x, g, target):
    S, D = x.shape
    tr = min(ROWS, S)

    def body(x_ref, g_ref, t_ref, loss_ref, dx_ref, dxb_ref, dg_ref):
        xv = x_ref[...]
        gv = g_ref[...]
        rstd = lax.rsqrt(jnp.mean(xv * xv, axis=-1, keepdims=True) + EPS)
        xh = xv * rstd
        err = xh * gv - t_ref[...]
        dy = err * (1.0 / D)
        dxh = dy * gv
        dx = rstd * (dxh - xh * jnp.mean(dxh * xh, axis=-1, keepdims=True))
        dx_ref[...] = dx
        dxb_ref[...] = dx.astype(BF16)

        @pl.when(pl.program_id(0) == 0)
        def _():
            dg_ref[...] = jnp.zeros_like(dg_ref)
            loss_ref[...] = jnp.zeros_like(loss_ref)

        dg_ref[...] += jnp.sum(dy * xh, axis=0, keepdims=True)
        loss_ref[...] += 0.5 * jnp.sum(jnp.mean(err * err, axis=-1, keepdims=True))

    row = pl.BlockSpec((tr, D), lambda i: (i, 0))
    vec = pl.BlockSpec((1, D), lambda i: (0, 0))
    return pl.pallas_call(
        body, name=name, grid=(S // tr,), in_specs=[row, vec, row],
        out_specs=[pl.BlockSpec((SUBLANES, LANES), lambda i: (0, 0)), row, row, vec],
        out_shape=[_sds((SUBLANES, LANES), F32), _sds((S, D), F32), _sds((S, D), BF16), _sds((1, D), F32)],
        compiler_params=_cparams(("arbitrary",)),
    )(x, g, target)


def _gelu(x):
    return 0.5 * x * (1.0 + lax.erf(x * 0.7071067811865476))


def _gelu_grad(x):
    return 0.5 * (1.0 + lax.erf(x * 0.7071067811865476)) + x * jnp.exp(-0.5 * x * x) * 0.3989422804014327


def _gm_common(zp, lng, lnb, D):
    z = _gelu(zp)
    u, v = z[:, :D], z[:, D:]
    xc = v - jnp.mean(v, axis=-1, keepdims=True)
    rstd = lax.rsqrt(jnp.mean(xc * xc, axis=-1, keepdims=True) + EPS)
    xh = xc * rstd
    return u, xh, rstd, xh * lng + lnb


def gm_mid_fwd(name, zp, lng, lnb, wm, bT):
    S, D2 = zp.shape
    D = D2 // 2
    dg = D // GM_GROUPS
    P = GM_BLOCK

    def body(z_ref, lng_ref, lnb_ref, wm_ref, bT_ref, o_ref):
        u, _, _, vn = _gm_common(z_ref[...].astype(F32), lng_ref[...], lnb_ref[...], D)
        vnb = vn.astype(BF16)
        for gi in range(GM_GROUPS):
            cols = slice(gi * dg, (gi + 1) * dg)
            mixed = jnp.dot(wm_ref[gi], vnb[:, cols], preferred_element_type=F32) + bT_ref[:, gi:gi + 1]
            o_ref[:, cols] = (u[:, cols] * mixed).astype(BF16)

    vec = pl.BlockSpec((1, D), lambda i: (0, 0))
    return pl.pallas_call(
        body, name=name, grid=(S // P,),
        in_specs=[pl.BlockSpec((P, D2), lambda i: (i, 0)), vec, vec,
                  pl.BlockSpec((GM_GROUPS, P, P), lambda i: (0, 0, 0)), pl.BlockSpec((P, GM_GROUPS), lambda i: (0, 0))],
        out_specs=pl.BlockSpec((P, D), lambda i: (i, 0)), out_shape=_sds((S, D), BF16),
        compiler_params=_cparams(("parallel",)),
    )(zp, lng, lnb, wm, bT)


def gm_mid_bwd(name, zp, dgated, lng, lnb, wm, wmT, bT):
    S, D2 = zp.shape
    D = D2 // 2
    dg = D // GM_GROUPS
    P = GM_BLOCK

    def body(z_ref, dgt_ref, lng_ref, lnb_ref, wm_ref, wmT_ref, bT_ref, dz_ref, dlng_ref, dlnb_ref, dw_ref, dbT_ref, dvn_ref):
        @pl.when(pl.program_id(0) == 0)
        def _():
            dlng_ref[...] = jnp.zeros_like(dlng_ref)
            dlnb_ref[...] = jnp.zeros_like(dlnb_ref)
            dw_ref[...] = jnp.zeros_like(dw_ref)
            dbT_ref[...] = jnp.zeros_like(dbT_ref)

        zp_v = z_ref[...].astype(F32)
        lng_v = lng_ref[...]
        u, xh, rstd, vn = _gm_common(zp_v, lng_v, lnb_ref[...], D)
        vnb = vn.astype(BF16)
        dgt = dgt_ref[...].astype(F32)
        for gi in range(GM_GROUPS):
            cols = slice(gi * dg, (gi + 1) * dg)
            mixed = jnp.dot(wm_ref[gi], vnb[:, cols], preferred_element_type=F32) + bT_ref[:, gi:gi + 1]
            dm = dgt[:, cols] * u[:, cols]
            dmb = dm.astype(BF16)
            dz_ref[:, cols] = (dgt[:, cols] * mixed * _gelu_grad(zp_v[:, cols])).astype(BF16)
            dbT_ref[:, gi:gi + 1] += jnp.sum(dm, axis=1, keepdims=True)
            dw_ref[gi] += lax.dot_general(dmb, vnb[:, cols], (NT, ((), ())), preferred_element_type=F32)
            dvn_ref[:, cols] = jnp.dot(wmT_ref[gi], dmb, preferred_element_type=F32)
        dvn = dvn_ref[...]
        dlng_ref[...] += jnp.sum(dvn * xh, axis=0, keepdims=True)
        dlnb_ref[...] += jnp.sum(dvn, axis=0, keepdims=True)
        dyg = dvn * lng_v
        dv = rstd * (dyg - jnp.mean(dyg, axis=-1, keepdims=True) - xh * jnp.mean(dyg * xh, axis=-1, keepdims=True))
        dz_ref[:, D:] = (dv * _gelu_grad(zp_v[:, D:])).astype(BF16)

    vec = pl.BlockSpec((1, D), lambda i: (0, 0))
    wsp = pl.BlockSpec((GM_GROUPS, P, P), lambda i: (0, 0, 0))
    bsp = pl.BlockSpec((P, GM_GROUPS), lambda i: (0, 0))
    return pl.pallas_call(
        body, name=name, grid=(S // P,),
        in_specs=[pl.BlockSpec((P, D2), lambda i: (i, 0)), pl.BlockSpec((P, D), lambda i: (i, 0)), vec, vec, wsp, wsp, bsp],
        out_specs=[pl.BlockSpec((P, D2), lambda i: (i, 0)), vec, vec, wsp, bsp],
        out_shape=[_sds((S, D2), BF16), _sds((1, D), F32), _sds((1, D), F32), _sds((GM_GROUPS, P, P), F32), _sds((P, GM_GROUPS), F32)],
        scratch_shapes=[pltpu.VMEM((P, D), F32)],
        compiler_params=_cparams(("arbitrary",)),
    )(zp, dgated, lng, lnb, wm, wmT, bT)


def _gla_gate(lr, w2, ba, tri):
    logit = jnp.dot(lr, w2, preferred_element_type=F32) + ba
    la = (jnp.minimum(logit, 0.0) - jnp.log1p(jnp.exp(-jnp.abs(logit)))) * (1.0 / GATE_TAU)
    g = jnp.dot(tri, la, preferred_element_type=F32, precision=HIGHEST)
    return logit, g


def gla_scan_fwd(name, proj, lr, w2p, ba, tri):
    S, D3 = proj.shape
    D = D3 // 3
    H, C = GLA_HEADS, CHUNK
    dk, dv = D // 2 // H, D // H
    NC = S // C
    scale = dk ** -0.5

    def body(q_ref, k_ref, v_ref, lr_ref, w2_ref, ba_ref, tri_ref, o_ref, st_ref, state):
        @pl.when(pl.program_id(0) == 0)
        def _():
            state[...] = jnp.zeros_like(state)

        _, g = _gla_gate(lr_ref[...], w2_ref[...], ba_ref[...], tri_ref[...])
        gend = g[C - 1:C, :]
        kdec = (k_ref[...].astype(F32) * jnp.exp(gend - g)).astype(BF16)
        dec = jnp.exp(gend)
        qs = (q_ref[...].astype(F32) * scale).astype(BF16)
        for hh in range(H):
            kc, vc = slice(hh * dk, (hh + 1) * dk), slice(hh * dv, (hh + 1) * dv)
            kv = lax.dot_general(v_ref[:, vc], kdec[:, kc], (TN, ((), ())), preferred_element_type=F32)
            new = dec[:, kc] * state[hh] + kv
            state[hh] = new
            nb = new.astype(BF16)
            st_ref[hh] = nb
            o_ref[:, vc] = lax.dot_general(qs[:, kc], nb, (NT, ((), ())), preferred_element_type=F32).astype(BF16)

    return pl.pallas_call(
        body, name=name, grid=(NC,),
        in_specs=[pl.BlockSpec((C, D // 2), lambda t: (t, 0)), pl.BlockSpec((C, D // 2), lambda t: (t, 1)),
                  pl.BlockSpec((C, D), lambda t: (t, 1)), pl.BlockSpec((C, LOW), lambda t: (t, 0)),
                  pl.BlockSpec((LOW, D // 2), lambda t: (0, 0)), pl.BlockSpec((1, D // 2), lambda t: (0, 0)),
                  pl.BlockSpec((C, C), lambda t: (0, 0))],
        out_specs=[pl.BlockSpec((C, D), lambda t: (t, 0)), pl.BlockSpec((None, H, dv, dk), lambda t: (t, 0, 0, 0))],
        out_shape=[_sds((S, D), BF16), _sds((NC, H, dv, dk), BF16)],
        scratch_shapes=[pltpu.VMEM((H, dv, dk), F32)],
        compiler_params=_cparams(("arbitrary",)),
    )(proj, proj, proj, lr, w2p, ba, tri)


def gla_scan_bwd(name, proj, lr, w2p, ba, tri, triT, states, do, dr):
    S, D3 = proj.shape
    D = D3 // 3
    H, C = GLA_HEADS, CHUNK
    dk, dv = D // 2 // H, D // H
    NC = S // C
    scale = dk ** -0.5

    def body(q_ref, k_ref, v_ref, lr_ref, w2_ref, ba_ref, tri_ref, triT_ref, st_ref, sp_ref, do_ref, dr_ref,
             dp_ref, dl_ref, dba_ref, dstate, dkd_ref, ddec_ref):
        t = pl.program_id(0)
        dp_ref[:, 2 * D:] = dr_ref[...]

        @pl.when(t == 0)
        def _():
            dstate[...] = jnp.zeros_like(dstate)
            dba_ref[...] = jnp.zeros_like(dba_ref)

        logit, g = _gla_gate(lr_ref[...], w2_ref[...], ba_ref[...], tri_ref[...])
        gend = g[C - 1:C, :]
        e = jnp.exp(gend - g)
        kf = k_ref[...].astype(F32)
        kdec = (kf * e).astype(BF16)
        dec = jnp.exp(gend)
        qs = (q_ref[...].astype(F32) * scale).astype(BF16)
        has_prev = (t < NC - 1).astype(F32)
        for hh in range(H):
            kc, vc = slice(hh * dk, (hh + 1) * dk), slice(hh * dv, (hh + 1) * dv)
            dob = do_ref[:, vc]
            dp_ref[:, kc] = (jnp.dot(dob, st_ref[hh], preferred_element_type=F32) * scale).astype(BF16)
            ds = dstate[hh] + lax.dot_general(dob, qs[:, kc], (TN, ((), ())), preferred_element_type=F32)
            dsb = ds.astype(BF16)
            dkd_ref[:, kc] = jnp.dot(v_ref[:, vc], dsb, preferred_element_type=F32)
            dp_ref[:, D + hh * dv:D + (hh + 1) * dv] = lax.dot_general(kdec[:, kc], dsb, (NT, ((), ())),
                                                                    preferred_element_type=F32).astype(BF16)
            ddec_ref[:, kc] = jnp.sum(ds * sp_ref[hh].astype(F32), axis=0, keepdims=True) * has_prev
            dstate[hh] = dec[:, kc] * ds
        dkdec = dkd_ref[...]
        dp_ref[:, D // 2:D] = (dkdec * e).astype(BF16)
        dd = dkdec * kf * e
        dgend = jnp.sum(dd, axis=0, keepdims=True) + ddec_ref[...] * dec
        last = lax.broadcasted_iota(jnp.int32, (C, 1), 0) == C - 1
        dg = jnp.where(last, dgend, 0.0) - dd
        dla = jnp.dot(triT_ref[...], dg, preferred_element_type=F32, precision=HIGHEST)
        dlogit = dla * (1.0 / GATE_TAU) * (1.0 - jax.nn.sigmoid(logit))
        dl_ref[...] = dlogit.astype(BF16)
        dba_ref[...] += jnp.sum(dlogit, axis=0, keepdims=True)

    rev = lambda t: NC - 1 - t
    half = pl.BlockSpec((C, D // 2), lambda t: (rev(t), 0))
    full = pl.BlockSpec((C, D), lambda t: (rev(t), 0))
    return pl.pallas_call(
        body, name=name, grid=(NC,),
        in_specs=[half, pl.BlockSpec((C, D // 2), lambda t: (rev(t), 1)), pl.BlockSpec((C, D), lambda t: (rev(t), 1)),
                  pl.BlockSpec((C, LOW), lambda t: (rev(t), 0)),
                  pl.BlockSpec((LOW, D // 2), lambda t: (0, 0)), pl.BlockSpec((1, D // 2), lambda t: (0, 0)),
                  pl.BlockSpec((C, C), lambda t: (0, 0)), pl.BlockSpec((C, C), lambda t: (0, 0)),
                  pl.BlockSpec((None, H, dv, dk), lambda t: (rev(t), 0, 0, 0)),
                  pl.BlockSpec((None, H, dv, dk), lambda t: (jnp.maximum(rev(t) - 1, 0), 0, 0, 0)),
                  full, full],
        out_specs=[pl.BlockSpec((C, D3), lambda t: (rev(t), 0)), half, pl.BlockSpec((1, D // 2), lambda t: (0, 0))],
        out_shape=[_sds((S, D3), BF16), _sds((S, D // 2), BF16), _sds((1, D // 2), F32)],
        scratch_shapes=[pltpu.VMEM((H, dv, dk), F32), pltpu.VMEM((C, D // 2), F32), pltpu.VMEM((1, D // 2), F32)],
        compiler_params=_cparams(("arbitrary",)),
    )(proj, proj, proj, lr, w2p, ba, tri, triT, states, states, do, dr)


def _gla_post_common(o, r, ng):
    rs = lax.rsqrt(jnp.mean(o * o, axis=-1, keepdims=True) + EPS)
    oh = o * rs
    sig = jax.nn.sigmoid(r)
    return rs, oh, oh * ng, sig, r * sig


def gla_post_fwd(name, o_raw, proj, ng):
    S, D = o_raw.shape
    dv = D // GLA_HEADS
    tr = min(ROWS, S)

    def body(o_ref, r_ref, ng_ref, og_ref):
        for hh in range(GLA_HEADS):
            cols = slice(hh * dv, (hh + 1) * dv)
            _, _, on, _, sil = _gla_post_common(o_ref[:, cols].astype(F32), r_ref[:, cols].astype(F32), ng_ref[:, cols])
            og_ref[:, cols] = (on * sil).astype(BF16)

    row = pl.BlockSpec((tr, D), lambda i: (i, 0))
    return pl.pallas_call(
        body, name=name, grid=(S // tr,),
        in_specs=[row, pl.BlockSpec((tr, D), lambda i: (i, 2)), pl.BlockSpec((1, D), lambda i: (0, 0))],
        out_specs=row, out_shape=_sds((S, D), BF16), compiler_params=_cparams(("parallel",)),
    )(o_raw, proj, ng)


def gla_post_bwd(name, dog, o_raw, proj, ng):
    S, D = o_raw.shape
    dv = D // GLA_HEADS
    tr = min(ROWS, S)

    def body(dog_ref, o_ref, r_ref, ng_ref, do_ref, dr_ref, dng_ref):
        @pl.when(pl.program_id(0) == 0)
        def _():
            dng_ref[...] = jnp.zeros_like(dng_ref)

        for hh in range(GLA_HEADS):
            cols = slice(hh * dv, (hh + 1) * dv)
            r = r_ref[:, cols].astype(F32)
            ngv = ng_ref[:, cols]
            rs, oh, on, sig, sil = _gla_post_common(o_ref[:, cols].astype(F32), r, ngv)
            dogv = dog_ref[:, cols].astype(F32)
            don = dogv * sil
            dr_ref[:, cols] = (dogv * on * (sig * (1.0 + r * (1.0 - sig)))).astype(BF16)
            dng_ref[:, cols] += jnp.sum(don * oh, axis=0, keepdims=True)
            doh = don * ngv
            do_ref[:, cols] = (rs * (doh - oh * jnp.mean(doh * oh, axis=-1, keepdims=True))).astype(BF16)

    row = pl.BlockSpec((tr, D), lambda i: (i, 0))
    vec = pl.BlockSpec((1, D), lambda i: (0, 0))
    return pl.pallas_call(
        body, name=name, grid=(S // tr,),
        in_specs=[row, row, pl.BlockSpec((tr, D), lambda i: (i, 2)), vec],
        out_specs=[row, row, vec], out_shape=[_sds((S, D), BF16), _sds((S, D), BF16), _sds((1, D), F32)],
        compiler_params=_cparams(("arbitrary",)),
    )(dog, o_raw, proj, ng)


def cast_into(name, w, w_off, rows, chip_idx):
    W = w.shape[1]
    tr = _rows(rows, 512)

    def body(p_ref, w_ref, o_ref):
        o_ref[...] = w_ref[...].astype(BF16)

    return pl.pallas_call(
        body, name=name,
        grid_spec=pltpu.PrefetchScalarGridSpec(
            num_scalar_prefetch=1, grid=(rows // tr,),
            in_specs=[pl.BlockSpec((tr, W), lambda i, p: (w_off // tr + i, 0))],
            out_specs=pl.BlockSpec((None, tr, W), lambda i, p: (p[0], i, 0))),
        out_shape=_sds((N_CHIPS, rows, W), BF16), compiler_params=_cparams(("parallel",)),
    )(chip_idx, w)


def pair_add(name, gbuf, rsib, c_idx):
    _, R, W = gbuf.shape
    hr = R // 2
    tr = _rows(hr, 512)
    nb = hr // tr

    def body(c_ref, a_ref, b_ref, o_ref):
        o_ref[...] = (a_ref[...].astype(F32) + b_ref[...].astype(F32)).astype(BF16)

    return pl.pallas_call(
        body, name=name,
        grid_spec=pltpu.PrefetchScalarGridSpec(
            num_scalar_prefetch=1, grid=(N_CHIPS, nb),
            in_specs=[pl.BlockSpec((None, tr, W), lambda s, i, c: (s, c[0] * nb + i, 0)),
                      pl.BlockSpec((None, tr, W), lambda s, i, c: (s, i, 0))],
            out_specs=pl.BlockSpec((None, tr, W), lambda s, i, c: (s, i, 0))),
        out_shape=_sds((N_CHIPS, hr, W), BF16), compiler_params=_cparams(("parallel", "parallel")),
    )(c_idx, gbuf, rsib)


def sum_chips(name, p, q, chip_idx, c_idx):
    _, hr, W = p.shape
    tr = _rows(hr, 512)
    nb = hr // tr

    def body(s_ref, c_ref, p_ref, q0_ref, q1_ref, q2_ref, o_ref):
        o_ref[...] = ((p_ref[...].astype(F32) + q0_ref[...].astype(F32)) + q1_ref[...].astype(F32)) + q2_ref[...].astype(F32)

    def qspec(j):
        return pl.BlockSpec((None, tr, W), lambda i, s, c: (j, i, 0))

    return pl.pallas_call(
        body, name=name,
        grid_spec=pltpu.PrefetchScalarGridSpec(
            num_scalar_prefetch=2, grid=(nb,),
            in_specs=[pl.BlockSpec((None, tr, W), lambda i, s, c: (s[0], i, 0)), qspec(0), qspec(1), qspec(2)],
            out_specs=pl.BlockSpec((tr, W), lambda i, s, c: (c[0] * nb + i, 0))),
        out_shape=_sds((2 * hr, W), F32), compiler_params=_cparams(("parallel",)),
    )(chip_idx, c_idx, p, q, q, q)


def sum_devices(name, parts):
    _, R, W = parts.shape

    def body(p_ref, o_ref):
        acc = p_ref[0]
        for d in range(1, N_DEV):
            acc = acc + p_ref[d]
        o_ref[...] = acc

    tr = _rows(R, 512)
    return pl.pallas_call(
        body, name=name, grid=(R // tr,), in_specs=[pl.BlockSpec((N_DEV, tr, W), lambda i: (0, i, 0))],
        out_specs=pl.BlockSpec((tr, W), lambda i: (i, 0)), out_shape=_sds((R, W), F32),
        compiler_params=_cparams(("parallel",)),
    )(parts)


def adamw(name, g, w, m, v, w_off, prev=None):
    rows, W = g.shape
    tr = _rows(rows, 256)

    def body(*refs):
        g_ref, w_ref, m_ref, v_ref = refs[:4]
        go_ref, d_ref, mo_ref, vo_ref = refs[-4:]
        gv = g_ref[...]
        mn = ADAM_B1 * m_ref[...] + (1.0 - ADAM_B1) * gv
        vn = ADAM_B2 * v_ref[...] + (1.0 - ADAM_B2) * (gv * gv)
        m_hat = mn / (1.0 - ADAM_B1 ** ADAM_STEP)
        v_hat = vn / (1.0 - ADAM_B2 ** ADAM_STEP)
        go_ref[...] = gv
        d_ref[...] = -ADAM_LR * (m_hat / (jnp.sqrt(v_hat) + ADAM_EPS) + ADAM_WD * w_ref[...])
        mo_ref[...] = mn
        vo_ref[...] = vn

    blk = pl.BlockSpec((tr, W), lambda i: (w_off // tr + i, 0))
    inputs = [g, w, m, v]
    in_specs = [pl.BlockSpec((tr, W), lambda i: (i, 0)), blk, blk, blk]
    aliases = {}
    if prev is not None:
        aliases = {4 + t: t for t in range(4)}
        inputs += list(prev)
        in_specs += [ANY] * 4
    return pl.pallas_call(
        body, name=name, grid=(rows // tr,), in_specs=in_specs, out_specs=[blk] * 4,
        out_shape=[_sds(w.shape, F32)] * 4, input_output_aliases=aliases, compiler_params=_cparams(("parallel",)),
    )(*inputs)


def _pack(arrs):
    flat = jnp.concatenate([a.reshape(-1).astype(F32) for a in arrs])
    tile = SUBLANES * LANES * 2
    pad = (-flat.shape[0]) % tile
    return jnp.pad(flat, (0, pad)).reshape(-1, LANES)


def _unpack(packed, shapes):
    flat = packed.reshape(-1)
    out, pos = [], 0
    for s in shapes:
        n = 1
        for d in s:
            n *= d
        out.append(flat[pos:pos + n].reshape(s))
        pos += n
    return out


def kernel(x, norm_mix_g, norm_ffn_g, final_g, gm_w_in, gm_ln_g, gm_ln_b, gm_w_s, gm_b_s, gm_w_out, gla_w_in, gla_w_a1, gla_w_a2, gla_b_a, gla_norm_g, gla_w_o, ffn_w_up, ffn_w_down, loss_target, m_norm_mix_g, m_norm_ffn_g, m_final_g, m_gm_w_in, m_gm_ln_g, m_gm_ln_b, m_gm_w_s, m_gm_b_s, m_gm_w_out, m_gla_w_in, m_gla_w_a1, m_gla_w_a2, m_gla_b_a, m_gla_norm_g, m_gla_w_o, m_ffn_w_up, m_ffn_w_down, v_norm_mix_g, v_norm_ffn_g, v_final_g, v_gm_w_in, v_gm_ln_g, v_gm_ln_b, v_gm_w_s, v_gm_b_s, v_gm_w_out, v_gla_w_in, v_gla_w_a1, v_gla_w_a2, v_gla_b_a, v_gla_norm_g, v_gla_w_o, v_ffn_w_up, v_ffn_w_down):
    S, D = x.shape[1], x.shape[2]
    depth = norm_mix_g.shape[0]
    n_gm, n_gla = gm_w_in.shape[0], gla_w_in.shape[0]
    F = 4 * D
    P = GM_BLOCK
    xi, yi, ci = lax.axis_index("x"), lax.axis_index("y"), lax.axis_index("c")
    chip = 2 * xi + yi
    chip_idx = jnp.reshape(chip, (1,)).astype(jnp.int32)
    c_idx = jnp.reshape(ci, (1,)).astype(jnp.int32)

    is_gm = [i % 2 == 0 for i in range(depth)]
    w_mix_in = [(gm_w_in, m_gm_w_in, v_gm_w_in) if is_gm[i] else (gla_w_in, m_gla_w_in, v_gla_w_in) for i in range(depth)]
    w_mix_out = [(gm_w_out, m_gm_w_out, v_gm_w_out) if is_gm[i] else (gla_w_o, m_gla_w_o, v_gla_w_o) for i in range(depth)]
    w_up = (ffn_w_up, m_ffn_w_up, v_ffn_w_up)
    w_down = (ffn_w_down, m_ffn_w_down, v_ffn_w_down)

    def flat2(w):
        return w.reshape(w.shape[0] * w.shape[1], w.shape[2])

    B = {}

    for i in range(depth):
        j = i // 2
        B[f"w_in_{i}"] = cast_into(f"cast_in_{i}", flat2(w_mix_in[i][0]), j * D, D, chip_idx)
        B[f"w_out_{i}"] = cast_into(f"cast_out_{i}", flat2(w_mix_out[i][0]), j * (D // 4), D // 4, chip_idx)
        B[f"w_up_{i}"] = cast_into(f"cast_up_{i}", flat2(ffn_w_up), i * D, D, chip_idx)
        B[f"w_down_{i}"] = cast_into(f"cast_down_{i}", flat2(ffn_w_down), i * D, D, chip_idx)
    run_comm("allgather_mixer0", comm_gather_all(B, ["w_in_0", "w_out_0"]), B)

    small_w = [gla_w_a1, gla_w_a2, gla_b_a, gla_norm_g]
    gs = allgather_devices("allgather_small_weights", _pack(small_w))
    per_chip = [_unpack(gs[2 * s], [a.shape for a in small_w]) for s in range(N_CHIPS)]
    w_a1 = jnp.concatenate([p[0] for p in per_chip], axis=1)
    w_a2 = jnp.concatenate([p[1] for p in per_chip], axis=2)
    b_a = jnp.concatenate([p[2] for p in per_chip], axis=1)
    gnorm = jnp.concatenate([p[3] for p in per_chip], axis=1)
    w_a1p = jnp.pad(w_a1, ((0, 0), (0, 0), (0, LOW - GATE_RANK))).astype(BF16)
    w_a2p = jnp.pad(w_a2, ((0, 0), (0, LOW - GATE_RANK), (0, 0))).astype(BF16)

    chunk_id = jnp.arange(P) // CHUNK
    mask = chunk_id[None, :] <= chunk_id[:, None]
    wm_all = jnp.where(mask[None, None], gm_w_s, 0.0)
    tri = jnp.tril(jnp.ones((CHUNK, CHUNK), F32))
    triT = tri.T

    def ici(*keys):
        return comm_gather_ici(B, list(keys))

    def fwd(*keys):
        return comm_gather_forward(B, list(keys))

    xs = x[0]
    saved = []
    for i in range(depth):
        j = i // 2
        nxt = i + 1 < depth
        c_in = ici("w_up_0") if i == 0 else comm_merge(fwd(f"w_up_{i}"), ici(f"w_down_{i}"))
        c_out = comm_merge(fwd("w_up_0"), ici("w_down_0")) if i == 0 else fwd(f"w_down_{i}")
        c_up = comm_merge(fwd("w_down_0") if i == 0 else None, ici(f"w_in_{i + 1}", f"w_out_{i + 1}") if nxt else None)
        c_down = comm_merge(fwd(f"w_in_{i + 1}", f"w_out_{i + 1}"), ici(f"w_up_{i + 1}")) if nxt else None
        h1 = rmsnorm_fwd(f"norm_mix_{i}", xs, norm_mix_g[i][None])
        if is_gm[i]:
            zp = mm_fwd_col(f"gm_in_{i}", h1, B[f"w_in_{i}"], 2 * D, comm=c_in, B=B)
            wm = wm_all[j].astype(BF16)
            gated = gm_mid_fwd(f"gm_mid_{i}", zp, gm_ln_g[j][None], gm_ln_b[j][None], wm, gm_b_s[j].T)
            x_mid = mm_fwd_row(f"gm_out_{i}", gated, B[f"w_out_{i}"], xs, comm=c_out, B=B)
            mix = (h1, zp, gated)
        else:
            proj = mm_fwd_col(f"gla_in_{i}", h1, B[f"w_in_{i}"], 3 * D, comm=c_in, B=B)
            lr = mm_plain(f"gla_low_{i}", h1, w_a1p[j], NN, BF16)
            o_raw, states = gla_scan_fwd(f"gla_scan_{i}", proj, lr, w_a2p[j], b_a[j][None], tri)
            og = gla_post_fwd(f"gla_post_{i}", o_raw, proj, gnorm[j][None])
            x_mid = mm_fwd_row(f"gla_out_{i}", og, B[f"w_out_{i}"], xs, comm=c_out, B=B)
            mix = (h1, proj, lr, o_raw, states, og)
        h2 = rmsnorm_fwd(f"norm_ffn_{i}", x_mid, norm_ffn_g[i][None])
        act = mm_fwd_col(f"ffn_up_{i}", h2, B[f"w_up_{i}"], F, epilogue=lambda acc: jnp.maximum(acc, 0.0), comm=c_up, B=B)
        x_out = mm_fwd_row(f"ffn_down_{i}", act, B[f"w_down_{i}"], x_mid, prologue=lambda a: a * a, comm=c_down, B=B)
        saved.append((xs, x_mid, h2, act, mix))
        xs = x_out

    loss_part, dx, dxb, d_final_g = final_loss("final_loss", xs, final_g[None], loss_target[0])
    loss = lax.psum(loss_part[0, 0], ("x", "y", "c"))

    def exchange(i, kinds):
        return comm_exchange(B, [f"d_{k}_{i}" for k in kinds], [f"r_{k}_{i}" for k in kinds])

    def scatter(i, kinds):
        return comm_scatter(B, [f"p_{k}_{i}" for k in kinds], [f"q_{k}_{i}" for k in kinds])

    def join(i, kinds):
        return comm_join(B, [f"f_{k}_{i}" for k in kinds])

    def pair_sums(i, kinds):
        for k in kinds:
            B[f"p_{k}_{i}"] = pair_add(f"pair_add_{k}_{i}", B[f"d_{k}_{i}"], B[f"r_{k}_{i}"], c_idx)

    def chip_sums(i, kinds):
        for k in kinds:
            B[f"f_{k}_{i}"] = sum_chips(f"sum_chips_{k}_{i}", B[f"p_{k}_{i}"], B[f"q_{k}_{i}"], chip_idx, c_idx)

    FFN, MIX = ("up", "down"), ("out", "in")
    d_mix_g, d_ffn_g = [None] * depth, [None] * depth
    d_ln_g, d_ln_b, d_w_s, d_b_s = [None] * n_gm, [None] * n_gm, [None] * n_gm, [None] * n_gm
    d_a1, d_a2, d_ba, d_gn = [None] * n_gla, [None] * n_gla, [None] * n_gla, [None] * n_gla
    for i in reversed(range(depth)):
        j = i // 2
        up = i + 1 < depth
        x_in, x_mid, h2, act, mix = saved[i]
        com = comm_merge(join(i + 1, FFN), exchange(i + 1, MIX)) if up else None
        d_apre = mm_bwd_row_x(f"ffn_down_dx_{i}", dxb, B[f"w_down_{i}"], mul=act, comm=com, B=B)
        if up:
            pair_sums(i + 1, MIX)
        B[f"d_down_{i}"] = mm_bwd_row_w(f"ffn_down_dw_{i}", act, dxb, prologue=lambda a: a * a,
                                        comm=scatter(i + 1, MIX) if up else None, B=B)
        if up:
            chip_sums(i + 1, MIX)
        B[f"d_up_{i}"] = mm_bwd_col_w(f"ffn_up_dw_{i}", h2, d_apre, comm=join(i + 1, MIX) if up else None, B=B)
        dh2 = mm_bwd_col_x(f"ffn_up_dx_{i}", d_apre, B[f"w_up_{i}"])
        dx, dxb, d_ffn_g[i] = rmsnorm_bwd(f"norm_ffn_bwd_{i}", x_mid, norm_ffn_g[i][None], dh2, dx)
        if is_gm[i]:
            h1, zp, gated = mix
            d_gated = mm_bwd_row_x(f"gm_out_dx_{i}", dxb, B[f"w_out_{i}"], comm=exchange(i, FFN), B=B)
            pair_sums(i, FFN)
            B[f"d_out_{i}"] = mm_bwd_row_w(f"gm_out_dw_{i}", gated, dxb)
            wm = wm_all[j].astype(BF16)
            wmT = jnp.swapaxes(wm_all[j], 1, 2).astype(BF16)
            dzp, d_ln_g[j], d_ln_b[j], dw, dbT = gm_mid_bwd(f"gm_mid_bwd_{i}", zp, d_gated, gm_ln_g[j][None], gm_ln_b[j][None],
                                                             wm, wmT, gm_b_s[j].T)
            d_w_s[j] = jnp.where(mask[None], dw, 0.0)
            d_b_s[j] = dbT.T
            B[f"d_in_{i}"] = mm_bwd_col_w(f"gm_in_dw_{i}", h1, dzp, comm=scatter(i, ("up",)), B=B)
            dh1 = mm_bwd_col_x(f"gm_in_dx_{i}", dzp, B[f"w_in_{i}"], comm=scatter(i, ("down",)), B=B)
        else:
            h1, proj, lr, o_raw, states, og = mix
            d_og = mm_bwd_row_x(f"gla_out_dx_{i}", dxb, B[f"w_out_{i}"], comm=exchange(i, FFN), B=B)
            pair_sums(i, FFN)
            B[f"d_out_{i}"] = mm_bwd_row_w(f"gla_out_dw_{i}", og, dxb)
            d_oraw, d_r, d_gn[j] = gla_post_bwd(f"gla_post_bwd_{i}", d_og, o_raw, proj, gnorm[j][None])
            dproj, dlogit, d_ba[j] = gla_scan_bwd(f"gla_scan_bwd_{i}", proj, lr, w_a2p[j], b_a[j][None], tri, triT, states, d_oraw, d_r)
            B[f"d_in_{i}"] = mm_bwd_col_w(f"gla_in_dw_{i}", h1, dproj, comm=scatter(i, ("up",)), B=B)
            dh1 = mm_bwd_col_x(f"gla_in_dx_{i}", dproj, B[f"w_in_{i}"], comm=scatter(i, ("down",)), B=B)
            dlr = mm_plain(f"gla_gate_dlow_{i}", dlogit, w_a2p[j], NT, BF16)
            d_a2[j] = mm_plain(f"gla_gate_dw2_{i}", lr, dlogit, TN, F32)[:GATE_RANK]
            d_a1[j] = mm_plain(f"gla_gate_dw1_{i}", h1, dlr, TN, F32)[:, :GATE_RANK]
            dh1 = mm_plain(f"gla_gate_dx_{i}", dlr, w_a1p[j], NT, F32, add=dh1)
        chip_sums(i, FFN)
        dx, dxb, d_mix_g[i] = rmsnorm_bwd(f"norm_mix_bwd_{i}", x_in, norm_mix_g[i][None], dh1, dx)
    grad_x = dx[None]

    small_g = [jnp.concatenate(d_mix_g), jnp.concatenate(d_ffn_g), d_final_g[0], jnp.concatenate(d_ln_g), jnp.concatenate(d_ln_b),
               jnp.stack(d_w_s), jnp.stack(d_b_s), jnp.stack(d_a1), jnp.stack(d_a2), jnp.concatenate(d_ba), jnp.concatenate(d_gn)]
    small_shapes = [(depth, D), (depth, D), (D,), (n_gm, D), (n_gm, D), (n_gm, GM_GROUPS, P, P), (n_gm, GM_GROUPS, P),
                    (n_gla, D, GATE_RANK), (n_gla, GATE_RANK, D // 2), (n_gla, D // 2), (n_gla, D)]
    B["small_g"] = _pack(small_g)

    def small_rows():
        return comm_allgather_rows(B, "small_g", "small_parts", 0, B["small_g"].shape[0], fresh=True)

    res = {}

    def big(kind, i):
        j = i // 2
        key, wmv, w_off = {"up": ("up", w_up, i * D), "down": ("down", w_down, i * D),
                           "out": ("gm_out" if is_gm[i] else "gla_o", w_mix_out[i], j * (D // 4)),
                           "in": ("gm_in" if is_gm[i] else "gla_in", w_mix_in[i], j * D)}[kind]
        w, m, v = (flat2(t) for t in wmv)
        res[key] = adamw(f"adamw_{key}_{i}", B[f"f_{kind}_{i}"], w, m, v, w_off, prev=res.get(key))

    run_comm("join_ffn0_exchange_mixer0", comm_merge(join(0, FFN), exchange(0, MIX)), B)
    pair_sums(0, MIX)
    run_comm("scatter_mixer0_allgather_small", comm_merge(scatter(0, MIX), small_rows()), B)
    chip_sums(0, MIX)
    run_comm("join_mixer0", join(0, MIX), B)
    for i in reversed(range(depth)):
        for kind in ("up", "down", "out", "in"):
            big(kind, i)

    red = _unpack(sum_devices("sum_small_grads", B["small_parts"]), small_shapes)
    g_rep = red[:7]
    g_a1 = lax.dynamic_slice_in_dim(red[7], chip * (D // 4), D // 4, axis=1)
    g_a2 = lax.dynamic_slice_in_dim(red[8], chip * (D // 8), D // 8, axis=2)
    g_ba = lax.dynamic_slice_in_dim(red[9], chip * (D // 8), D // 8, axis=1)
    g_gn = lax.dynamic_slice_in_dim(red[10], chip * (D // 4), D // 4, axis=1)
    g_small = g_rep + [g_a1, g_a2, g_ba, g_gn]
    w_small = [norm_mix_g, norm_ffn_g, final_g, gm_ln_g, gm_ln_b, gm_w_s, gm_b_s, gla_w_a1, gla_w_a2, gla_b_a, gla_norm_g]
    m_small = [m_norm_mix_g, m_norm_ffn_g, m_final_g, m_gm_ln_g, m_gm_ln_b, m_gm_w_s, m_gm_b_s, m_gla_w_a1, m_gla_w_a2, m_gla_b_a, m_gla_norm_g]
    v_small = [v_norm_mix_g, v_norm_ffn_g, v_final_g, v_gm_ln_g, v_gm_ln_b, v_gm_w_s, v_gm_b_s, v_gla_w_a1, v_gla_w_a2, v_gla_b_a, v_gla_norm_g]
    shapes_small = [w.shape for w in w_small]
    sm = adamw("adamw_small", _pack(g_small), _pack(w_small), _pack(m_small), _pack(v_small), 0)
    sm = [_unpack(o, shapes_small) for o in sm]

    def shaped(key, like):
        return [o.reshape(like.shape) for o in res[key]]

    o_gm_in, o_gm_out = shaped("gm_in", gm_w_in), shaped("gm_out", gm_w_out)
    o_gla_in, o_gla_o = shaped("gla_in", gla_w_in), shaped("gla_o", gla_w_o)
    o_up, o_down = shaped("up", ffn_w_up), shaped("down", ffn_w_down)

    def ordered(kind):
        s = sm[kind]
        return [s[0], s[1], s[2], o_gm_in[kind], s[3], s[4], s[5], s[6], o_gm_out[kind], o_gla_in[kind],
                s[7], s[8], s[9], s[10], o_gla_o[kind], o_up[kind], o_down[kind]]

    return (loss, grad_x, *ordered(0), *ordered(1), *ordered(2), *ordered(3))
```

```python
import jax
import jax.numpy as jnp
from jax import lax
from jax.experimental import pallas as pl
from jax.experimental.pallas import tpu as pltpu

F32 = jnp.float32
BF16 = jnp.bfloat16

EPS = 1e-6
CHUNK = 64
GM_BLOCK = 128
GM_GROUPS = 8
GLA_HEADS = 4
GATE_RANK = 16
GATE_TAU = 16.0
LOW = 128
N_CHIPS = 4
N_DEV = 8

ADAM_LR = 0.001
ADAM_B1 = 0.9
ADAM_B2 = 0.999
ADAM_EPS = 1e-08
ADAM_WD = 0.01
ADAM_STEP = 10

V7X_VMEM_LIMIT = 48 * 1024 * 1024
LANES = 128
SUBLANES = 8
BF16_ROWS = 16
MESH = pl.DeviceIdType.MESH
HIGHEST = lax.Precision.HIGHEST


def _pick(n, cap):
    if n <= cap:
        return n
    best = LANES
    for t in range(LANES, cap + 1, LANES):
        if n % t == 0:
            best = t
    return best


def _rows(n, cap):
    if n <= cap:
        return n
    best = 0
    for t in range(BF16_ROWS, cap + 1, BF16_ROWS):
        if n % t == 0:
            best = t
    return best if best >= LANES else n


def _cparams(sem=None):
    return pltpu.CompilerParams(dimension_semantics=sem, vmem_limit_bytes=V7X_VMEM_LIMIT)


ANY = pl.BlockSpec(memory_space=pl.ANY)


def _sds(shape, dtype):
    return jax.ShapeDtypeStruct(shape, dtype)


def _place():
    return lax.axis_index("x"), lax.axis_index("y"), lax.axis_index("c")


def _other_chips(x, y):
    return [(1 - x, y), (x, 1 - y), (1 - x, 1 - y)]


class Comm:
    def __init__(self, ins, outs, alias, n_sems, start, wait):
        self.ins, self.outs, self.alias, self.n_sems, self.start, self.wait = ins, outs, alias, n_sems, start, wait


class _Shift:
    def __init__(self, sems, by):
        self.sems, self.by = sems, by

    @property
    def at(self):
        return self

    def __getitem__(self, k):
        return self.sems.at[self.by + k]


def comm_merge(*comms):
    comms = [c for c in comms if c is not None]
    if not comms:
        return None
    if len(comms) == 1:
        return comms[0]
    offs, total = [], 0
    for c in comms:
        offs.append(total)
        total += c.n_sems

    def start(R, ss, rs):
        for c, o in zip(comms, offs):
            c.start(R, _Shift(ss, o), _Shift(rs, o))

    def wait(R, ss, rs):
        for c, o in zip(comms, offs):
            c.wait(R, _Shift(ss, o), _Shift(rs, o))

    return Comm(sum((c.ins for c in comms), []), sum((c.outs for c in comms), []), sum((c.alias for c in comms), []),
                total, start, wait)


def _remote(src, dst, ss, rs, k, to):
    return pltpu.make_async_remote_copy(src_ref=src, dst_ref=dst, send_sem=ss.at[k], recv_sem=rs.at[k],
                                        device_id=to, device_id_type=MESH)


def comm_gather_ici(B, keys):
    hr = {k: B[k].shape[1] // 2 for k in keys}

    def region(R, k, chip, hc):
        return R[k].at[chip, pl.ds(hc * hr[k], hr[k]), :]

    def start(R, ss, rs):
        x, y, c = _place()
        for a, k in enumerate(keys):
            for j, (px, py) in enumerate(_other_chips(x, y)):
                mine = region(R, k, 2 * x + y, c)
                _remote(mine, mine, ss, rs, 3 * a + j, (px, py, c)).start()

    def wait(R, ss, rs):
        x, y, c = _place()
        for a, k in enumerate(keys):
            for j, (px, py) in enumerate(_other_chips(x, y)):
                theirs = region(R, k, 2 * px + py, c)
                _remote(theirs, theirs, ss, rs, 3 * a + j, (px, py, c)).wait_recv()
                mine = region(R, k, 2 * x + y, c)
                _remote(mine, mine, ss, rs, 3 * a + j, (px, py, c)).wait_send()

    return Comm([], [], list(keys), 3 * len(keys), start, wait)


def comm_gather_forward(B, keys):
    hr = {k: B[k].shape[1] // 2 for k in keys}

    def region(R, k, chip, hc):
        return R[k].at[chip, pl.ds(hc * hr[k], hr[k]), :]

    def start(R, ss, rs):
        x, y, c = _place()
        for a, k in enumerate(keys):
            for j, (px, py) in enumerate(_other_chips(x, y)):
                got = region(R, k, 2 * px + py, c)
                _remote(got, got, ss, rs, 3 * a + j, (x, y, 1 - c)).start()

    def wait(R, ss, rs):
        x, y, c = _place()
        for a, k in enumerate(keys):
            for j, (px, py) in enumerate(_other_chips(x, y)):
                other = region(R, k, 2 * px + py, 1 - c)
                _remote(other, other, ss, rs, 3 * a + j, (x, y, 1 - c)).wait_recv()
                got = region(R, k, 2 * px + py, c)
                _remote(got, got, ss, rs, 3 * a + j, (x, y, 1 - c)).wait_send()

    return Comm([], [], list(keys), 3 * len(keys), start, wait)


def comm_gather_all(B, keys):
    ici, fwd = comm_gather_ici(B, keys), comm_gather_forward(B, keys)
    n1 = ici.n_sems

    def wait(R, ss, rs):
        ici.wait(R, ss, rs)
        fwd.start(R, _Shift(ss, n1), _Shift(rs, n1))
        fwd.wait(R, _Shift(ss, n1), _Shift(rs, n1))

    return Comm([], [], list(keys), n1 + fwd.n_sems, ici.start, wait)


def comm_exchange(B, src_keys, out_keys):
    hr = {k: B[k].shape[1] // 2 for k in src_keys}

    def descr(R, ss, rs, a):
        x, y, c = _place()
        k = src_keys[a]
        return _remote(R[k].at[:, pl.ds((1 - c) * hr[k], hr[k]), :], R[out_keys[a]], ss, rs, a, (x, y, 1 - c))

    def start(R, ss, rs):
        for a in range(len(src_keys)):
            descr(R, ss, rs, a).start()

    def wait(R, ss, rs):
        for a in range(len(src_keys)):
            descr(R, ss, rs, a).wait()

    outs = [(o, _sds((N_CHIPS, hr[k], B[k].shape[2]), B[k].dtype)) for k, o in zip(src_keys, out_keys)]
    return Comm(list(src_keys), outs, [], len(src_keys), start, wait)


def comm_scatter(B, p_keys, q_keys):
    def each(R, ss, rs, fn):
        x, y, c = _place()
        for a, (pk, qk) in enumerate(zip(p_keys, q_keys)):
            for j, (px, py) in enumerate(_other_chips(x, y)):
                fn(_remote(R[pk].at[2 * px + py], R[qk].at[j], ss, rs, 3 * a + j, (px, py, c)))

    def start(R, ss, rs):
        each(R, ss, rs, lambda d: d.start())

    def wait(R, ss, rs):
        each(R, ss, rs, lambda d: d.wait())

    outs = [(qk, _sds((3,) + B[pk].shape[1:], B[pk].dtype)) for pk, qk in zip(p_keys, q_keys)]
    return Comm(list(p_keys), outs, [], 3 * len(p_keys), start, wait)


def comm_join(B, keys):
    hr = {k: B[k].shape[0] // 2 for k in keys}

    def region(R, k, hc):
        return R[k].at[pl.ds(hc * hr[k], hr[k]), :]

    def start(R, ss, rs):
        x, y, c = _place()
        for a, k in enumerate(keys):
            mine = region(R, k, c)
            _remote(mine, mine, ss, rs, a, (x, y, 1 - c)).start()

    def wait(R, ss, rs):
        x, y, c = _place()
        for a, k in enumerate(keys):
            other = region(R, k, 1 - c)
            _remote(other, other, ss, rs, a, (x, y, 1 - c)).wait_recv()
            mine = region(R, k, c)
            _remote(mine, mine, ss, rs, a, (x, y, 1 - c)).wait_send()

    return Comm([], [], list(keys), len(keys), start, wait)


def comm_allgather_rows(B, src_key, out_key, r0, nr, fresh):
    def peer(x, y, c, k):
        return (1 - x if k & 4 else x, 1 - y if k & 2 else y, 1 - c if k & 1 else c)

    def start(R, ss, rs):
        x, y, c = _place()
        me = 4 * x + 2 * y + c
        src = R[src_key].at[pl.ds(r0, nr), :]
        dst = R[out_key].at[me, pl.ds(r0, nr), :]
        pltpu.make_async_copy(src, dst, ss.at[N_DEV - 1]).start()
        for k in range(1, N_DEV):
            _remote(src, dst, ss, rs, k - 1, peer(x, y, c, k)).start()

    def wait(R, ss, rs):
        x, y, c = _place()
        me = 4 * x + 2 * y + c
        src = R[src_key].at[pl.ds(r0, nr), :]
        for k in range(1, N_DEV):
            px, py, pc = peer(x, y, c, k)
            theirs = R[out_key].at[4 * px + 2 * py + pc, pl.ds(r0, nr), :]
            _remote(theirs, theirs, ss, rs, k - 1, (px, py, pc)).wait_recv()
            _remote(src, R[out_key].at[me, pl.ds(r0, nr), :], ss, rs, k - 1, (px, py, pc)).wait_send()
        pltpu.make_async_copy(src, R[out_key].at[me, pl.ds(r0, nr), :], ss.at[N_DEV - 1]).wait()

    outs = [(out_key, _sds((N_DEV,) + B[src_key].shape, B[src_key].dtype))] if fresh else []
    return Comm([src_key], outs, [] if fresh else [out_key], N_DEV, start, wait)


def run_comm(name, comm, B):
    n_in, n_out, n_al = len(comm.ins), len(comm.outs), len(comm.alias)

    def body(*refs):
        R = dict(zip(comm.ins, refs[:n_in]))
        R.update(zip([k for k, _ in comm.outs], refs[n_in + n_al:n_in + n_al + n_out]))
        R.update(zip(comm.alias, refs[n_in + n_al + n_out:n_in + n_al + n_out + n_al]))
        ss, rs = refs[-2], refs[-1]
        comm.start(R, ss, rs)
        comm.wait(R, ss, rs)

    res = pl.pallas_call(
        body, name=name, in_specs=[ANY] * (n_in + n_al), out_specs=[ANY] * (n_out + n_al),
        out_shape=[s for _, s in comm.outs] + [_sds(B[k].shape, B[k].dtype) for k in comm.alias],
        input_output_aliases={n_in + t: n_out + t for t in range(n_al)},
        scratch_shapes=[pltpu.SemaphoreType.DMA((comm.n_sems,)), pltpu.SemaphoreType.DMA((comm.n_sems,))],
    )(*[B[k] for k in comm.ins], *[B[k] for k in comm.alias])
    B.update(zip([k for k, _ in comm.outs] + list(comm.alias), res))


def allgather_devices(name, v):
    def body(v_ref, out_ref, send_sems, recv_sems, local_sem):
        x, y, c = _place()
        me = 4 * x + 2 * y + c
        mine = pltpu.make_async_copy(v_ref, out_ref.at[me], local_sem)
        mine.start()

        def peer(k):
            return (1 - x if k & 4 else x, 1 - y if k & 2 else y, 1 - c if k & 1 else c)

        def copy(k, src, dst):
            return pltpu.make_async_remote_copy(src_ref=src, dst_ref=dst, send_sem=send_sems.at[k - 1],
                                                recv_sem=recv_sems.at[k - 1], device_id=peer(k), device_id_type=MESH)

        sends = [copy(k, v_ref, out_ref.at[me]) for k in range(1, N_DEV)]
        for cp in sends:
            cp.start()
        for k in range(1, N_DEV):
            px, py, pc = peer(k)
            them = 4 * px + 2 * py + pc
            copy(k, out_ref.at[them], out_ref.at[them]).wait_recv()
        for cp in sends:
            cp.wait_send()
        mine.wait()

    return pl.pallas_call(
        body, name=name, in_specs=[ANY], out_specs=ANY, out_shape=_sds((N_DEV,) + v.shape, v.dtype),
        scratch_shapes=[pltpu.SemaphoreType.DMA((N_DEV - 1,)), pltpu.SemaphoreType.DMA((N_DEV - 1,)), pltpu.SemaphoreType.DMA],
    )(v)


NN = ((1,), (0,))
NT = ((1,), (1,))
TN = ((0,), (0,))


def _mm(name, a, b, *, dims, grid, a_spec, b_spec, out_spec, out_shape, acc_shape, extra=(), extra_specs=(),
        prologue=None, epilogue=None, b_reshape=None, dot_fn=None, norm_g=None, comm=None, B=None):
    gi, gj, nk = grid
    n_extra = len(extra)
    n_norm = 1 if norm_g is not None else 0
    n_in, n_out, n_al = (len(comm.ins), len(comm.outs), len(comm.alias)) if comm is not None else (0, 0, 0)

    def body(*refs):
        a_ref, b_ref = refs[0], refs[1]
        ex = refs[2:2 + n_extra]
        pos = 2 + n_extra + n_norm
        c_ins = refs[pos:pos + n_in]
        pos += n_in + n_al
        o_ref = refs[pos]
        pos += n_norm
        c_outs = refs[pos + 1:pos + 1 + n_out]
        c_alias = refs[pos + 1 + n_out:pos + 1 + n_out + n_al]
        scratch = refs[pos + 1 + n_out + n_al:]
        i, j, k = pl.program_id(0), pl.program_id(1), pl.program_id(2)
        if comm is not None:
            ss, rs = scratch[-2], scratch[-1]
            R = dict(zip(comm.ins, c_ins))
            R.update(zip([key for key, _ in comm.outs], c_outs))
            R.update(zip(comm.alias, c_alias))

            @pl.when(jnp.logical_and(jnp.logical_and(i == 0, j == 0), k == 0))
            def _():
                comm.start(R, ss, rs)

        if norm_g is not None:
            g_ref, h_ref, hbuf = refs[2 + n_extra], refs[pos], scratch[1 if nk > 1 else 0]

            @pl.when(j == 0)
            def _():
                xv = a_ref[...]
                rstd = lax.rsqrt(jnp.mean(xv * xv, axis=-1, keepdims=True) + EPS)
                hv = (xv * rstd * g_ref[...]).astype(BF16)
                hbuf[...] = hv
                h_ref[...] = hv

            av = hbuf[...]
        else:
            av = a_ref[...]
        if prologue is not None:
            av = prologue(av)
        if dot_fn is not None:
            p = dot_fn(av, b_ref)
        else:
            bv = b_ref[...]
            if b_reshape is not None:
                bv = bv.reshape(b_reshape)
            p = lax.dot_general(av, bv, (dims, ((), ())), preferred_element_type=F32)

        def finish(acc):
            r = acc if epilogue is None else epilogue(acc, *ex)
            o_ref[...] = r.astype(o_ref.dtype)

        if nk == 1:
            finish(p)
        else:
            acc_ref = scratch[0]

            @pl.when(k == 0)
            def _():
                acc_ref[...] = p

            @pl.when(jnp.logical_and(k > 0, k < nk - 1))
            def _():
                acc_ref[...] += p

            @pl.when(k == nk - 1)
            def _():
                finish(acc_ref[...] + p)

        if comm is not None:
            @pl.when(jnp.logical_and(jnp.logical_and(i == gi - 1, j == gj - 1), k == nk - 1))
            def _():
                comm.wait(R, ss, rs)

    inputs = [a, b, *extra]
    in_specs = [a_spec, b_spec, *extra_specs]
    out_shapes, out_specs = [out_shape], [out_spec]
    scratch_shapes = [pltpu.VMEM(acc_shape, F32)] if nk > 1 else []
    if norm_g is not None:
        inputs.append(norm_g)
        in_specs.append(pl.BlockSpec((1, a.shape[1]), lambda i, j, k: (0, 0)))
        out_shapes.append(_sds(a.shape, BF16))
        out_specs.append(pl.BlockSpec(a_spec.block_shape, a_spec.index_map))
        scratch_shapes.append(pltpu.VMEM(a_spec.block_shape, BF16))
    if comm is None:
        sem = ("parallel", "arbitrary" if norm_g is not None else "parallel", "arbitrary")
        res = pl.pallas_call(
            body, name=name, grid=grid, in_specs=in_specs, out_specs=out_specs, out_shape=out_shapes,
            scratch_shapes=scratch_shapes, compiler_params=_cparams(sem),
        )(*inputs)
        return res[0] if norm_g is None else (res[0], res[1])
    aliases = {len(inputs) + n_in + t: 1 + n_norm + n_out + t for t in range(n_al)}
    inputs += [B[key] for key in comm.ins] + [B[key] for key in comm.alias]
    in_specs += [ANY] * (n_in + n_al)
    res = pl.pallas_call(
        body, name=name, grid=grid, in_specs=in_specs, out_specs=out_specs + [ANY] * (n_out + n_al),
        out_shape=out_shapes + [s for _, s in comm.outs] + [_sds(B[key].shape, B[key].dtype) for key in comm.alias],
        scratch_shapes=scratch_shapes + [pltpu.SemaphoreType.DMA((comm.n_sems,)), pltpu.SemaphoreType.DMA((comm.n_sems,))],
        input_output_aliases=aliases, compiler_params=_cparams(("arbitrary", "arbitrary", "arbitrary")),
    )(*inputs)
    B.update(zip([key for key, _ in comm.outs] + list(comm.alias), res[1 + n_norm:]))
    return res[0] if norm_g is None else (res[0], res[1])


def mm_fwd_col(name, x, g, wg, n_out, epilogue=None, out_dtype=BF16, **kw):
    S, D = x.shape
    W = wg.shape[2]
    tm, tn = min(1024, S), _pick(W, 1024)
    wps = W // tn
    return _mm(name, x, wg, norm_g=g, dims=NN, grid=(S // tm, n_out // tn, 1),
               a_spec=pl.BlockSpec((tm, D), lambda i, j, k: (i, 0)),
               b_spec=pl.BlockSpec((None, D, tn), lambda i, j, k: (j // wps, 0, j % wps)),
               out_spec=pl.BlockSpec((tm, tn), lambda i, j, k: (i, j)),
               out_shape=_sds((S, n_out), out_dtype), acc_shape=(tm, tn), epilogue=epilogue, **kw)


def mm_fwd_row(name, a, wg, res, prologue=None, **kw):
    S, K = a.shape
    ksh, D = wg.shape[1], wg.shape[2]
    tm, tn, tk = min(1024, S), min(1024, D), min(2048, ksh)
    res_spec = pl.BlockSpec((tm, tn), lambda i, j, k: (i, j))
    if K <= 2048:
        return _mm(name, a, wg, dims=NN, grid=(S // tm, D // tn, 1),
                   a_spec=pl.BlockSpec((tm, K), lambda i, j, k: (i, 0)),
                   b_spec=pl.BlockSpec((N_CHIPS, ksh, tn), lambda i, j, k: (0, 0, j)),
                   out_spec=res_spec, out_shape=_sds((S, D), F32), acc_shape=(tm, tn),
                   extra=(res,), extra_specs=(res_spec,), prologue=prologue, epilogue=lambda acc, r: acc + r[...],
                   b_reshape=(K, tn), **kw)
    kps = ksh // tk
    return _mm(name, a, wg, dims=NN, grid=(S // tm, D // tn, K // tk),
               a_spec=pl.BlockSpec((tm, tk), lambda i, j, k: (i, k)),
               b_spec=pl.BlockSpec((None, tk, tn), lambda i, j, k: (k // kps, k % kps, j)),
               out_spec=res_spec, out_shape=_sds((S, D), F32), acc_shape=(tm, tn),
               extra=(res,), extra_specs=(res_spec,), prologue=prologue, epilogue=lambda acc, r: acc + r[...], **kw)


def mm_bwd_col_x(name, dz, wg, add=None, **kw):
    S, N = dz.shape
    D, W = wg.shape[1], wg.shape[2]
    t = 1024 if N <= 4096 else 512
    tm, to = min(t, S), min(t, D)
    extra, extra_specs, epi = (), (), None
    if add is not None:
        extra, extra_specs = (add,), (pl.BlockSpec((tm, to), lambda i, j, k: (i, j)),)
        epi = lambda acc, r: acc + r[...]
    if N > 6144:
        tm, to = min(1024, S), min(1024, D)
        return _mm(name, dz, wg, dims=NT, grid=(S // tm, D // to, N_CHIPS),
                   a_spec=pl.BlockSpec((tm, W), lambda i, j, k: (i, k)),
                   b_spec=pl.BlockSpec((None, to, W), lambda i, j, k: (k, j, 0)),
                   out_spec=pl.BlockSpec((tm, to), lambda i, j, k: (i, j)),
                   out_shape=_sds((S, D), F32), acc_shape=(tm, to), extra=extra, extra_specs=extra_specs, epilogue=epi, **kw)

    def dot_fn(av, b_ref):
        p = lax.dot_general(av[:, :W], b_ref[0], (NT, ((), ())), preferred_element_type=F32)
        for s in range(1, N_CHIPS):
            p = p + lax.dot_general(av[:, s * W:(s + 1) * W], b_ref[s], (NT, ((), ())), preferred_element_type=F32)
        return p

    return _mm(name, dz, wg, dims=NT, grid=(S // tm, D // to, 1),
               a_spec=pl.BlockSpec((tm, N), lambda i, j, k: (i, 0)),
               b_spec=pl.BlockSpec((N_CHIPS, to, W), lambda i, j, k: (0, j, 0)),
               out_spec=pl.BlockSpec((tm, to), lambda i, j, k: (i, j)),
               out_shape=_sds((S, D), F32), acc_shape=(tm, to), extra=extra, extra_specs=extra_specs, epilogue=epi,
               dot_fn=dot_fn, **kw)


def mm_bwd_row_x(name, dxb, wg, mul=None, **kw):
    S, D = dxb.shape
    ksh = wg.shape[1]
    tm, tn = min(1024, S), min(1024, ksh)
    kps = ksh // tn
    extra, extra_specs, epi = (), (), None
    if mul is not None:
        extra, extra_specs = (mul,), (pl.BlockSpec((tm, tn), lambda i, j, k: (i, j)),)
        epi = lambda acc, r: acc * (2.0 * r[...].astype(F32))
    return _mm(name, dxb, wg, dims=NT, grid=(S // tm, 4 * kps, 1),
               a_spec=pl.BlockSpec((tm, D), lambda i, j, k: (i, 0)),
               b_spec=pl.BlockSpec((None, tn, D), lambda i, j, k: (j // kps, j % kps, 0)),
               out_spec=pl.BlockSpec((tm, tn), lambda i, j, k: (i, j)),
               out_shape=_sds((S, 4 * ksh), BF16), acc_shape=(tm, tn), extra=extra, extra_specs=extra_specs, epilogue=epi, **kw)


def mm_bwd_col_w(name, h, dz, **kw):
    S, D = h.shape
    N = dz.shape[1]
    W = N // N_CHIPS
    tk, tn, ts = min(512, D), _pick(W, 512), S
    wps = W // tn
    return _mm(name, h, dz, dims=TN, grid=(D // tk, N // tn, S // ts),
               a_spec=pl.BlockSpec((ts, tk), lambda i, j, k: (k, i)),
               b_spec=pl.BlockSpec((ts, tn), lambda i, j, k: (k, j)),
               out_spec=pl.BlockSpec((None, tk, tn), lambda i, j, k: (j // wps, i, j % wps)),
               out_shape=_sds((N_CHIPS, D, W), BF16), acc_shape=(tk, tn), **kw)


def mm_bwd_row_w(name, a, dxb, prologue=None, **kw):
    S, K = a.shape
    D = dxb.shape[1]
    ksh = K // N_CHIPS
    tk, tn, ts = min(512, ksh), min(512, D), S
    kps = ksh // tk
    return _mm(name, a, dxb, dims=TN, grid=(K // tk, D // tn, S // ts),
               a_spec=pl.BlockSpec((ts, tk), lambda i, j, k: (k, i)),
               b_spec=pl.BlockSpec((ts, tn), lambda i, j, k: (k, j)),
               out_spec=pl.BlockSpec((None, tk, tn), lambda i, j, k: (i // kps, i % kps, j)),
               out_shape=_sds((N_CHIPS, ksh, D), BF16), acc_shape=(tk, tn), prologue=prologue, **kw)


def mm_plain(name, a, b, dims, out_dtype, add=None):
    if dims == NN:
        M, N = a.shape[0], b.shape[1]
    elif dims == NT:
        M, N = a.shape[0], b.shape[0]
    else:
        M, N = a.shape[1], b.shape[1]
    red = a.shape[0] if dims == TN else a.shape[1]
    tm, tn = min(1024, M), min(1024, N)
    tr = min(1024, red) if dims == TN else red
    nk = red // tr
    if dims == TN:
        a_spec = pl.BlockSpec((tr, tm), lambda i, j, k: (k, i))
        b_spec = pl.BlockSpec((tr, tn), lambda i, j, k: (k, j))
    elif dims == NN:
        a_spec = pl.BlockSpec((tm, tr), lambda i, j, k: (i, k))
        b_spec = pl.BlockSpec((tr, tn), lambda i, j, k: (k, j))
    else:
        a_spec = pl.BlockSpec((tm, tr), lambda i, j, k: (i, k))
        b_spec = pl.BlockSpec((tn, tr), lambda i, j, k: (j, k))
    extra, extra_specs, epi = (), (), None
    if add is not None:
        extra, extra_specs = (add,), (pl.BlockSpec((tm, tn), lambda i, j, k: (i, j)),)
        epi = lambda acc, r: acc + r[...]
    return _mm(name, a, b, dims=dims, grid=(M // tm, N // tn, nk), a_spec=a_spec, b_spec=b_spec,
               out_spec=pl.BlockSpec((tm, tn), lambda i, j, k: (i, j)),
               out_shape=_sds((M, N), out_dtype), acc_shape=(tm, tn), extra=extra, extra_specs=extra_specs, epilogue=epi)


ROWS = 256


def rmsnorm_bwd(name, x, g, dh, dres):
    S, D = x.shape
    tr = min(ROWS, S)

    def body(x_ref, g_ref, dh_ref, dres_ref, dx_ref, dxb_ref, dg_ref):
        xv = x_ref[...]
        rstd = lax.rsqrt(jnp.mean(xv * xv, axis=-1, keepdims=True) + EPS)
        xh = xv * rstd
        dy = dh_ref[...]
        dxh = dy * g_ref[...]
        dx = dres_ref[...] + rstd * (dxh - xh * jnp.mean(dxh * xh, axis=-1, keepdims=True))
        dx_ref[...] = dx
        dxb_ref[...] = dx.astype(BF16)

        @pl.when(pl.program_id(0) == 0)
        def _():
            dg_ref[...] = jnp.zeros_like(dg_ref)

        dg_ref[...] += jnp.sum(dy * xh, axis=0, keepdims=True)

    row = pl.BlockSpec((tr, D), lambda i: (i, 0))
    vec = pl.BlockSpec((1, D), lambda i: (0, 0))
    return pl.pallas_call(
        body, name=name, grid=(S // tr,), in_specs=[row, vec, row, row], out_specs=[row, row, vec],
        out_shape=[_sds((S, D), F32), _sds((S, D), BF16), _sds((1, D), F32)],
        compiler_params=_cparams(("arbitrary",)),
    )(x, g, dh, dres)


def final_loss(name, x, g, target):
    S, D = x.shape
    tr = min(ROWS, S)

    def body(x_ref, g_ref, t_ref, loss_ref, dx_ref, dxb_ref, dg_ref):
        xv = x_ref[...]
        gv = g_ref[...]
        rstd = lax.rsqrt(jnp.mean(xv * xv, axis=-1, keepdims=True) + EPS)
        xh = xv * rstd
        err = xh * gv - t_ref[...]
        dy = err * (1.0 / D)
        dxh = dy * gv
        dx = rstd * (dxh - xh * jnp.mean(dxh * xh, axis=-1, keepdims=True))
        dx_ref[...] = dx
        dxb_ref[...] = dx.astype(BF16)

        @pl.when(pl.program_id(0) == 0)
        def _():
            dg_ref[...] = jnp.zeros_like(dg_ref)
            loss_ref[...] = jnp.zeros_like(loss_ref)

        dg_ref[...] += jnp.sum(dy * xh, axis=0, keepdims=True)
        loss_ref[...] += 0.5 * jnp.sum(jnp.mean(err * err, axis=-1, keepdims=True))

    row = pl.BlockSpec((tr, D), lambda i: (i, 0))
    vec = pl.BlockSpec((1, D), lambda i: (0, 0))
    return pl.pallas_call(
        body, name=name, grid=(S // tr,), in_specs=[row, vec, row],
        out_specs=[pl.BlockSpec((SUBLANES, LANES), lambda i: (0, 0)), row, row, vec],
        out_shape=[_sds((SUBLANES, LANES), F32), _sds((S, D), F32), _sds((S, D), BF16), _sds((1, D), F32)],
        compiler_params=_cparams(("arbitrary",)),
    )(x, g, target)


def _gelu(x):
    return 0.5 * x * (1.0 + lax.erf(x * 0.7071067811865476))


def _gelu_grad(x):
    return 0.5 * (1.0 + lax.erf(x * 0.7071067811865476)) + x * jnp.exp(-0.5 * x * x) * 0.3989422804014327


def _gm_common(zp, lng, lnb, D):
    z = _gelu(zp)
    u, v = z[:, :D], z[:, D:]
    xc = v - jnp.mean(v, axis=-1, keepdims=True)
    rstd = lax.rsqrt(jnp.mean(xc * xc, axis=-1, keepdims=True) + EPS)
    xh = xc * rstd
    return u, xh, rstd, xh * lng + lnb


def gm_mid_fwd(name, zp, lng, lnb, wm, bT):
    S, D2 = zp.shape
    D = D2 // 2
    dg = D // GM_GROUPS
    P = GM_BLOCK

    def body(z_ref, lng_ref, lnb_ref, wm_ref, bT_ref, o_ref):
        u, _, _, vn = _gm_common(z_ref[...].astype(F32), lng_ref[...], lnb_ref[...], D)
        vnb = vn.astype(BF16)
        for gi in range(GM_GROUPS):
            cols = slice(gi * dg, (gi + 1) * dg)
            mixed = jnp.dot(wm_ref[gi], vnb[:, cols], preferred_element_type=F32) + bT_ref[:, gi:gi + 1]
            o_ref[:, cols] = (u[:, cols] * mixed).astype(BF16)

    vec = pl.BlockSpec((1, D), lambda i: (0, 0))
    return pl.pallas_call(
        body, name=name, grid=(S // P,),
        in_specs=[pl.BlockSpec((P, D2), lambda i: (i, 0)), vec, vec,
                  pl.BlockSpec((GM_GROUPS, P, P), lambda i: (0, 0, 0)), pl.BlockSpec((P, GM_GROUPS), lambda i: (0, 0))],
        out_specs=pl.BlockSpec((P, D), lambda i: (i, 0)), out_shape=_sds((S, D), BF16),
        compiler_params=_cparams(("parallel",)),
    )(zp, lng, lnb, wm, bT)


def gm_mid_bwd(name, zp, dgated, lng, lnb, wm, wmT, bT):
    S, D2 = zp.shape
    D = D2 // 2
    dg = D // GM_GROUPS
    P = GM_BLOCK

    def body(z_ref, dgt_ref, lng_ref, lnb_ref, wm_ref, wmT_ref, bT_ref, dz_ref, dlng_ref, dlnb_ref, dw_ref, dbT_ref, dvn_ref):
        @pl.when(pl.program_id(0) == 0)
        def _():
            dlng_ref[...] = jnp.zeros_like(dlng_ref)
            dlnb_ref[...] = jnp.zeros_like(dlnb_ref)
            dw_ref[...] = jnp.zeros_like(dw_ref)
            dbT_ref[...] = jnp.zeros_like(dbT_ref)

        zp_v = z_ref[...].astype(F32)
        lng_v = lng_ref[...]
        u, xh, rstd, vn = _gm_common(zp_v, lng_v, lnb_ref[...], D)
        vnb = vn.astype(BF16)
        dgt = dgt_ref[...].astype(F32)
        for gi in range(GM_GROUPS):
            cols = slice(gi * dg, (gi + 1) * dg)
            mixed = jnp.dot(wm_ref[gi], vnb[:, cols], preferred_element_type=F32) + bT_ref[:, gi:gi + 1]
            dm = dgt[:, cols] * u[:, cols]
            dmb = dm.astype(BF16)
            dz_ref[:, cols] = (dgt[:, cols] * mixed * _gelu_grad(zp_v[:, cols])).astype(BF16)
            dbT_ref[:, gi:gi + 1] += jnp.sum(dm, axis=1, keepdims=True)
            dw_ref[gi] += lax.dot_general(dmb, vnb[:, cols], (NT, ((), ())), preferred_element_type=F32)
            dvn_ref[:, cols] = jnp.dot(wmT_ref[gi], dmb, preferred_element_type=F32)
        dvn = dvn_ref[...]
        dlng_ref[...] += jnp.sum(dvn * xh, axis=0, keepdims=True)
        dlnb_ref[...] += jnp.sum(dvn, axis=0, keepdims=True)
        dyg = dvn * lng_v
        dv = rstd * (dyg - jnp.mean(dyg, axis=-1, keepdims=True) - xh * jnp.mean(dyg * xh, axis=-1, keepdims=True))
        dz_ref[:, D:] = (dv * _gelu_grad(zp_v[:, D:])).astype(BF16)

    vec = pl.BlockSpec((1, D), lambda i: (0, 0))
    wsp = pl.BlockSpec((GM_GROUPS, P, P), lambda i: (0, 0, 0))
    bsp = pl.BlockSpec((P, GM_GROUPS), lambda i: (0, 0))
    return pl.pallas_call(
        body, name=name, grid=(S // P,),
        in_specs=[pl.BlockSpec((P, D2), lambda i: (i, 0)), pl.BlockSpec((P, D), lambda i: (i, 0)), vec, vec, wsp, wsp, bsp],
        out_specs=[pl.BlockSpec((P, D2), lambda i: (i, 0)), vec, vec, wsp, bsp],
        out_shape=[_sds((S, D2), BF16), _sds((1, D), F32), _sds((1, D), F32), _sds((GM_GROUPS, P, P), F32), _sds((P, GM_GROUPS), F32)],
        scratch_shapes=[pltpu.VMEM((P, D), F32)],
        compiler_params=_cparams(("arbitrary",)),
    )(zp, dgated, lng, lnb, wm, wmT, bT)


def _gla_gate(lr, w2, ba, tri):
    logit = jnp.dot(lr, w2, preferred_element_type=F32) + ba
    la = (jnp.minimum(logit, 0.0) - jnp.log1p(jnp.exp(-jnp.abs(logit)))) * (1.0 / GATE_TAU)
    g = jnp.dot(tri, la, preferred_element_type=F32, precision=HIGHEST)
    return logit, g


def gla_scan_fwd(name, proj, lr, w2p, ba, tri):
    S, D3 = proj.shape
    D = D3 // 3
    H, C = GLA_HEADS, CHUNK
    dk, dv = D // 2 // H, D // H
    NC = S // C
    scale = dk ** -0.5

    def body(q_ref, k_ref, v_ref, lr_ref, w2_ref, ba_ref, tri_ref, o_ref, st_ref, state):
        @pl.when(pl.program_id(0) == 0)
        def _():
            state[...] = jnp.zeros_like(state)

        _, g = _gla_gate(lr_ref[...], w2_ref[...], ba_ref[...], tri_ref[...])
        gend = g[C - 1:C, :]
        kdec = (k_ref[...].astype(F32) * jnp.exp(gend - g)).astype(BF16)
        dec = jnp.exp(gend)
        qs = (q_ref[...].astype(F32) * scale).astype(BF16)
        for hh in range(H):
            kc, vc = slice(hh * dk, (hh + 1) * dk), slice(hh * dv, (hh + 1) * dv)
            kv = lax.dot_general(v_ref[:, vc], kdec[:, kc], (TN, ((), ())), preferred_element_type=F32)
            new = dec[:, kc] * state[hh] + kv
            state[hh] = new
            nb = new.astype(BF16)
            st_ref[hh] = nb
            o_ref[:, vc] = lax.dot_general(qs[:, kc], nb, (NT, ((), ())), preferred_element_type=F32).astype(BF16)

    return pl.pallas_call(
        body, name=name, grid=(NC,),
        in_specs=[pl.BlockSpec((C, D // 2), lambda t: (t, 0)), pl.BlockSpec((C, D // 2), lambda t: (t, 1)),
                  pl.BlockSpec((C, D), lambda t: (t, 1)), pl.BlockSpec((C, LOW), lambda t: (t, 0)),
                  pl.BlockSpec((LOW, D // 2), lambda t: (0, 0)), pl.BlockSpec((1, D // 2), lambda t: (0, 0)),
                  pl.BlockSpec((C, C), lambda t: (0, 0))],
        out_specs=[pl.BlockSpec((C, D), lambda t: (t, 0)), pl.BlockSpec((None, H, dv, dk), lambda t: (t, 0, 0, 0))],
        out_shape=[_sds((S, D), BF16), _sds((NC, H, dv, dk), BF16)],
        scratch_shapes=[pltpu.VMEM((H, dv, dk), F32)],
        compiler_params=_cparams(("arbitrary",)),
    )(proj, proj, proj, lr, w2p, ba, tri)


def gla_scan_bwd(name, proj, lr, w2p, ba, tri, triT, states, do, dr):
    S, D3 = proj.shape
    D = D3 // 3
    H, C = GLA_HEADS, CHUNK
    dk, dv = D // 2 // H, D // H
    NC = S // C
    scale = dk ** -0.5

    def body(q_ref, k_ref, v_ref, lr_ref, w2_ref, ba_ref, tri_ref, triT_ref, st_ref, sp_ref, do_ref, dr_ref,
             dp_ref, dl_ref, dba_ref, dstate, dkd_ref, ddec_ref):
        t = pl.program_id(0)
        dp_ref[:, 2 * D:] = dr_ref[...]

        @pl.when(t == 0)
        def _():
            dstate[...] = jnp.zeros_like(dstate)
            dba_ref[...] = jnp.zeros_like(dba_ref)

        logit, g = _gla_gate(lr_ref[...], w2_ref[...], ba_ref[...], tri_ref[...])
        gend = g[C - 1:C, :]
        e = jnp.exp(gend - g)
        kf = k_ref[...].astype(F32)
        kdec = (kf * e).astype(BF16)
        dec = jnp.exp(gend)
        qs = (q_ref[...].astype(F32) * scale).astype(BF16)
        has_prev = (t < NC - 1).astype(F32)
        for hh in range(H):
            kc, vc = slice(hh * dk, (hh + 1) * dk), slice(hh * dv, (hh + 1) * dv)
            dob = do_ref[:, vc]
            dp_ref[:, kc] = (jnp.dot(dob, st_ref[hh], preferred_element_type=F32) * scale).astype(BF16)
            ds = dstate[hh] + lax.dot_general(dob, qs[:, kc], (TN, ((), ())), preferred_element_type=F32)
            dsb = ds.astype(BF16)
            dkd_ref[:, kc] = jnp.dot(v_ref[:, vc], dsb, preferred_element_type=F32)
            dp_ref[:, D + hh * dv:D + (hh + 1) * dv] = lax.dot_general(kdec[:, kc], dsb, (NT, ((), ())),
                                                                    preferred_element_type=F32).astype(BF16)
            ddec_ref[:, kc] = jnp.sum(ds * sp_ref[hh].astype(F32), axis=0, keepdims=True) * has_prev
            dstate[hh] = dec[:, kc] * ds
        dkdec = dkd_ref[...]
        dp_ref[:, D // 2:D] = (dkdec * e).astype(BF16)
        dd = dkdec * kf * e
        dgend = jnp.sum(dd, axis=0, keepdims=True) + ddec_ref[...] * dec
        last = lax.broadcasted_iota(jnp.int32, (C, 1), 0) == C - 1
        dg = jnp.where(last, dgend, 0.0) - dd
        dla = jnp.dot(triT_ref[...], dg, preferred_element_type=F32, precision=HIGHEST)
        dlogit = dla * (1.0 / GATE_TAU) * (1.0 - jax.nn.sigmoid(logit))
        dl_ref[...] = dlogit.astype(BF16)
        dba_ref[...] += jnp.sum(dlogit, axis=0, keepdims=True)

    rev = lambda t: NC - 1 - t
    half = pl.BlockSpec((C, D // 2), lambda t: (rev(t), 0))
    full = pl.BlockSpec((C, D), lambda t: (rev(t), 0))
    return pl.pallas_call(
        body, name=name, grid=(NC,),
        in_specs=[half, pl.BlockSpec((C, D // 2), lambda t: (rev(t), 1)), pl.BlockSpec((C, D), lambda t: (rev(t), 1)),
                  pl.BlockSpec((C, LOW), lambda t: (rev(t), 0)),
                  pl.BlockSpec((LOW, D // 2), lambda t: (0, 0)), pl.BlockSpec((1, D // 2), lambda t: (0, 0)),
                  pl.BlockSpec((C, C), lambda t: (0, 0)), pl.BlockSpec((C, C), lambda t: (0, 0)),
                  pl.BlockSpec((None, H, dv, dk), lambda t: (rev(t), 0, 0, 0)),
                  pl.BlockSpec((None, H, dv, dk), lambda t: (jnp.maximum(rev(t) - 1, 0), 0, 0, 0)),
                  full, full],
        out_specs=[pl.BlockSpec((C, D3), lambda t: (rev(t), 0)), half, pl.BlockSpec((1, D // 2), lambda t: (0, 0))],
        out_shape=[_sds((S, D3), BF16), _sds((S, D // 2), BF16), _sds((1, D // 2), F32)],
        scratch_shapes=[pltpu.VMEM((H, dv, dk), F32), pltpu.VMEM((C, D // 2), F32), pltpu.VMEM((1, D // 2), F32)],
        compiler_params=_cparams(("arbitrary",)),
    )(proj, proj, proj, lr, w2p, ba, tri, triT, states, states, do, dr)


def _gla_post_common(o, r, ng):
    rs = lax.rsqrt(jnp.mean(o * o, axis=-1, keepdims=True) + EPS)
    oh = o * rs
    sig = jax.nn.sigmoid(r)
    return rs, oh, oh * ng, sig, r * sig


def gla_post_fwd(name, o_raw, proj, ng):
    S, D = o_raw.shape
    dv = D // GLA_HEADS
    tr = min(ROWS, S)

    def body(o_ref, r_ref, ng_ref, og_ref):
        for hh in range(GLA_HEADS):
            cols = slice(hh * dv, (hh + 1) * dv)
            _, _, on, _, sil = _gla_post_common(o_ref[:, cols].astype(F32), r_ref[:, cols].astype(F32), ng_ref[:, cols])
            og_ref[:, cols] = (on * sil).astype(BF16)

    row = pl.BlockSpec((tr, D), lambda i: (i, 0))
    return pl.pallas_call(
        body, name=name, grid=(S // tr,),
        in_specs=[row, pl.BlockSpec((tr, D), lambda i: (i, 2)), pl.BlockSpec((1, D), lambda i: (0, 0))],
        out_specs=row, out_shape=_sds((S, D), BF16), compiler_params=_cparams(("parallel",)),
    )(o_raw, proj, ng)


def gla_post_bwd(name, dog, o_raw, proj, ng):
    S, D = o_raw.shape
    dv = D // GLA_HEADS
    tr = min(ROWS, S)

    def body(dog_ref, o_ref, r_ref, ng_ref, do_ref, dr_ref, dng_ref):
        @pl.when(pl.program_id(0) == 0)
        def _():
            dng_ref[...] = jnp.zeros_like(dng_ref)

        for hh in range(GLA_HEADS):
            cols = slice(hh * dv, (hh + 1) * dv)
            r = r_ref[:, cols].astype(F32)
            ngv = ng_ref[:, cols]
            rs, oh, on, sig, sil = _gla_post_common(o_ref[:, cols].astype(F32), r, ngv)
            dogv = dog_ref[:, cols].astype(F32)
            don = dogv * sil
            dr_ref[:, cols] = (dogv * on * (sig * (1.0 + r * (1.0 - sig)))).astype(BF16)
            dng_ref[:, cols] += jnp.sum(don * oh, axis=0, keepdims=True)
            doh = don * ngv
            do_ref[:, cols] = (rs * (doh - oh * jnp.mean(doh * oh, axis=-1, keepdims=True))).astype(BF16)

    row = pl.BlockSpec((tr, D), lambda i: (i, 0))
    vec = pl.BlockSpec((1, D), lambda i: (0, 0))
    return pl.pallas_call(
        body, name=name, grid=(S // tr,),
        in_specs=[row, row, pl.BlockSpec((tr, D), lambda i: (i, 2)), vec],
        out_specs=[row, row, vec], out_shape=[_sds((S, D), BF16), _sds((S, D), BF16), _sds((1, D), F32)],
        compiler_params=_cparams(("arbitrary",)),
    )(dog, o_raw, proj, ng)


def cast_into(name, w, w_off, rows, chip_idx):
    W = w.shape[1]
    tr = _rows(rows, 512)

    def body(p_ref, w_ref, o_ref):
        o_ref[...] = w_ref[...].astype(BF16)

    return pl.pallas_call(
        body, name=name,
        grid_spec=pltpu.PrefetchScalarGridSpec(
            num_scalar_prefetch=1, grid=(rows // tr,),
            in_specs=[pl.BlockSpec((tr, W), lambda i, p: (w_off // tr + i, 0))],
            out_specs=pl.BlockSpec((None, tr, W), lambda i, p: (p[0], i, 0))),
        out_shape=_sds((N_CHIPS, rows, W), BF16), compiler_params=_cparams(("parallel",)),
    )(chip_idx, w)


def pair_add(name, gbuf, rsib, c_idx):
    _, R, W = gbuf.shape
    hr = R // 2
    tr = _rows(hr, 512)
    nb = hr // tr

    def body(c_ref, a_ref, b_ref, o_ref):
        o_ref[...] = (a_ref[...].astype(F32) + b_ref[...].astype(F32)).astype(BF16)

    return pl.pallas_call(
        body, name=name,
        grid_spec=pltpu.PrefetchScalarGridSpec(
            num_scalar_prefetch=1, grid=(N_CHIPS, nb),
            in_specs=[pl.BlockSpec((None, tr, W), lambda s, i, c: (s, c[0] * nb + i, 0)),
                      pl.BlockSpec((None, tr, W), lambda s, i, c: (s, i, 0))],
            out_specs=pl.BlockSpec((None, tr, W), lambda s, i, c: (s, i, 0))),
        out_shape=_sds((N_CHIPS, hr, W), BF16), compiler_params=_cparams(("parallel", "parallel")),
    )(c_idx, gbuf, rsib)


def sum_chips(name, p, q, chip_idx, c_idx):
    _, hr, W = p.shape
    tr = _rows(hr, 512)
    nb = hr // tr

    def body(s_ref, c_ref, p_ref, q0_ref, q1_ref, q2_ref, o_ref):
        o_ref[...] = ((p_ref[...].astype(F32) + q0_ref[...].astype(F32)) + q1_ref[...].astype(F32)) + q2_ref[...].astype(F32)

    def qspec(j):
        return pl.BlockSpec((None, tr, W), lambda i, s, c: (j, i, 0))

    return pl.pallas_call(
        body, name=name,
        grid_spec=pltpu.PrefetchScalarGridSpec(
            num_scalar_prefetch=2, grid=(nb,),
            in_specs=[pl.BlockSpec((None, tr, W), lambda i, s, c: (s[0], i, 0)), qspec(0), qspec(1), qspec(2)],
            out_specs=pl.BlockSpec((tr, W), lambda i, s, c: (c[0] * nb + i, 0))),
        out_shape=_sds((2 * hr, W), F32), compiler_params=_cparams(("parallel",)),
    )(chip_idx, c_idx, p, q, q, q)


def sum_devices(name, parts):
    _, R, W = parts.shape

    def body(p_ref, o_ref):
        acc = p_ref[0]
        for d in range(1, N_DEV):
            acc = acc + p_ref[d]
        o_ref[...] = acc

    tr = _rows(R, 512)
    return pl.pallas_call(
        body, name=name, grid=(R // tr,), in_specs=[pl.BlockSpec((N_DEV, tr, W), lambda i: (0, i, 0))],
        out_specs=pl.BlockSpec((tr, W), lambda i: (i, 0)), out_shape=_sds((R, W), F32),
        compiler_params=_cparams(("parallel",)),
    )(parts)


def adamw(name, g, w, m, v, w_off, prev=None):
    rows, W = g.shape
    tr = _rows(rows, 256)

    def body(*refs):
        g_ref, w_ref, m_ref, v_ref = refs[:4]
        go_ref, d_ref, mo_ref, vo_ref = refs[-4:]
        gv = g_ref[...]
        mn = ADAM_B1 * m_ref[...] + (1.0 - ADAM_B1) * gv
        vn = ADAM_B2 * v_ref[...] + (1.0 - ADAM_B2) * (gv * gv)
        m_hat = mn / (1.0 - ADAM_B1 ** ADAM_STEP)
        v_hat = vn / (1.0 - ADAM_B2 ** ADAM_STEP)
        go_ref[...] = gv
        d_ref[...] = -ADAM_LR * (m_hat / (jnp.sqrt(v_hat) + ADAM_EPS) + ADAM_WD * w_ref[...])
        mo_ref[...] = mn
        vo_ref[...] = vn

    blk = pl.BlockSpec((tr, W), lambda i: (w_off // tr + i, 0))
    inputs = [g, w, m, v]
    in_specs = [pl.BlockSpec((tr, W), lambda i: (i, 0)), blk, blk, blk]
    aliases = {}
    if prev is not None:
        aliases = {4 + t: t for t in range(4)}
        inputs += list(prev)
        in_specs += [ANY] * 4
    return pl.pallas_call(
        body, name=name, grid=(rows // tr,), in_specs=in_specs, out_specs=[blk] * 4,
        out_shape=[_sds(w.shape, F32)] * 4, input_output_aliases=aliases, compiler_params=_cparams(("parallel",)),
    )(*inputs)


def _pack(arrs):
    flat = jnp.concatenate([a.reshape(-1).astype(F32) for a in arrs])
    tile = SUBLANES * LANES * 2
    pad = (-flat.shape[0]) % tile
    return jnp.pad(flat, (0, pad)).reshape(-1, LANES)


def _unpack(packed, shapes):
    flat = packed.reshape(-1)
    out, pos = [], 0
    for s in shapes:
        n = 1
        for d in s:
            n *= d
        out.append(flat[pos:pos + n].reshape(s))
        pos += n
    return out


def kernel(x, norm_mix_g, norm_ffn_g, final_g, gm_w_in, gm_ln_g, gm_ln_b, gm_w_s, gm_b_s, gm_w_out, gla_w_in, gla_w_a1, gla_w_a2, gla_b_a, gla_norm_g, gla_w_o, ffn_w_up, ffn_w_down, loss_target, m_norm_mix_g, m_norm_ffn_g, m_final_g, m_gm_w_in, m_gm_ln_g, m_gm_ln_b, m_gm_w_s, m_gm_b_s, m_gm_w_out, m_gla_w_in, m_gla_w_a1, m_gla_w_a2, m_gla_b_a, m_gla_norm_g, m_gla_w_o, m_ffn_w_up, m_ffn_w_down, v_norm_mix_g, v_norm_ffn_g, v_final_g, v_gm_w_in, v_gm_ln_g, v_gm_ln_b, v_gm_w_s, v_gm_b_s, v_gm_w_out, v_gla_w_in, v_gla_w_a1, v_gla_w_a2, v_gla_b_a, v_gla_norm_g, v_gla_w_o, v_ffn_w_up, v_ffn_w_down):
    S, D = x.shape[1], x.shape[2]
    depth = norm_mix_g.shape[0]
    n_gm, n_gla = gm_w_in.shape[0], gla_w_in.shape[0]
    F = 4 * D
    P = GM_BLOCK
    xi, yi, ci = lax.axis_index("x"), lax.axis_index("y"), lax.axis_index("c")
    chip = 2 * xi + yi
    chip_idx = jnp.reshape(chip, (1,)).astype(jnp.int32)
    c_idx = jnp.reshape(ci, (1,)).astype(jnp.int32)

    is_gm = [i % 2 == 0 for i in range(depth)]
    w_mix_in = [(gm_w_in, m_gm_w_in, v_gm_w_in) if is_gm[i] else (gla_w_in, m_gla_w_in, v_gla_w_in) for i in range(depth)]
    w_mix_out = [(gm_w_out, m_gm_w_out, v_gm_w_out) if is_gm[i] else (gla_w_o, m_gla_w_o, v_gla_w_o) for i in range(depth)]
    w_up = (ffn_w_up, m_ffn_w_up, v_ffn_w_up)
    w_down = (ffn_w_down, m_ffn_w_down, v_ffn_w_down)

    def flat2(w):
        return w.reshape(w.shape[0] * w.shape[1], w.shape[2])

    B = {}

    for i in range(depth):
        j = i // 2
        B[f"w_in_{i}"] = cast_into(f"cast_in_{i}", flat2(w_mix_in[i][0]), j * D, D, chip_idx)
        B[f"w_out_{i}"] = cast_into(f"cast_out_{i}", flat2(w_mix_out[i][0]), j * (D // 4), D // 4, chip_idx)
        B[f"w_up_{i}"] = cast_into(f"cast_up_{i}", flat2(ffn_w_up), i * D, D, chip_idx)
        B[f"w_down_{i}"] = cast_into(f"cast_down_{i}", flat2(ffn_w_down), i * D, D, chip_idx)
    run_comm("allgather_mixer0", comm_gather_all(B, ["w_in_0", "w_out_0"]), B)

    small_w = [gla_w_a1, gla_w_a2, gla_b_a, gla_norm_g]
    gs = allgather_devices("allgather_small_weights", _pack(small_w))
    per_chip = [_unpack(gs[2 * s], [a.shape for a in small_w]) for s in range(N_CHIPS)]
    w_a1 = jnp.concatenate([p[0] for p in per_chip], axis=1)
    w_a2 = jnp.concatenate([p[1] for p in per_chip], axis=2)
    b_a = jnp.concatenate([p[2] for p in per_chip], axis=1)
    gnorm = jnp.concatenate([p[3] for p in per_chip], axis=1)
    w_a1p = jnp.pad(w_a1, ((0, 0), (0, 0), (0, LOW - GATE_RANK))).astype(BF16)
    w_a2p = jnp.pad(w_a2, ((0, 0), (0, LOW - GATE_RANK), (0, 0))).astype(BF16)

    chunk_id = jnp.arange(P) // CHUNK
    mask = chunk_id[None, :] <= chunk_id[:, None]
    wm_all = jnp.where(mask[None, None], gm_w_s, 0.0)
    tri = jnp.tril(jnp.ones((CHUNK, CHUNK), F32))
    triT = tri.T

    def ici(*keys):
        return comm_gather_ici(B, list(keys))

    def fwd(*keys):
        return comm_gather_forward(B, list(keys))

    xs = x[0]
    saved = []
    for i in range(depth):
        j = i // 2
        nxt = i + 1 < depth
        c_in = ici("w_up_0") if i == 0 else comm_merge(fwd(f"w_up_{i}"), ici(f"w_down_{i}"))
        c_out = comm_merge(fwd("w_up_0"), ici("w_down_0")) if i == 0 else fwd(f"w_down_{i}")
        c_up = comm_merge(fwd("w_down_0") if i == 0 else None, ici(f"w_in_{i + 1}", f"w_out_{i + 1}") if nxt else None)
        c_down = comm_merge(fwd(f"w_in_{i + 1}", f"w_out_{i + 1}"), ici(f"w_up_{i + 1}")) if nxt else None
        if is_gm[i]:
            zp, h1 = mm_fwd_col(f"gm_in_{i}", xs, norm_mix_g[i][None], B[f"w_in_{i}"], 2 * D, comm=c_in, B=B)
            wm = wm_all[j].astype(BF16)
            gated = gm_mid_fwd(f"gm_mid_{i}", zp, gm_ln_g[j][None], gm_ln_b[j][None], wm, gm_b_s[j].T)
            x_mid = mm_fwd_row(f"gm_out_{i}", gated, B[f"w_out_{i}"], xs, comm=c_out, B=B)
            mix = (h1, zp, gated)
        else:
            proj, h1 = mm_fwd_col(f"gla_in_{i}", xs, norm_mix_g[i][None], B[f"w_in_{i}"], 3 * D, comm=c_in, B=B)
            lr = mm_plain(f"gla_low_{i}", h1, w_a1p[j], NN, BF16)
            o_raw, states = gla_scan_fwd(f"gla_scan_{i}", proj, lr, w_a2p[j], b_a[j][None], tri)
            og = gla_post_fwd(f"gla_post_{i}", o_raw, proj, gnorm[j][None])
            x_mid = mm_fwd_row(f"gla_out_{i}", og, B[f"w_out_{i}"], xs, comm=c_out, B=B)
            mix = (h1, proj, lr, o_raw, states, og)
        act, h2 = mm_fwd_col(f"ffn_up_{i}", x_mid, norm_ffn_g[i][None], B[f"w_up_{i}"], F,
                             epilogue=lambda acc: jnp.maximum(acc, 0.0), comm=c_up, B=B)
        x_out = mm_fwd_row(f"ffn_down_{i}", act, B[f"w_down_{i}"], x_mid, prologue=lambda a: a * a, comm=c_down, B=B)
        saved.append((xs, x_mid, h2, act, mix))
        xs = x_out

    loss_part, dx, dxb, d_final_g = final_loss("final_loss", xs, final_g[None], loss_target[0])
    loss = lax.psum(loss_part[0, 0], ("x", "y", "c"))

    def exchange(i, kinds):
        return comm_exchange(B, [f"d_{k}_{i}" for k in kinds], [f"r_{k}_{i}" for k in kinds])

    def scatter(i, kinds):
        return comm_scatter(B, [f"p_{k}_{i}" for k in kinds], [f"q_{k}_{i}" for k in kinds])

    def join(i, kinds):
        return comm_join(B, [f"f_{k}_{i}" for k in kinds])

    def pair_sums(i, kinds):
        for k in kinds:
            B[f"p_{k}_{i}"] = pair_add(f"pair_add_{k}_{i}", B[f"d_{k}_{i}"], B[f"r_{k}_{i}"], c_idx)

    def chip_sums(i, kinds):
        for k in kinds:
            B[f"f_{k}_{i}"] = sum_chips(f"sum_chips_{k}_{i}", B[f"p_{k}_{i}"], B[f"q_{k}_{i}"], chip_idx, c_idx)

    FFN, MIX = ("up", "down"), ("out", "in")
    d_mix_g, d_ffn_g = [None] * depth, [None] * depth
    d_ln_g, d_ln_b, d_w_s, d_b_s = [None] * n_gm, [None] * n_gm, [None] * n_gm, [None] * n_gm
    d_a1, d_a2, d_ba, d_gn = [None] * n_gla, [None] * n_gla, [None] * n_gla, [None] * n_gla
    for i in reversed(range(depth)):
        j = i // 2
        up = i + 1 < depth
        x_in, x_mid, h2, act, mix = saved[i]
        com = comm_merge(join(i + 1, FFN), exchange(i + 1, MIX)) if up else None
        d_apre = mm_bwd_row_x(f"ffn_down_dx_{i}", dxb, B[f"w_down_{i}"], mul=act, comm=com, B=B)
        if up:
            pair_sums(i + 1, MIX)
        B[f"d_down_{i}"] = mm_bwd_row_w(f"ffn_down_dw_{i}", act, dxb, prologue=lambda a: a * a,
                                        comm=scatter(i + 1, MIX) if up else None, B=B)
        if up:
            chip_sums(i + 1, MIX)
        B[f"d_up_{i}"] = mm_bwd_col_w(f"ffn_up_dw_{i}", h2, d_apre, comm=join(i + 1, MIX) if up else None, B=B)
        dh2 = mm_bwd_col_x(f"ffn_up_dx_{i}", d_apre, B[f"w_up_{i}"])
        dx, dxb, d_ffn_g[i] = rmsnorm_bwd(f"norm_ffn_bwd_{i}", x_mid, norm_ffn_g[i][None], dh2, dx)
        if is_gm[i]:
            h1, zp, gated = mix
            d_gated = mm_bwd_row_x(f"gm_out_dx_{i}", dxb, B[f"w_out_{i}"], comm=exchange(i, FFN), B=B)
            pair_sums(i, FFN)
            B[f"d_out_{i}"] = mm_bwd_row_w(f"gm_out_dw_{i}", gated, dxb)
            wm = wm_all[j].astype(BF16)
            wmT = jnp.swapaxes(wm_all[j], 1, 2).astype(BF16)
            dzp, d_ln_g[j], d_ln_b[j], dw, dbT = gm_mid_bwd(f"gm_mid_bwd_{i}", zp, d_gated, gm_ln_g[j][None], gm_ln_b[j][None],
                                                             wm, wmT, gm_b_s[j].T)
            d_w_s[j] = jnp.where(mask[None], dw, 0.0)
            d_b_s[j] = dbT.T
            B[f"d_in_{i}"] = mm_bwd_col_w(f"gm_in_dw_{i}", h1, dzp, comm=scatter(i, ("up",)), B=B)
            dh1 = mm_bwd_col_x(f"gm_in_dx_{i}", dzp, B[f"w_in_{i}"], comm=scatter(i, ("down",)), B=B)
        else:
            h1, proj, lr, o_raw, states, og = mix
            d_og = mm_bwd_row_x(f"gla_out_dx_{i}", dxb, B[f"w_out_{i}"], comm=exchange(i, FFN), B=B)
            pair_sums(i, FFN)
            B[f"d_out_{i}"] = mm_bwd_row_w(f"gla_out_dw_{i}", og, dxb)
            d_oraw, d_r, d_gn[j] = gla_post_bwd(f"gla_post_bwd_{i}", d_og, o_raw, proj, gnorm[j][None])
            dproj, dlogit, d_ba[j] = gla_scan_bwd(f"gla_scan_bwd_{i}", proj, lr, w_a2p[j], b_a[j][None], tri, triT, states, d_oraw, d_r)
            B[f"d_in_{i}"] = mm_bwd_col_w(f"gla_in_dw_{i}", h1, dproj, comm=scatter(i, ("up",)), B=B)
            dh1 = mm_bwd_col_x(f"gla_in_dx_{i}", dproj, B[f"w_in_{i}"], comm=scatter(i, ("down",)), B=B)
            dlr = mm_plain(f"gla_gate_dlow_{i}", dlogit, w_a2p[j], NT, BF16)
            d_a2[j] = mm_plain(f"gla_gate_dw2_{i}", lr, dlogit, TN, F32)[:GATE_RANK]
            d_a1[j] = mm_plain(f"gla_gate_dw1_{i}", h1, dlr, TN, F32)[:, :GATE_RANK]
            dh1 = mm_plain(f"gla_gate_dx_{i}", dlr, w_a1p[j], NT, F32, add=dh1)
        chip_sums(i, FFN)
        dx, dxb, d_mix_g[i] = rmsnorm_bwd(f"norm_mix_bwd_{i}", x_in, norm_mix_g[i][None], dh1, dx)
    grad_x = dx[None]

    small_g = [jnp.concatenate(d_mix_g), jnp.concatenate(d_ffn_g), d_final_g[0], jnp.concatenate(d_ln_g), jnp.concatenate(d_ln_b),
               jnp.stack(d_w_s), jnp.stack(d_b_s), jnp.stack(d_a1), jnp.stack(d_a2), jnp.concatenate(d_ba), jnp.concatenate(d_gn)]
    small_shapes = [(depth, D), (depth, D), (D,), (n_gm, D), (n_gm, D), (n_gm, GM_GROUPS, P, P), (n_gm, GM_GROUPS, P),
                    (n_gla, D, GATE_RANK), (n_gla, GATE_RANK, D // 2), (n_gla, D // 2), (n_gla, D)]
    B["small_g"] = _pack(small_g)

    def small_rows():
        return comm_allgather_rows(B, "small_g", "small_parts", 0, B["small_g"].shape[0], fresh=True)

    res = {}

    def big(kind, i):
        j = i // 2
        key, wmv, w_off = {"up": ("up", w_up, i * D), "down": ("down", w_down, i * D),
                           "out": ("gm_out" if is_gm[i] else "gla_o", w_mix_out[i], j * (D // 4)),
                           "in": ("gm_in" if is_gm[i] else "gla_in", w_mix_in[i], j * D)}[kind]
        w, m, v = (flat2(t) for t in wmv)
        res[key] = adamw(f"adamw_{key}_{i}", B[f"f_{kind}_{i}"], w, m, v, w_off, prev=res.get(key))

    run_comm("join_ffn0_exchange_mixer0", comm_merge(join(0, FFN), exchange(0, MIX)), B)
    pair_sums(0, MIX)
    run_comm("scatter_mixer0_allgather_small", comm_merge(scatter(0, MIX), small_rows()), B)
    chip_sums(0, MIX)
    run_comm("join_mixer0", join(0, MIX), B)
    for i in reversed(range(depth)):
        for kind in ("up", "down", "out", "in"):
            big(kind, i)

    red = _unpack(sum_devices("sum_small_grads", B["small_parts"]), small_shapes)
    g_rep = red[:7]
    g_a1 = lax.dynamic_slice_in_dim(red[7], chip * (D // 4), D // 4, axis=1)
    g_a2 = lax.dynamic_slice_in_dim(red[8], chip * (D // 8), D // 8, axis=2)
    g_ba = lax.dynamic_slice_in_dim(red[9], chip * (D // 8), D // 8, axis=1)
    g_gn = lax.dynamic_slice_in_dim(red[10], chip * (D // 4), D // 4, axis=1)
    g_small = g_rep + [g_a1, g_a2, g_ba, g_gn]
    w_small = [norm_mix_g, norm_ffn_g, final_g, gm_ln_g, gm_ln_b, gm_w_s, gm_b_s, gla_w_a1, gla_w_a2, gla_b_a, gla_norm_g]
    m_small = [m_norm_mix_g, m_norm_ffn_g, m_final_g, m_gm_ln_g, m_gm_ln_b, m_gm_w_s, m_gm_b_s, m_gla_w_a1, m_gla_w_a2, m_gla_b_a, m_gla_norm_g]
    v_small = [v_norm_mix_g, v_norm_ffn_g, v_final_g, v_gm_ln_g, v_gm_ln_b, v_gm_w_s, v_gm_b_s, v_gla_w_a1, v_gla_w_a2, v_gla_b_a, v_gla_norm_g]
    shapes_small = [w.shape for w in w_small]
    sm = adamw("adamw_small", _pack(g_small), _pack(w_small), _pack(m_small), _pack(v_small), 0)
    sm = [_unpack(o, shapes_small) for o in sm]

    def shaped(key, like):
        return [o.reshape(like.shape) for o in res[key]]

    o_gm_in, o_gm_out = shaped("gm_in", gm_w_in), shaped("gm_out", gm_w_out)
    o_gla_in, o_gla_o = shaped("gla_in", gla_w_in), shaped("gla_o", gla_w_o)
    o_up, o_down = shaped("up", ffn_w_up), shaped("down", ffn_w_down)

    def ordered(kind):
        s = sm[kind]
        return [s[0], s[1], s[2], o_gm_in[kind], s[3], s[4], s[5], s[6], o_gm_out[kind], o_gla_in[kind],
                s[7], s[8], s[9], s[10], o_gla_o[kind], o_up[kind], o_down[kind]]

    return (loss, grad_x, *ordered(0), *ordered(1), *ordered(2), *ordered(3))
```
